```python
import math
import numpy as np
import jax
import jax.numpy as jnp
from jax import lax

D_MODEL = 2048
BATCH = 1
SEQ = 8192
DEPTH = 2

N_BRANCH = 4
HG_HEADS = 4
HG_DK = 128
HG_DV = 128
S5_CH = 512
S5_GROUP = 16
S5_GROUPS = S5_CH // S5_GROUP
S5_STATE = 64
S5_DT_MIN = 0.001
S5_DT_MAX = 0.1
RET_HEADS = 4
RET_DK = 64
RET_DV = 128
RET_ROPE_BASE = 10000.0
DIFF_HEADS = 4
DIFF_DQK = 64
DIFF_DV = 128
ROPE_THETA = 500000.0
ROPE_DIM = DIFF_DQK // 4
CHUNK = 128
Q_BLOCK = 128
D_FF = 5632
LN_EPS = 1e-5
RMS_EPS = 1e-6
MASK_VALUE = -1e30
DN_ALPHA = (2 * DEPTH) ** 0.25
DN_BETA = (8 * DEPTH) ** -0.25
IN_SPLITS = (HG_HEADS * HG_DK, HG_HEADS * HG_DK, HG_HEADS * HG_DV, HG_HEADS * HG_DV,
             S5_CH,
             RET_HEADS * RET_DK, RET_HEADS * RET_DK, RET_HEADS * RET_DV, RET_HEADS * RET_DV,
             2 * DIFF_HEADS * DIFF_DQK, 2 * DIFF_HEADS * DIFF_DQK, DIFF_HEADS * DIFF_DV,
             N_BRANCH * D_MODEL)
IN_WIDTH = sum(IN_SPLITS)

kernel_name = 'hybrid_gated_mixer_trunk'

F32 = jnp.float32


def _layer_norm(x, g, b):
    xf = x.astype(F32)
    mu = jnp.mean(xf, axis=-1, keepdims=True)
    var = jnp.mean(jnp.square(xf - mu), axis=-1, keepdims=True)
    return ((xf - mu) * lax.rsqrt(var + LN_EPS) * g.astype(F32) + b.astype(F32)).astype(x.dtype)


def _head_rms_norm(o, g):
    B, L = o.shape[:2]
    of = o.astype(F32)
    of = of * lax.rsqrt(jnp.mean(jnp.square(of), axis=-1, keepdims=True) + RMS_EPS)
    return of.reshape(B, L, -1) * g.astype(F32)


def _swiglu(x, w1, w3, w2):
    return (jax.nn.silu(x @ w1) * (x @ w3)) @ w2


def _rotary(x, positions, rot_dim, theta):
    half = rot_dim // 2
    inv_freq = 1.0 / jnp.power(theta, jnp.arange(half, dtype=F32) * (2.0 / rot_dim))
    ang = positions.astype(F32)[:, :, None] * inv_freq
    cos = jnp.cos(ang)[:, :, None, :]
    sin = jnp.sin(ang)[:, :, None, :]
    xf = x.astype(F32)
    x1 = xf[..., :half]
    x2 = xf[..., half:rot_dim]
    return jnp.concatenate([x1 * cos - x2 * sin, x2 * cos + x1 * sin, xf[..., rot_dim:]], axis=-1)


def _hgrn2(q, f_logit, i, lb):
    B, L = q.shape[:2]
    H, DK, DV, C = HG_HEADS, HG_DK, HG_DV, CHUNK
    N = L // C
    q = jax.nn.silu(q.astype(F32)).reshape(B, L, H, DK)
    fl = f_logit.astype(F32).reshape(B, L, H, DK)
    lb = lb.astype(F32).reshape(H, DK)
    log_f = jax.nn.log_sigmoid(fl) + jnp.log1p(lb * jnp.exp(-fl))
    k = (1.0 - lb) * jax.nn.sigmoid(-fl)
    v = i.astype(F32).reshape(B, L, H, DV)

    def chunks(t):
        return t.reshape(B, N, C, H, t.shape[-1]).transpose(1, 0, 3, 2, 4)

    tri = (jnp.arange(C)[:, None] >= jnp.arange(C)[None, :])[:, :, None]

    def step(S, inp):
        qc, kc, vc, gc = inp
        b = jnp.cumsum(gc, axis=2)
        rel = jnp.where(tri, b[:, :, :, None, :] - b[:, :, None, :, :], 0.0)
        dec = jnp.where(tri, jnp.exp(rel), 0.0)
        a = jnp.einsum('bhtd,bhsd,bhtsd->bhts', qc, kc, dec)
        o = jnp.einsum('bhts,bhsv->bhtv', a, vc) + jnp.einsum('bhtd,bhdv->bhtv', qc * jnp.exp(b), S)
        b_end = b[:, :, -1]
        S = jnp.exp(b_end)[..., None] * S + jnp.einsum(
            'bhsd,bhsv->bhdv', kc * jnp.exp(b_end[:, :, None, :] - b), vc)
        return S, o

    S0 = jnp.zeros((B, H, DK, DV), F32)
    _, o = lax.scan(step, S0, (chunks(q), chunks(k), chunks(v), chunks(log_f)))
    return o.transpose(1, 0, 3, 2, 4).reshape(B, L, H, DV)


def _s5(u, lam_re, lam_im, log_dt, b_re, b_im, c_re, c_im, d_skip):
    B, L = u.shape[:2]
    G, CG = S5_GROUPS, S5_GROUP
    uf = u.astype(F32).reshape(B, L, G, CG)
    lam_re = lam_re.astype(F32)
    lam_im = lam_im.astype(F32)
    dt = jnp.exp(log_dt.astype(F32))[:, None]
    mag = jnp.exp(dt * lam_re)
    ab_re = mag * jnp.cos(dt * lam_im)
    ab_im = mag * jnp.sin(dt * lam_im)
    den = jnp.square(lam_re) + jnp.square(lam_im)
    nr = ab_re - 1.0
    coef_re = (nr * lam_re + ab_im * lam_im) / den
    coef_im = (ab_im * lam_re - nr * lam_im) / den
    b_re = b_re.astype(F32)
    b_im = b_im.astype(F32)
    bb_re = coef_re[..., None] * b_re - coef_im[..., None] * b_im
    bb_im = coef_re[..., None] * b_im + coef_im[..., None] * b_re
    bu_re = jnp.einsum('gpc,blgc->blgp', bb_re, uf)
    bu_im = jnp.einsum('gpc,blgc->blgp', bb_im, uf)
    a_re = jnp.broadcast_to(ab_re, bu_re.shape)
    a_im = jnp.broadcast_to(ab_im, bu_im.shape)

    def combine(e1, e2):
        a1r, a1i, b1r, b1i = e1
        a2r, a2i, b2r, b2i = e2
        return (a1r * a2r - a1i * a2i,
                a1r * a2i + a1i * a2r,
                a2r * b1r - a2i * b1i + b2r,
                a2r * b1i + a2i * b1r + b2i)

    _, _, h_re, h_im = lax.associative_scan(combine, (a_re, a_im, bu_re, bu_im), axis=1)
    y = (jnp.einsum('gcp,blgp->blgc', c_re.astype(F32), h_re)
         - jnp.einsum('gcp,blgp->blgc', c_im.astype(F32), h_im))
    y = y + d_skip.astype(F32).reshape(G, CG) * uf
    return y.reshape(B, L, G * CG)


def _retention(q, k, v, positions):
    B, L = q.shape[:2]
    H, DK, DV, C = RET_HEADS, RET_DK, RET_DV, CHUNK
    N = L // C
    q = _rotary(q.reshape(B, L, H, DK), positions, DK, RET_ROPE_BASE)
    k = _rotary(k.reshape(B, L, H, DK), positions, DK, RET_ROPE_BASE) * (DK ** -0.5)
    v = v.astype(F32).reshape(B, L, H, DV)
    log_gamma = jnp.log1p(-jnp.exp2(-5.0 - jnp.arange(H, dtype=F32)))

    def chunks(t):
        return t.reshape(B, N, C, H, t.shape[-1]).transpose(0, 3, 1, 2, 4)

    qc, kc, vc = chunks(q), chunks(k), chunks(v)
    idx = jnp.arange(C, dtype=F32)
    tri = idx[:, None] >= idx[None, :]
    rel = jnp.where(tri, idx[:, None] - idx[None, :], 0.0)
    d_intra = jnp.where(tri, jnp.exp(rel * log_gamma[:, None, None]), 0.0)
    scores = jnp.einsum('bhntd,bhnsd->bhnts', qc, kc) * d_intra[:, None]
    o = jnp.einsum('bhnts,bhnsv->bhntv', scores, vc)
    zeta = jnp.exp((C - 1.0 - idx) * log_gamma[:, None])
    u = jnp.einsum('bhnsd,hs,bhnsv->bhndv', kc, zeta, vc)
    nid = jnp.arange(N, dtype=F32)
    e = nid[:, None] - 1.0 - nid[None, :]
    m = jnp.where(e >= 0, jnp.exp(C * jnp.maximum(e, 0.0) * log_gamma[:, None, None]), 0.0)
    r = jnp.einsum('hnj,bhjdv->bhndv', m, u)
    xi = jnp.exp((idx + 1.0) * log_gamma[:, None])
    o = o + jnp.einsum('bhntd,bhndv->bhntv', qc, r) * xi[None, :, None, :, None]
    return o.transpose(0, 2, 3, 1, 4).reshape(B, L, H, DV)


def _diff_attention(q, k, v, positions, lam_q1, lam_k1, lam_q2, lam_k2, lambda_init):
    B, L = q.shape[:2]
    H, D, DV, QB = DIFF_HEADS, DIFF_DQK, DIFF_DV, Q_BLOCK
    nb = L // QB
    q = _rotary(q.reshape(B, L, 2 * H, D), positions, ROPE_DIM, ROPE_THETA) * (D ** -0.5)
    k = _rotary(k.reshape(B, L, 2 * H, D), positions, ROPE_DIM, ROPE_THETA)
    v = v.astype(F32).reshape(B, L, H, DV).transpose(0, 2, 1, 3)
    lam = (jnp.exp(jnp.sum(lam_q1.astype(F32) * lam_k1.astype(F32)))
           - jnp.exp(jnp.sum(lam_q2.astype(F32) * lam_k2.astype(F32))) + lambda_init)
    qb = q.reshape(B, nb, QB, H, 2, D).transpose(1, 0, 3, 4, 2, 5)
    kk = k.reshape(B, L, H, 2, D).transpose(0, 2, 3, 1, 4)
    kpos = jnp.arange(L)

    def block(args):
        qblk, start = args
        s = jnp.einsum('bhmqd,bhmkd->bhmqk', qblk, kk)
        qpos = start + jnp.arange(QB)
        s = jnp.where(kpos[None, :] <= qpos[:, None], s, MASK_VALUE)
        p = jax.nn.softmax(s, axis=-1)
        w = p[:, :, 0] - lam * p[:, :, 1]
        return jnp.einsum('bhqk,bhkv->bhqv', w, v)

    o = lax.map(block, (qb, jnp.arange(nb, dtype=jnp.int32) * QB))
    return o.transpose(1, 0, 3, 2, 4).reshape(B, L, H, DV)


def setup_inputs(seed: int = 0) -> dict:
    key = jax.random.key(seed)
    ks = iter(list(jax.random.split(key, 48)))
    L, D = DEPTH, D_MODEL
    G, P = S5_GROUPS, S5_STATE

    def nrm(shape, scale):
        return jax.random.normal(next(ks), shape, F32) * scale

    def gain(shape):
        return 1.0 + nrm(shape, 0.02)

    u_dt = jax.random.uniform(next(ks), (L, G), F32)
    return {
        'x': nrm((BATCH, SEQ, D), 1.0),
        'positions': jnp.broadcast_to(jnp.arange(SEQ, dtype=jnp.int32)[None, :], (BATCH, SEQ)),
        'ffa_w1': nrm((L, D, D_FF), D ** -0.5),
        'ffa_w3': nrm((L, D, D_FF), D ** -0.5),
        'ffa_w2': nrm((L, D_FF, D), DN_BETA * D_FF ** -0.5),
        'ln_a_g': gain((L, D)),
        'ln_a_b': nrm((L, D), 0.02),
        'w_in': nrm((L, D, IN_WIDTH), D ** -0.5),
        'hg_lb_logits': nrm((L, HG_HEADS * HG_DK), 1.0),
        'hg_norm_g': gain((L, HG_HEADS * HG_DV)),
        's5_lam_re': -0.5 + nrm((L, G, P), 0.01),
        's5_lam_im': math.pi * jnp.arange(P, dtype=F32)[None, None, :] + nrm((L, G, P), 0.01),
        's5_log_dt': math.log(S5_DT_MIN) + u_dt * (math.log(S5_DT_MAX) - math.log(S5_DT_MIN)),
        's5_b_re': nrm((L, G, P, S5_GROUP), (2 * S5_GROUP) ** -0.5),
        's5_b_im': nrm((L, G, P, S5_GROUP), (2 * S5_GROUP) ** -0.5),
        's5_c_re': nrm((L, G, S5_GROUP, P), P ** -0.5),
        's5_c_im': nrm((L, G, S5_GROUP, P), P ** -0.5),
        's5_d': nrm((L, S5_CH), 0.5),
        'ret_norm_g': gain((L, RET_HEADS * RET_DV)),
        'diff_lam_q1': nrm((L, DIFF_DQK), 0.1),
        'diff_lam_k1': nrm((L, DIFF_DQK), 0.1),
        'diff_lam_q2': nrm((L, DIFF_DQK), 0.1),
        'diff_lam_k2': nrm((L, DIFF_DQK), 0.1),
        'diff_norm_g': gain((L, DIFF_HEADS * DIFF_DV)),
        'w_up_hg': nrm((L, HG_HEADS * HG_DV, D), (HG_HEADS * HG_DV) ** -0.5),
        'w_up_s5': nrm((L, S5_CH, 2 * D), S5_CH ** -0.5),
        'w_up_ret': nrm((L, RET_HEADS * RET_DV, D), (RET_HEADS * RET_DV) ** -0.5),
        'w_up_diff': nrm((L, DIFF_HEADS * DIFF_DV, D), (DIFF_HEADS * DIFF_DV) ** -0.5),
        'w_out': nrm((L, D, D), DN_BETA * D ** -0.5),
        'ln_m_g': gain((L, D)),
        'ln_m_b': nrm((L, D), 0.02),
        'ffb_w1': nrm((L, D, D_FF), D ** -0.5),
        'ffb_w3': nrm((L, D, D_FF), D ** -0.5),
        'ffb_w2': nrm((L, D_FF, D), DN_BETA * D_FF ** -0.5),
        'ln_b_g': gain((L, D)),
        'ln_b_b': nrm((L, D), 0.02),
    }


def reference(x, positions, ffa_w1, ffa_w3, ffa_w2, ln_a_g, ln_a_b, w_in,
              hg_lb_logits, hg_norm_g, s5_lam_re, s5_lam_im, s5_log_dt, s5_b_re, s5_b_im,
              s5_c_re, s5_c_im, s5_d, ret_norm_g, diff_lam_q1, diff_lam_k1, diff_lam_q2,
              diff_lam_k2, diff_norm_g, w_up_hg, w_up_s5, w_up_ret, w_up_diff, w_out,
              ln_m_g, ln_m_b, ffb_w1, ffb_w3, ffb_w2, ln_b_g, ln_b_b):
    B, L, D = x.shape
    p_lb = jax.nn.softmax(hg_lb_logits.astype(F32), axis=0)
    lower_bounds = jnp.maximum(jnp.cumsum(p_lb, axis=0) - p_lb[0], 0.0)
    offsets = np.cumsum(np.array(IN_SPLITS))[:-1].tolist()
    for l in range(DEPTH):
        x = _layer_norm(DN_ALPHA * x + 0.5 * _swiglu(x, ffa_w1[l], ffa_w3[l], ffa_w2[l]),
                        ln_a_g[l], ln_a_b[l])
        (hq, hf, hi, hg, su, rq, rk, rv, rg, dq, dk, dv, gl) = jnp.split(x @ w_in[l], offsets, axis=-1)
        o_hg = _hgrn2(hq, hf, hi, lower_bounds[l])
        y_hg = _head_rms_norm(o_hg, hg_norm_g[l]) * jax.nn.silu(hg.astype(F32))
        up_hg = y_hg.astype(x.dtype) @ w_up_hg[l]
        y_s5 = _s5(su, s5_lam_re[l], s5_lam_im[l], s5_log_dt[l], s5_b_re[l], s5_b_im[l],
                   s5_c_re[l], s5_c_im[l], s5_d[l])
        z_a, z_b = jnp.split(jax.nn.gelu(y_s5).astype(x.dtype) @ w_up_s5[l], 2, axis=-1)
        up_s5 = z_a * jax.nn.sigmoid(z_b)
        o_r = _retention(rq, rk, rv, positions)
        y_r = _head_rms_norm(o_r, ret_norm_g[l]) * jax.nn.silu(rg.astype(F32))
        up_ret = y_r.astype(x.dtype) @ w_up_ret[l]
        lambda_init = 0.8 - 0.6 * math.exp(-0.3 * l)
        o_d = _diff_attention(dq, dk, dv, positions, diff_lam_q1[l], diff_lam_k1[l],
                              diff_lam_q2[l], diff_lam_k2[l], lambda_init)
        y_d = _head_rms_norm(o_d, diff_norm_g[l]) * (1.0 - lambda_init)
        up_diff = y_d.astype(x.dtype) @ w_up_diff[l]
        gates = jax.nn.sigmoid(gl.astype(F32)).reshape(B, L, N_BRANCH, D)
        merged = (gates[:, :, 0] * up_hg + gates[:, :, 1] * up_s5
                  + gates[:, :, 2] * up_ret + gates[:, :, 3] * up_diff)
        x = _layer_norm(DN_ALPHA * x + merged.astype(x.dtype) @ w_out[l], ln_m_g[l], ln_m_b[l])
        x = _layer_norm(DN_ALPHA * x + 0.5 * _swiglu(x, ffb_w1[l], ffb_w3[l], ffb_w2[l]),
                        ln_b_g[l], ln_b_b[l])
    return x
```

```python
import functools
import math

import jax
import jax.numpy as jnp
from jax import lax
from jax.experimental import pallas as pl
from jax.experimental.pallas import tpu as pltpu

F32 = jnp.float32
BF16 = jnp.bfloat16

D_MODEL = 2048
DEPTH = 2
N_BRANCH = 4
HG_HEADS, HG_DK, HG_DV = 4, 128, 128
S5_CH, S5_GROUP, S5_STATE = 512, 16, 64
S5_GROUPS = S5_CH // S5_GROUP
RET_HEADS, RET_DK, RET_DV = 4, 64, 128
RET_ROPE_BASE = 10000.0
DIFF_HEADS, DIFF_DQK, DIFF_DV = 4, 64, 128
ROPE_THETA = 500000.0
ROPE_DIM = DIFF_DQK // 4
D_FF = 5632
LN_EPS = 1e-5
RMS_EPS = 1e-6
MASK_VALUE = -1e30
DN_ALPHA = (2 * DEPTH) ** 0.25

OFF_HQ, OFF_HF, OFF_HI, OFF_HG = 0, 512, 1024, 1536
OFF_SU = 2048
OFF_RQ, OFF_RK, OFF_RV, OFF_RG = 2560, 2816, 3072, 3584
OFF_DQ, OFF_DK, OFF_DV = 4096, 4608, 5120
OFF_GL = 5632
MIX_WIDTH = OFF_GL
GATE_WIDTH = N_BRANCH * D_MODEL

VMEM_LIMIT_BYTES = 56 * 1024 * 1024

S5_T = 16
HG_SUB = 16
HG_TC = 512
RET_C = 128
ATT_T = 256


def _cparams(sem):
    return pltpu.CompilerParams(dimension_semantics=sem, vmem_limit_bytes=VMEM_LIMIT_BYTES)


def _layer_norm(y, g, b):
    mu = jnp.mean(y, axis=-1, keepdims=True)
    d = y - mu
    var = jnp.mean(d * d, axis=-1, keepdims=True)
    return d * lax.rsqrt(var + LN_EPS) * g + b


def _rms(o):
    return o * lax.rsqrt(jnp.mean(o * o, axis=-1, keepdims=True) + RMS_EPS)


def _silu(x):
    return x * jax.nn.sigmoid(x)


def _dot(a, b):
    return jnp.dot(a, b, preferred_element_type=F32)


def _dot_nt(a, b):
    return lax.dot_general(a, b, (((1,), (1,)), ((), ())), preferred_element_type=F32)


def _dot_tn(a, b):
    return lax.dot_general(a, b, (((0,), (0,)), ((), ())), preferred_element_type=F32)


def _ffn_kernel(x_ref, w1_ref, w3_ref, w2_ref, g_ref, b_ref, o_ref, acc_ref, xb_ref):
    f = pl.program_id(1)

    @pl.when(f == 0)
    def _():
        xb_ref[...] = x_ref[...].astype(BF16)
        acc_ref[...] = jnp.zeros_like(acc_ref)

    xb = xb_ref[...]
    h1 = _dot(xb, w1_ref[...])
    h3 = _dot(xb, w3_ref[...])
    h = (_silu(h1) * h3).astype(BF16)
    acc_ref[...] += _dot(h, w2_ref[...])

    @pl.when(f == pl.num_programs(1) - 1)
    def _():
        y = DN_ALPHA * x_ref[...] + 0.5 * acc_ref[...]
        o_ref[...] = _layer_norm(y, g_ref[...], b_ref[...])


def _ffn_ln(x, w1, w3, w2, g, b, *, tm=512, tf=512):
    s, d = x.shape
    f = w1.shape[1]
    tm = min(tm, s)
    return pl.pallas_call(
        _ffn_kernel,
        out_shape=jax.ShapeDtypeStruct((s, d), F32),
        grid=(s // tm, f // tf),
        in_specs=[
            pl.BlockSpec((tm, d), lambda i, j: (i, 0)),
            pl.BlockSpec((d, tf), lambda i, j: (0, j)),
            pl.BlockSpec((d, tf), lambda i, j: (0, j)),
            pl.BlockSpec((tf, d), lambda i, j: (j, 0)),
            pl.BlockSpec((1, d), lambda i, j: (0, 0)),
            pl.BlockSpec((1, d), lambda i, j: (0, 0)),
        ],
        out_specs=pl.BlockSpec((tm, d), lambda i, j: (i, 0)),
        scratch_shapes=[pltpu.VMEM((tm, d), F32), pltpu.VMEM((tm, d), BF16)],
        compiler_params=_cparams(("parallel", "arbitrary")),
        name="ffn_ln",
    )(x, w1, w3, w2, g.reshape(1, d), b.reshape(1, d))


def _proj_kernel(x_ref, w_ref, o_ref, xb_ref, *, gate):
    @pl.when(pl.program_id(1) == 0)
    def _():
        xb_ref[...] = x_ref[...].astype(BF16)

    y = _dot(xb_ref[...], w_ref[...])
    if gate:
        y = jax.nn.sigmoid(y)
    o_ref[...] = y.astype(o_ref.dtype)


def _in_proj(x, w_in, col0, width, out_dtype, gate, *, tm=1024, tn=512):
    s, d = x.shape
    tm = min(tm, s)
    cb0 = col0 // tn
    return pl.pallas_call(
        functools.partial(_proj_kernel, gate=gate),
        out_shape=jax.ShapeDtypeStruct((s, width), out_dtype),
        grid=(s // tm, width // tn),
        in_specs=[
            pl.BlockSpec((tm, d), lambda i, j: (i, 0)),
            pl.BlockSpec((d, tn), lambda i, j: (0, cb0 + j)),
        ],
        out_specs=pl.BlockSpec((tm, tn), lambda i, j: (i, j)),
        scratch_shapes=[pltpu.VMEM((tm, d), BF16)],
        compiler_params=_cparams(("parallel", "arbitrary")),
        name="in_proj_gate" if gate else "in_proj_mix",
    )(x, w_in)


def _cumsum_rows(x, row):
    n = x.shape[0]
    shift = 1
    while shift < n:
        x = x + jnp.where(row >= shift, pltpu.roll(x, shift, axis=0), 0.0)
        shift *= 2
    return x


def _hgrn2_kernel(q_ref, f_ref, i_ref, g_ref, lb_ref, gain_ref, o_ref, st_ref, *, sub):
    @pl.when(pl.program_id(1) == 0)
    def _():
        st_ref[...] = jnp.zeros_like(st_ref)

    lb = lb_ref[...]
    gain = gain_ref[...]
    row = lax.broadcasted_iota(jnp.int32, (sub, HG_DK), 0)
    n_sub = q_ref.shape[0] // sub

    def body(c, carry):
        r0 = pl.multiple_of(c * sub, sub)
        rows = pl.ds(r0, sub)
        fl = f_ref[rows, :]
        q = _silu(q_ref[rows, :])
        v = i_ref[rows, :]
        log_f = jax.nn.log_sigmoid(fl) + jnp.log1p(lb * jnp.exp(-fl))
        k = (1.0 - lb) * jax.nn.sigmoid(-fl)
        b = _cumsum_rows(log_f, row)

        st = st_ref[...]
        o = _dot_nt((q * jnp.exp(b)).astype(BF16), st.astype(BF16))
        for s_ in range(sub):
            tri = row >= s_
            dec = jnp.where(tri, jnp.exp(jnp.where(tri, b - b[s_:s_ + 1, :], 0.0)), 0.0)
            a = jnp.sum(q * k[s_:s_ + 1, :] * dec, axis=-1, keepdims=True)
            o = o + a * v[s_:s_ + 1, :]

        b_end = b[sub - 1:sub, :]
        kd = (k * jnp.exp(b_end - b)).astype(BF16)
        st_ref[...] = st * jnp.exp(b_end) + _dot_tn(v.astype(BF16), kd)

        y = _rms(o) * gain * _silu(g_ref[rows, :])
        o_ref[rows, :] = y.astype(o_ref.dtype)
        return carry

    lax.fori_loop(0, n_sub, body, 0)


def _hgrn2(proj, lb, gain, *, tc=HG_TC, sub=HG_SUB):
    s = proj.shape[0]
    tc = min(tc, s)

    def col(off):
        return pl.BlockSpec((tc, HG_DK), lambda h, t: (t, off // HG_DK + h))

    return pl.pallas_call(
        functools.partial(_hgrn2_kernel, sub=sub),
        out_shape=jax.ShapeDtypeStruct((s, HG_HEADS * HG_DV), BF16),
        grid=(HG_HEADS, s // tc),
        in_specs=[col(OFF_HQ), col(OFF_HF), col(OFF_HI), col(OFF_HG),
                  pl.BlockSpec((1, HG_DK), lambda h, t: (0, h)),
                  pl.BlockSpec((1, HG_DV), lambda h, t: (0, h))],
        out_specs=pl.BlockSpec((tc, HG_DV), lambda h, t: (t, h)),
        scratch_shapes=[pltpu.VMEM((HG_DV, HG_DK), F32)],
        compiler_params=_cparams(("parallel", "arbitrary")),
        name="hgrn2",
    )(proj, proj, proj, proj, lb.reshape(1, -1), gain.reshape(1, -1))


def _s5_tables(lam_re, lam_im, log_dt, b_re, b_im, c_re, c_im, d_skip):
    hi = lax.Precision.HIGHEST
    g, p, cg, t = S5_GROUPS, S5_STATE, S5_GROUP, S5_T
    dt = jnp.exp(log_dt.astype(F32))[:, None]
    lam_re = lam_re.astype(F32)
    lam_im = lam_im.astype(F32)
    mag = jnp.exp(dt * lam_re)
    ab_re = mag * jnp.cos(dt * lam_im)
    ab_im = mag * jnp.sin(dt * lam_im)
    den = jnp.square(lam_re) + jnp.square(lam_im)
    nr = ab_re - 1.0
    coef_re = (nr * lam_re + ab_im * lam_im) / den
    coef_im = (ab_im * lam_re - nr * lam_im) / den
    b_re = b_re.astype(F32)
    b_im = b_im.astype(F32)
    bb_re = coef_re[..., None] * b_re - coef_im[..., None] * b_im
    bb_im = coef_re[..., None] * b_im + coef_im[..., None] * b_re
    c_re = c_re.astype(F32)
    c_im = c_im.astype(F32)
    tau = jnp.arange(t + 1, dtype=F32)[:, None, None]
    pmag = jnp.exp(tau * (dt * lam_re)[None])
    pw_re = pmag * jnp.cos(tau * (dt * lam_im)[None])
    pw_im = pmag * jnp.sin(tau * (dt * lam_im)[None])
    lb_re = pw_re[:t, :, :, None] * bb_re[None] - pw_im[:t, :, :, None] * bb_im[None]
    lb_im = pw_re[:t, :, :, None] * bb_im[None] + pw_im[:t, :, :, None] * bb_re[None]
    ktau = (jnp.einsum('gcp,tgpd->tgcd', c_re, lb_re, precision=hi)
            - jnp.einsum('gcp,tgpd->tgcd', c_im, lb_im, precision=hi))
    idx = jnp.arange(t)
    diff = idx[None, :] - idx[:, None]
    toep = jnp.where((diff >= 0)[:, :, None, None, None],
                     ktau[jnp.maximum(diff, 0)], 0.0)
    m_op = toep.transpose(2, 0, 4, 1, 3).reshape(g, t * cg, t * cg)
    rev = lb_re[::-1], lb_im[::-1]
    p_op = jnp.concatenate([rev[0].transpose(1, 0, 3, 2), rev[1].transpose(1, 0, 3, 2)],
                           axis=-1).reshape(g, t * cg, 2 * p)
    cl_re = c_re[None] * pw_re[1:, :, None, :] - c_im[None] * pw_im[1:, :, None, :]
    cl_im = c_re[None] * pw_im[1:, :, None, :] + c_im[None] * pw_re[1:, :, None, :]
    q_op = jnp.concatenate([cl_re.transpose(1, 3, 0, 2), -cl_im.transpose(1, 3, 0, 2)],
                           axis=1).reshape(g, 2 * p, t * cg)
    a_blk = jnp.concatenate([pw_re[t], pw_im[t]], axis=-1).reshape(g, 1, 2 * p)
    d_blk = jnp.tile(d_skip.astype(F32).reshape(g, 1, cg), (1, t, 1)).reshape(g, 1, t * cg)
    return m_op.astype(BF16), p_op.astype(BF16), q_op.astype(BF16), a_blk, d_blk


def _s5_kernel(u_ref, m_ref, p_ref, q_ref, a_ref, d_ref, o_ref):
    u = u_ref[...]
    ub = u.astype(BF16)
    n = u.shape[0]
    half = S5_STATE
    h = _dot(ub, p_ref[...])
    row = lax.broadcasted_iota(jnp.int32, h.shape, 0)
    lane = lax.broadcasted_iota(jnp.int32, (1, 2 * half), 1)
    a = a_ref[...]
    shift = 1
    while shift < n:
        a_sw = pltpu.roll(a, half, axis=1)
        a1 = jnp.where(lane < half, a, a_sw)
        a2 = jnp.where(lane < half, -a_sw, a)
        x = jnp.where(row >= shift, pltpu.roll(h, shift, axis=0), 0.0)
        h = h + a1 * x + a2 * pltpu.roll(x, half, axis=1)
        a = a1 * a + a2 * a_sw
        shift *= 2
    h_prev = jnp.where(row >= 1, pltpu.roll(h, 1, axis=0), 0.0)
    y = _dot(ub, m_ref[...]) + _dot(h_prev.astype(BF16), q_ref[...]) + d_ref[...] * u
    o_ref[...] = jax.nn.gelu(y, approximate=True).astype(o_ref.dtype)


def _s5(su, tables):
    s = su.shape[0]
    g, cg, t = S5_GROUPS, S5_GROUP, S5_T
    n = s // t
    w = t * cg
    u = su.reshape(n, t, g, cg).transpose(2, 0, 1, 3).reshape(g, n, w)
    m_op, p_op, q_op, a_blk, d_blk = tables

    def per_group(shape):
        return pl.BlockSpec((None,) + shape, lambda i: (i, 0, 0))

    y = pl.pallas_call(
        _s5_kernel,
        out_shape=jax.ShapeDtypeStruct((g, n, w), BF16),
        grid=(g,),
        in_specs=[per_group((n, w)), per_group((w, w)), per_group((w, 2 * S5_STATE)),
                  per_group((2 * S5_STATE, w)), per_group((1, 2 * S5_STATE)), per_group((1, w))],
        out_specs=per_group((n, w)),
        compiler_params=_cparams(("parallel",)),
        name="s5",
    )(u, m_op, p_op, q_op, a_blk, d_blk)
    return y.reshape(g, n, t, cg).transpose(1, 2, 0, 3).reshape(s, g * cg)


def _rope_kernel(pos_ref, fr_ref, fd_ref, cr_ref, sr_ref, cd_ref, sd_ref):
    pos = pos_ref[...].astype(F32)
    ang_r = pos * fr_ref[...]
    lane_r = lax.broadcasted_iota(jnp.int32, ang_r.shape, 1)
    cr_ref[...] = jnp.cos(ang_r)
    sr_ref[...] = jnp.where(lane_r % RET_DK < RET_DK // 2, -1.0, 1.0) * jnp.sin(ang_r)
    ang_d = pos * fd_ref[...]
    lane_d = lax.broadcasted_iota(jnp.int32, ang_d.shape, 1)
    cd_ref[...] = jnp.cos(ang_d)
    sd_ref[...] = jnp.where(lane_d % DIFF_DQK < ROPE_DIM // 2, -1.0, 1.0) * jnp.sin(ang_d)


def _rope_tables(positions, *, tb=512):
    s = positions.shape[0]
    tb = min(tb, s)
    half_r = RET_DK // 2
    inv_r = 1.0 / jnp.power(RET_ROPE_BASE, jnp.arange(half_r, dtype=F32) * (2.0 / RET_DK))
    fr = jnp.tile(inv_r, RET_HEADS * RET_DK // half_r).reshape(1, RET_HEADS * RET_DK)
    half_d = ROPE_DIM // 2
    inv_d = 1.0 / jnp.power(ROPE_THETA, jnp.arange(half_d, dtype=F32) * (2.0 / ROPE_DIM))
    fd_head = jnp.concatenate([inv_d, inv_d, jnp.zeros((DIFF_DQK - ROPE_DIM,), F32)])
    fd = jnp.tile(fd_head, 2).reshape(1, 2 * DIFF_DQK)
    wr, wd = fr.shape[1], fd.shape[1]
    return pl.pallas_call(
        _rope_kernel,
        out_shape=[jax.ShapeDtypeStruct((s, wr), F32), jax.ShapeDtypeStruct((s, wr), F32),
                   jax.ShapeDtypeStruct((s, wd), F32), jax.ShapeDtypeStruct((s, wd), F32)],
        grid=(s // tb,),
        in_specs=[pl.BlockSpec((tb, 1), lambda i: (i, 0)),
                  pl.BlockSpec((1, wr), lambda i: (0, 0)),
                  pl.BlockSpec((1, wd), lambda i: (0, 0))],
        out_specs=[pl.BlockSpec((tb, wr), lambda i: (i, 0)), pl.BlockSpec((tb, wr), lambda i: (i, 0)),
                   pl.BlockSpec((tb, wd), lambda i: (i, 0)), pl.BlockSpec((tb, wd), lambda i: (i, 0))],
        compiler_params=_cparams(("parallel",)),
        name="rope_tables",
    )(positions.reshape(s, 1), fr, fd)


def _swap_halves(x, group, half):
    n = x.shape[-1]
    lane = lax.broadcasted_iota(jnp.int32, x.shape, x.ndim - 1)
    return jnp.where(lane % group < half,
                     pltpu.roll(x, n - half, axis=x.ndim - 1),
                     pltpu.roll(x, half, axis=x.ndim - 1))


def _ret_kernel(q_ref, k_ref, v_ref, g_ref, cos_ref, sin_ref, gain_ref, o_ref, r_ref):
    @pl.when(pl.program_id(0) == 0)
    def _():
        r_ref[...] = jnp.zeros_like(r_ref)

    c = q_ref.shape[0]
    cos = cos_ref[...]
    sin = sin_ref[...]
    q = q_ref[...]
    k = k_ref[...]
    q = q * cos + _swap_halves(q, RET_DK, RET_DK // 2) * sin
    k = (k * cos + _swap_halves(k, RET_DK, RET_DK // 2) * sin) * (RET_DK ** -0.5)
    ti = lax.broadcasted_iota(jnp.int32, (c, c), 0)
    si = lax.broadcasted_iota(jnp.int32, (c, c), 1)
    tri = ti >= si
    rel = jnp.where(tri, ti - si, 0).astype(F32)
    idx = lax.broadcasted_iota(jnp.int32, (c, 1), 0).astype(F32)
    gain = gain_ref[...]
    for h in range(RET_HEADS):
        log_gamma = math.log1p(-(2.0 ** (-5.0 - h)))
        qh = q[:, h * RET_DK:(h + 1) * RET_DK]
        kh = k[:, h * RET_DK:(h + 1) * RET_DK]
        vh = v_ref[:, h * RET_DV:(h + 1) * RET_DV]
        d_intra = jnp.where(tri, jnp.exp(rel * log_gamma), 0.0)
        scores = _dot_nt(qh.astype(BF16), kh.astype(BF16)) * d_intra
        r = r_ref[h]
        xi = jnp.exp((idx + 1.0) * log_gamma)
        o = _dot(scores.astype(BF16), vh.astype(BF16)) + _dot(qh.astype(BF16), r.astype(BF16)) * xi
        zeta = jnp.exp((c - 1.0 - idx) * log_gamma)
        r_ref[h] = math.exp(c * log_gamma) * r + _dot_tn((kh * zeta).astype(BF16), vh.astype(BF16))
        sl = slice(h * RET_DV, (h + 1) * RET_DV)
        y = _rms(o) * gain[:, sl] * _silu(g_ref[:, sl])
        o_ref[:, sl] = y.astype(o_ref.dtype)


def _retention(proj, cos_r, sin_r, gain, *, c=RET_C):
    s = proj.shape[0]
    c = min(c, s)
    wq, wv = RET_HEADS * RET_DK, RET_HEADS * RET_DV
    return pl.pallas_call(
        _ret_kernel,
        out_shape=jax.ShapeDtypeStruct((s, wv), BF16),
        grid=(s // c,),
        in_specs=[pl.BlockSpec((c, wq), lambda i: (i, OFF_RQ // wq)),
                  pl.BlockSpec((c, wq), lambda i: (i, OFF_RK // wq)),
                  pl.BlockSpec((c, wv), lambda i: (i, OFF_RV // wv)),
                  pl.BlockSpec((c, wv), lambda i: (i, OFF_RG // wv)),
                  pl.BlockSpec((c, wq), lambda i: (i, 0)),
                  pl.BlockSpec((c, wq), lambda i: (i, 0)),
                  pl.BlockSpec((1, wv), lambda i: (0, 0))],
        out_specs=pl.BlockSpec((c, wv), lambda i: (i, 0)),
        scratch_shapes=[pltpu.VMEM((RET_HEADS, RET_DK, RET_DV), F32)],
        compiler_params=_cparams(("arbitrary",)),
        name="retention",
    )(proj, proj, proj, proj, cos_r, sin_r, gain.reshape(1, -1))


def _diff_prep_kernel(q_ref, k_ref, v_ref, cos_ref, sin_ref, qo_ref, ko_ref, vo_ref):
    reps = q_ref.shape[1] // cos_ref.shape[1]
    cos = jnp.concatenate([cos_ref[...]] * reps, axis=-1)
    sin = jnp.concatenate([sin_ref[...]] * reps, axis=-1)
    q = q_ref[...]
    k = k_ref[...]
    q = (q * cos + _swap_halves(q, DIFF_DQK, ROPE_DIM // 2) * sin) * (DIFF_DQK ** -0.5)
    k = k * cos + _swap_halves(k, DIFF_DQK, ROPE_DIM // 2) * sin
    qo_ref[...] = q.astype(BF16)
    ko_ref[...] = k.astype(BF16)
    vo_ref[...] = v_ref[...].astype(BF16)


def _diff_prep(proj, cos_d, sin_d, *, tb=512):
    s = proj.shape[0]
    tb = min(tb, s)
    w = 2 * DIFF_HEADS * DIFF_DQK
    wt = cos_d.shape[1]
    out = jax.ShapeDtypeStruct((s, w), BF16)
    return pl.pallas_call(
        _diff_prep_kernel,
        out_shape=[out, out, out],
        grid=(s // tb,),
        in_specs=[pl.BlockSpec((tb, w), lambda i: (i, OFF_DQ // w)),
                  pl.BlockSpec((tb, w), lambda i: (i, OFF_DK // w)),
                  pl.BlockSpec((tb, w), lambda i: (i, OFF_DV // w)),
                  pl.BlockSpec((tb, wt), lambda i: (i, 0)),
                  pl.BlockSpec((tb, wt), lambda i: (i, 0))],
        out_specs=[pl.BlockSpec((tb, w), lambda i: (i, 0))] * 3,
        compiler_params=_cparams(("parallel",)),
        name="diff_prep",
    )(proj, proj, proj, cos_d, sin_d)


def _diff_attn_kernel(q_ref, k_ref, v_ref, lq1_ref, lk1_ref, lq2_ref, lk2_ref, gain_ref, o_ref,
                      m_ref, l_ref, acc_ref, *, lambda_init):
    i = pl.program_id(1)
    t = q_ref.shape[0]
    d = DIFF_DQK
    q = q_ref[...]
    m_ref[...] = jnp.full_like(m_ref, MASK_VALUE)
    l_ref[...] = jnp.zeros_like(l_ref)
    acc_ref[...] = jnp.zeros_like(acc_ref)

    def step(j, masked):
        r0 = pl.multiple_of(j * t, t)
        kb = k_ref[pl.ds(r0, t), :]
        vb = v_ref[pl.ds(r0, t), :]
        for mp in range(2):
            s = _dot_nt(q[:, mp * d:(mp + 1) * d], kb[:, mp * d:(mp + 1) * d])
            if masked:
                qi = lax.broadcasted_iota(jnp.int32, (t, t), 0)
                ki = lax.broadcasted_iota(jnp.int32, (t, t), 1)
                s = jnp.where(ki <= qi, s, MASK_VALUE)
            m_old = m_ref[mp]
            m_new = jnp.maximum(m_old, jnp.max(s, axis=-1, keepdims=True))
            alpha = jnp.exp(m_old - m_new)
            p = jnp.exp(s - m_new)
            l_ref[mp] = alpha * l_ref[mp] + jnp.sum(p, axis=-1, keepdims=True)
            acc_ref[mp] = alpha * acc_ref[mp] + _dot(p.astype(BF16), vb)
            m_ref[mp] = m_new

    def full_step(j, carry):
        step(j, False)
        return carry

    lax.fori_loop(0, i, full_step, 0)
    step(i, True)

    lam = (jnp.exp(jnp.sum(lq1_ref[...] * lk1_ref[...], axis=-1, keepdims=True))
           - jnp.exp(jnp.sum(lq2_ref[...] * lk2_ref[...], axis=-1, keepdims=True)) + lambda_init)
    o = acc_ref[0] / l_ref[0] - lam * (acc_ref[1] / l_ref[1])
    y = _rms(o) * gain_ref[...] * (1.0 - lambda_init)
    o_ref[...] = y.astype(o_ref.dtype)


def _diff_attention(qr, kr, vb, lq1, lk1, lq2, lk2, gain, lambda_init, *, t=ATT_T):
    s = qr.shape[0]
    t = min(t, s)
    wh = 2 * DIFF_DQK
    lam_spec = pl.BlockSpec((1, DIFF_DQK), lambda h, i: (0, 0))
    return pl.pallas_call(
        functools.partial(_diff_attn_kernel, lambda_init=lambda_init),
        out_shape=jax.ShapeDtypeStruct((s, DIFF_HEADS * DIFF_DV), BF16),
        grid=(DIFF_HEADS, s // t),
        in_specs=[pl.BlockSpec((t, wh), lambda h, i: (i, h)),
                  pl.BlockSpec((s, wh), lambda h, i: (0, h)),
                  pl.BlockSpec((s, DIFF_DV), lambda h, i: (0, h)),
                  lam_spec, lam_spec, lam_spec, lam_spec,
                  pl.BlockSpec((1, DIFF_DV), lambda h, i: (0, h))],
        out_specs=pl.BlockSpec((t, DIFF_DV), lambda h, i: (i, h)),
        scratch_shapes=[pltpu.VMEM((2, t, 1), F32), pltpu.VMEM((2, t, 1), F32),
                        pltpu.VMEM((2, t, DIFF_DV), F32)],
        compiler_params=_cparams(("parallel", "arbitrary")),
        name="diff_attention",
    )(qr, kr, vb, lq1.reshape(1, -1), lk1.reshape(1, -1), lq2.reshape(1, -1), lk2.reshape(1, -1),
      gain.reshape(1, -1))


def _merge_kernel(x_ref, yh_ref, ys_ref, yr_ref, yd_ref, gt_ref, wh_ref, ws_ref, wr_ref, wd_ref,
                  wo_ref, g_ref, b_ref, o_ref, mg_ref, *, tn):
    d = D_MODEL
    yh = yh_ref[...]
    ys = ys_ref[...]
    yr = yr_ref[...]
    yd = yd_ref[...]
    for n in range(d // tn):
        c = slice(n * tn, (n + 1) * tn)
        c2 = slice(d + n * tn, d + (n + 1) * tn)
        up_h = _dot(yh, wh_ref[:, c])
        up_s = _dot(ys, ws_ref[:, c]) * jax.nn.sigmoid(_dot(ys, ws_ref[:, c2]))
        up_r = _dot(yr, wr_ref[:, c])
        up_d = _dot(yd, wd_ref[:, c])

        def gate(b):
            return gt_ref[:, b * d + n * tn:b * d + (n + 1) * tn].astype(F32)

        mg = gate(0) * up_h + gate(1) * up_s + gate(2) * up_r + gate(3) * up_d
        mg_ref[:, c] = mg.astype(BF16)
    y = DN_ALPHA * x_ref[...] + _dot(mg_ref[...], wo_ref[...])
    o_ref[...] = _layer_norm(y, g_ref[...], b_ref[...])


def _merge(x, yh, ys, yr, yd, gates, wh, ws, wr, wd, wo, g, b, *, tm=256, tn=512):
    s, d = x.shape
    tm = min(tm, s)
    wy = yh.shape[1]

    def rows(width):
        return pl.BlockSpec((tm, width), lambda i: (i, 0))

    def whole(arr):
        return pl.BlockSpec(arr.shape, lambda i: (0, 0), pipeline_mode=pl.Buffered(1))

    g2, b2 = g.reshape(1, d), b.reshape(1, d)
    return pl.pallas_call(
        functools.partial(_merge_kernel, tn=tn),
        out_shape=jax.ShapeDtypeStruct((s, d), F32),
        grid=(s // tm,),
        in_specs=[rows(d), rows(wy), rows(wy), rows(wy), rows(wy), rows(GATE_WIDTH),
                  whole(wh), whole(ws), whole(wr), whole(wd), whole(wo), whole(g2), whole(b2)],
        out_specs=rows(d),
        scratch_shapes=[pltpu.VMEM((tm, d), BF16)],
        compiler_params=_cparams(("parallel",)),
        name="merge_out_ln",
    )(x, yh, ys, yr, yd, gates, wh, ws, wr, wd, wo, g2, b2)


def kernel(x, positions, ffa_w1, ffa_w3, ffa_w2, ln_a_g, ln_a_b, w_in, hg_lb_logits, hg_norm_g, s5_lam_re, s5_lam_im, s5_log_dt, s5_b_re, s5_b_im, s5_c_re, s5_c_im, s5_d, ret_norm_g, diff_lam_q1, diff_lam_k1, diff_lam_q2, diff_lam_k2, diff_norm_g, w_up_hg, w_up_s5, w_up_ret, w_up_diff, w_out, ln_m_g, ln_m_b, ffb_w1, ffb_w3, ffb_w2, ln_b_g, ln_b_b):
    bsz, s, d = x.shape
    assert bsz == 1 and d == D_MODEL
    depth = w_in.shape[0]
    x = x.reshape(s, d)

    p_lb = jax.nn.softmax(hg_lb_logits.astype(F32), axis=0)
    lower_bounds = jnp.maximum(jnp.cumsum(p_lb, axis=0) - p_lb[0], 0.0)
    cos_r, sin_r, cos_d, sin_d = _rope_tables(positions.reshape(s))

    for l in range(depth):
        x = _ffn_ln(x, ffa_w1[l].astype(BF16), ffa_w3[l].astype(BF16), ffa_w2[l].astype(BF16),
                    ln_a_g[l], ln_a_b[l])
        w_in_l = w_in[l].astype(BF16)
        proj = _in_proj(x, w_in_l, 0, MIX_WIDTH, F32, gate=False)
        gates = _in_proj(x, w_in_l, OFF_GL, GATE_WIDTH, BF16, gate=True)

        y_hg = _hgrn2(proj, lower_bounds[l], hg_norm_g[l])
        tables = _s5_tables(s5_lam_re[l], s5_lam_im[l], s5_log_dt[l], s5_b_re[l], s5_b_im[l],
                            s5_c_re[l], s5_c_im[l], s5_d[l])
        y_s5 = _s5(proj[:, OFF_SU:OFF_SU + S5_CH], tables)
        y_r = _retention(proj, cos_r, sin_r, ret_norm_g[l])
        lambda_init = 0.8 - 0.6 * math.exp(-0.3 * l)
        qr, kr, vb = _diff_prep(proj, cos_d, sin_d)
        y_d = _diff_attention(qr, kr, vb, diff_lam_q1[l], diff_lam_k1[l], diff_lam_q2[l],
                              diff_lam_k2[l], diff_norm_g[l], lambda_init)

        x = _merge(x, y_hg, y_s5, y_r, y_d, gates,
                   w_up_hg[l].astype(BF16), w_up_s5[l].astype(BF16), w_up_ret[l].astype(BF16),
                   w_up_diff[l].astype(BF16), w_out[l].astype(BF16), ln_m_g[l], ln_m_b[l])
        x = _ffn_ln(x, ffb_w1[l].astype(BF16), ffb_w3[l].astype(BF16), ffb_w2[l].astype(BF16),
                    ln_b_g[l], ln_b_b[l])
    return x.reshape(bsz, s, d)
```

```python
import functools
import math

import jax
import jax.numpy as jnp
from jax import lax
from jax.experimental import pallas as pl
from jax.experimental.pallas import tpu as pltpu

F32 = jnp.float32
BF16 = jnp.bfloat16

D_MODEL = 2048
DEPTH = 2
N_BRANCH = 4
HG_HEADS, HG_DK, HG_DV = 4, 128, 128
S5_CH, S5_GROUP, S5_STATE = 512, 16, 64
S5_GROUPS = S5_CH // S5_GROUP
RET_HEADS, RET_DK, RET_DV = 4, 64, 128
RET_ROPE_BASE = 10000.0
DIFF_HEADS, DIFF_DQK, DIFF_DV = 4, 64, 128
ROPE_THETA = 500000.0
ROPE_DIM = DIFF_DQK // 4
D_FF = 5632
LN_EPS = 1e-5
RMS_EPS = 1e-6
MASK_VALUE = -1e30
DN_ALPHA = (2 * DEPTH) ** 0.25

OFF_HQ, OFF_HF, OFF_HI, OFF_HG = 0, 512, 1024, 1536
OFF_SU = 2048
OFF_RQ, OFF_RK, OFF_RV, OFF_RG = 2560, 2816, 3072, 3584
OFF_DQ, OFF_DK, OFF_DV = 4096, 4608, 5120
OFF_GL = 5632
MIX_WIDTH = OFF_GL
GATE_WIDTH = N_BRANCH * D_MODEL

VMEM_LIMIT_BYTES = 56 * 1024 * 1024

S5_T = 16
HG_SUB = 16
HG_TC = 512
RET_C = 128
ATT_TQ = 512
ATT_TK = 256
LOG2_E = math.log2(math.e)


def _cparams(sem):
    return pltpu.CompilerParams(dimension_semantics=sem, vmem_limit_bytes=VMEM_LIMIT_BYTES)


def _layer_norm(y, g, b):
    mu = jnp.mean(y, axis=-1, keepdims=True)
    d = y - mu
    var = jnp.mean(d * d, axis=-1, keepdims=True)
    return d * lax.rsqrt(var + LN_EPS) * g + b


def _rms(o):
    return o * lax.rsqrt(jnp.mean(o * o, axis=-1, keepdims=True) + RMS_EPS)


def _silu(x):
    return x * jax.nn.sigmoid(x)


def _dot(a, b):
    return jnp.dot(a, b, preferred_element_type=F32)


def _dot_nt(a, b):
    return lax.dot_general(a, b, (((1,), (1,)), ((), ())), preferred_element_type=F32)


def _dot_tn(a, b):
    return lax.dot_general(a, b, (((0,), (0,)), ((), ())), preferred_element_type=F32)


def _ffn_kernel(x_ref, w1_ref, w3_ref, w2_ref, g_ref, b_ref, o_ref, acc_ref, xb_ref):
    f = pl.program_id(1)

    @pl.when(f == 0)
    def _():
        xb_ref[...] = x_ref[...].astype(BF16)
        acc_ref[...] = jnp.zeros_like(acc_ref)

    xb = xb_ref[...]
    h1 = _dot(xb, w1_ref[...])
    h3 = _dot(xb, w3_ref[...])
    h = (_silu(h1) * h3).astype(BF16)
    acc_ref[...] += _dot(h, w2_ref[...])

    @pl.when(f == pl.num_programs(1) - 1)
    def _():
        y = DN_ALPHA * x_ref[...] + 0.5 * acc_ref[...]
        o_ref[...] = _layer_norm(y, g_ref[...], b_ref[...])


def _ffn_ln(x, w1, w3, w2, g, b, l, *, tm=512, tf=512):
    s, d = x.shape
    f = w1.shape[2]
    tm = min(tm, s)
    return pl.pallas_call(
        _ffn_kernel,
        out_shape=jax.ShapeDtypeStruct((s, d), F32),
        grid=(s // tm, f // tf),
        in_specs=[
            pl.BlockSpec((tm, d), lambda i, j: (i, 0)),
            pl.BlockSpec((None, d, tf), lambda i, j: (l, 0, j)),
            pl.BlockSpec((None, d, tf), lambda i, j: (l, 0, j)),
            pl.BlockSpec((None, tf, d), lambda i, j: (l, j, 0)),
            pl.BlockSpec((1, d), lambda i, j: (0, 0)),
            pl.BlockSpec((1, d), lambda i, j: (0, 0)),
        ],
        out_specs=pl.BlockSpec((tm, d), lambda i, j: (i, 0)),
        scratch_shapes=[pltpu.VMEM((tm, d), F32), pltpu.VMEM((tm, d), BF16)],
        compiler_params=_cparams(("parallel", "arbitrary")),
        name="ffn_ln",
    )(x, w1, w3, w2, g.reshape(1, d), b.reshape(1, d))


def _proj_kernel(x_ref, w_ref, o_ref, xb_ref, *, gate):
    @pl.when(pl.program_id(1) == 0)
    def _():
        xb_ref[...] = x_ref[...].astype(BF16)

    y = _dot(xb_ref[...], w_ref[...])
    if gate:
        y = jax.nn.sigmoid(y)
    o_ref[...] = y.astype(o_ref.dtype)


def _in_proj(x, w_in, l, col0, width, out_dtype, gate, *, tm=1024, tn=512):
    s, d = x.shape
    tm = min(tm, s)
    cb0 = col0 // tn
    return pl.pallas_call(
        functools.partial(_proj_kernel, gate=gate),
        out_shape=jax.ShapeDtypeStruct((s, width), out_dtype),
        grid=(s // tm, width // tn),
        in_specs=[
            pl.BlockSpec((tm, d), lambda i, j: (i, 0)),
            pl.BlockSpec((None, d, tn), lambda i, j: (l, 0, cb0 + j)),
        ],
        out_specs=pl.BlockSpec((tm, tn), lambda i, j: (i, j)),
        scratch_shapes=[pltpu.VMEM((tm, d), BF16)],
        compiler_params=_cparams(("parallel", "arbitrary")),
        name="in_proj_gate" if gate else "in_proj_mix",
    )(x, w_in)


def _cumsum_rows(x, row):
    n = x.shape[0]
    shift = 1
    while shift < n:
        x = x + jnp.where(row >= shift, pltpu.roll(x, shift, axis=0), 0.0)
        shift *= 2
    return x


def _hgrn2_kernel(q_ref, f_ref, i_ref, g_ref, lb_ref, gain_ref, o_ref, st_ref, *, sub, unroll):
    @pl.when(pl.program_id(0) == 0)
    def _():
        st_ref[...] = jnp.zeros_like(st_ref)

    half = sub // 2
    row = lax.broadcasted_iota(jnp.int32, (sub, HG_DK), 0)
    row_h = lax.broadcasted_iota(jnp.int32, (half, HG_DK), 0)
    n_sub = q_ref.shape[0] // sub

    def pair_sum(qh, bh, ks, bs, vs, mask_from):
        dec = jnp.exp(bh - bs)
        if mask_from is not None:
            dec = jnp.where(row_h >= mask_from, dec, 0.0)
        return jnp.sum(qh * ks * dec, axis=-1, keepdims=True) * vs

    def head_step(rows, h):
        cs = slice(h * HG_DK, (h + 1) * HG_DK)
        lb = lb_ref[:, cs]
        fl = f_ref[rows, cs]
        q = _silu(q_ref[rows, cs])
        v = i_ref[rows, cs]
        log_f = jax.nn.log_sigmoid(fl) + jnp.log1p(lb * jnp.exp(-fl))
        k = (1.0 - lb) * jax.nn.sigmoid(-fl)
        b = _cumsum_rows(log_f, row)

        st = st_ref[h]
        o = _dot_nt((q * jnp.exp(b)).astype(BF16), st.astype(BF16))
        q_t, q_b, b_t, b_b = q[:half], q[half:], b[:half], b[half:]
        o_t, o_b = o[:half], o[half:]
        for s_ in range(half):
            ks, bs, vs = k[s_:s_ + 1], b[s_:s_ + 1], v[s_:s_ + 1]
            o_t = o_t + pair_sum(q_t, b_t, ks, bs, vs, s_ if s_ > 0 else None)
            o_b = o_b + pair_sum(q_b, b_b, ks, bs, vs, None)
        for s_ in range(half):
            r = half + s_
            ks, bs, vs = k[r:r + 1], b[r:r + 1], v[r:r + 1]
            o_b = o_b + pair_sum(q_b, b_b, ks, bs, vs, s_ if s_ > 0 else None)
        o = jnp.concatenate([o_t, o_b], axis=0)

        b_end = b[sub - 1:sub]
        kd = (k * jnp.exp(b_end - b)).astype(BF16)
        st_ref[h] = st * jnp.exp(b_end) + _dot_tn(v.astype(BF16), kd)

        y = _rms(o) * gain_ref[:, cs] * _silu(g_ref[rows, cs])
        o_ref[rows, cs] = y.astype(o_ref.dtype)

    def body(c, carry):
        rows = pl.ds(pl.multiple_of(c * sub, sub), sub)
        for h in range(HG_HEADS):
            head_step(rows, h)
        return carry

    lax.fori_loop(0, n_sub, body, 0, unroll=unroll)


def _hgrn2(proj, lb, gain, *, tc=HG_TC, sub=HG_SUB, unroll=2):
    s = proj.shape[0]
    tc = min(tc, s)
    w = HG_HEADS * HG_DK

    def col(off):
        return pl.BlockSpec((tc, w), lambda t: (t, off // w))

    return pl.pallas_call(
        functools.partial(_hgrn2_kernel, sub=sub, unroll=unroll),
        out_shape=jax.ShapeDtypeStruct((s, HG_HEADS * HG_DV), BF16),
        grid=(s // tc,),
        in_specs=[col(OFF_HQ), col(OFF_HF), col(OFF_HI), col(OFF_HG),
                  pl.BlockSpec((1, w), lambda t: (0, 0)),
                  pl.BlockSpec((1, w), lambda t: (0, 0))],
        out_specs=pl.BlockSpec((tc, w), lambda t: (t, 0)),
        scratch_shapes=[pltpu.VMEM((HG_HEADS, HG_DV, HG_DK), F32)],
        compiler_params=_cparams(("arbitrary",)),
        name="hgrn2",
    )(proj, proj, proj, proj, lb.reshape(1, -1), gain.reshape(1, -1))


def _s5_tables(lam_re, lam_im, log_dt, b_re, b_im, c_re, c_im, d_skip):
    hi = lax.Precision.HIGHEST
    g, p, cg, t = S5_GROUPS, S5_STATE, S5_GROUP, S5_T
    dt = jnp.exp(log_dt.astype(F32))[:, None]
    lam_re = lam_re.astype(F32)
    lam_im = lam_im.astype(F32)
    mag = jnp.exp(dt * lam_re)
    ab_re = mag * jnp.cos(dt * lam_im)
    ab_im = mag * jnp.sin(dt * lam_im)
    den = jnp.square(lam_re) + jnp.square(lam_im)
    nr = ab_re - 1.0
    coef_re = (nr * lam_re + ab_im * lam_im) / den
    coef_im = (ab_im * lam_re - nr * lam_im) / den
    b_re = b_re.astype(F32)
    b_im = b_im.astype(F32)
    bb_re = coef_re[..., None] * b_re - coef_im[..., None] * b_im
    bb_im = coef_re[..., None] * b_im + coef_im[..., None] * b_re
    c_re = c_re.astype(F32)
    c_im = c_im.astype(F32)
    tau = jnp.arange(t + 1, dtype=F32)[:, None, None]
    pmag = jnp.exp(tau * (dt * lam_re)[None])
    pw_re = pmag * jnp.cos(tau * (dt * lam_im)[None])
    pw_im = pmag * jnp.sin(tau * (dt * lam_im)[None])
    lb_re = pw_re[:t, :, :, None] * bb_re[None] - pw_im[:t, :, :, None] * bb_im[None]
    lb_im = pw_re[:t, :, :, None] * bb_im[None] + pw_im[:t, :, :, None] * bb_re[None]
    ktau = (jnp.einsum('gcp,tgpd->tgcd', c_re, lb_re, precision=hi)
            - jnp.einsum('gcp,tgpd->tgcd', c_im, lb_im, precision=hi))
    idx = jnp.arange(t)
    diff = idx[None, :] - idx[:, None]
    toep = jnp.where((diff >= 0)[:, :, None, None, None],
                     ktau[jnp.maximum(diff, 0)], 0.0)
    m_op = toep.transpose(2, 0, 4, 1, 3).reshape(g, t * cg, t * cg)
    rev = lb_re[::-1], lb_im[::-1]
    p_op = jnp.concatenate([rev[0].transpose(1, 0, 3, 2), rev[1].transpose(1, 0, 3, 2)],
                           axis=-1).reshape(g, t * cg, 2 * p)
    cl_re = c_re[None] * pw_re[1:, :, None, :] - c_im[None] * pw_im[1:, :, None, :]
    cl_im = c_re[None] * pw_im[1:, :, None, :] + c_im[None] * pw_re[1:, :, None, :]
    q_op = jnp.concatenate([cl_re.transpose(1, 3, 0, 2), -cl_im.transpose(1, 3, 0, 2)],
                           axis=1).reshape(g, 2 * p, t * cg)
    a_blk = jnp.concatenate([pw_re[t], pw_im[t]], axis=-1).reshape(g, 1, 2 * p)
    d_blk = jnp.tile(d_skip.astype(F32).reshape(g, 1, cg), (1, t, 1)).reshape(g, 1, t * cg)
    return m_op.astype(BF16), p_op.astype(BF16), q_op.astype(BF16), a_blk, d_blk


def _s5_kernel(u_ref, m_ref, p_ref, q_ref, a_ref, d_ref, o_ref):
    u = u_ref[...]
    ub = u.astype(BF16)
    n = u.shape[0]
    half = S5_STATE
    h = _dot(ub, p_ref[...])
    row = lax.broadcasted_iota(jnp.int32, h.shape, 0)
    lane = lax.broadcasted_iota(jnp.int32, (1, 2 * half), 1)
    a = a_ref[...]
    shift = 1
    while shift < n:
        a_sw = pltpu.roll(a, half, axis=1)
        a1 = jnp.where(lane < half, a, a_sw)
        a2 = jnp.where(lane < half, -a_sw, a)
        x = jnp.where(row >= shift, pltpu.roll(h, shift, axis=0), 0.0)
        h = h + a1 * x + a2 * pltpu.roll(x, half, axis=1)
        a = a1 * a + a2 * a_sw
        shift *= 2
    h_prev = jnp.where(row >= 1, pltpu.roll(h, 1, axis=0), 0.0)
    y = _dot(ub, m_ref[...]) + _dot(h_prev.astype(BF16), q_ref[...]) + d_ref[...] * u
    o_ref[...] = jax.nn.gelu(y, approximate=True).astype(o_ref.dtype)


def _s5(su, tables):
    s = su.shape[0]
    g, cg, t = S5_GROUPS, S5_GROUP, S5_T
    n = s // t
    w = t * cg
    u = su.reshape(n, t, g, cg).transpose(2, 0, 1, 3).reshape(g, n, w)
    m_op, p_op, q_op, a_blk, d_blk = tables

    def per_group(shape):
        return pl.BlockSpec((None,) + shape, lambda i: (i, 0, 0))

    y = pl.pallas_call(
        _s5_kernel,
        out_shape=jax.ShapeDtypeStruct((g, n, w), BF16),
        grid=(g,),
        in_specs=[per_group((n, w)), per_group((w, w)), per_group((w, 2 * S5_STATE)),
                  per_group((2 * S5_STATE, w)), per_group((1, 2 * S5_STATE)), per_group((1, w))],
        out_specs=per_group((n, w)),
        compiler_params=_cparams(("parallel",)),
        name="s5",
    )(u, m_op, p_op, q_op, a_blk, d_blk)
    return y.reshape(g, n, t, cg).transpose(1, 2, 0, 3).reshape(s, g * cg)


def _rope_kernel(pos_ref, fr_ref, fd_ref, cr_ref, sr_ref, cd_ref, sd_ref):
    pos = pos_ref[...].astype(F32)
    ang_r = pos * fr_ref[...]
    lane_r = lax.broadcasted_iota(jnp.int32, ang_r.shape, 1)
    cr_ref[...] = jnp.cos(ang_r)
    sr_ref[...] = jnp.where(lane_r % RET_DK < RET_DK // 2, -1.0, 1.0) * jnp.sin(ang_r)
    ang_d = pos * fd_ref[...]
    lane_d = lax.broadcasted_iota(jnp.int32, ang_d.shape, 1)
    cd_ref[...] = jnp.cos(ang_d)
    sd_ref[...] = jnp.where(lane_d % DIFF_DQK < ROPE_DIM // 2, -1.0, 1.0) * jnp.sin(ang_d)


def _rope_tables(positions, *, tb=512):
    s = positions.shape[0]
    tb = min(tb, s)
    half_r = RET_DK // 2
    inv_r = 1.0 / jnp.power(RET_ROPE_BASE, jnp.arange(half_r, dtype=F32) * (2.0 / RET_DK))
    fr = jnp.tile(inv_r, RET_HEADS * RET_DK // half_r).reshape(1, RET_HEADS * RET_DK)
    half_d = ROPE_DIM // 2
    inv_d = 1.0 / jnp.power(ROPE_THETA, jnp.arange(half_d, dtype=F32) * (2.0 / ROPE_DIM))
    fd_head = jnp.concatenate([inv_d, inv_d, jnp.zeros((DIFF_DQK - ROPE_DIM,), F32)])
    fd = jnp.tile(fd_head, 2).reshape(1, 2 * DIFF_DQK)
    wr, wd = fr.shape[1], fd.shape[1]
    return pl.pallas_call(
        _rope_kernel,
        out_shape=[jax.ShapeDtypeStruct((s, wr), F32), jax.ShapeDtypeStruct((s, wr), F32),
                   jax.ShapeDtypeStruct((s, wd), F32), jax.ShapeDtypeStruct((s, wd), F32)],
        grid=(s // tb,),
        in_specs=[pl.BlockSpec((tb, 1), lambda i: (i, 0)),
                  pl.BlockSpec((1, wr), lambda i: (0, 0)),
                  pl.BlockSpec((1, wd), lambda i: (0, 0))],
        out_specs=[pl.BlockSpec((tb, wr), lambda i: (i, 0)), pl.BlockSpec((tb, wr), lambda i: (i, 0)),
                   pl.BlockSpec((tb, wd), lambda i: (i, 0)), pl.BlockSpec((tb, wd), lambda i: (i, 0))],
        compiler_params=_cparams(("parallel",)),
        name="rope_tables",
    )(positions.reshape(s, 1), fr, fd)


def _swap_halves(x, group, half):
    n = x.shape[-1]
    lane = lax.broadcasted_iota(jnp.int32, x.shape, x.ndim - 1)
    return jnp.where(lane % group < half,
                     pltpu.roll(x, n - half, axis=x.ndim - 1),
                     pltpu.roll(x, half, axis=x.ndim - 1))


def _ret_kernel(q_ref, k_ref, v_ref, g_ref, cos_ref, sin_ref, gain_ref, o_ref, r_ref):
    @pl.when(pl.program_id(0) == 0)
    def _():
        r_ref[...] = jnp.zeros_like(r_ref)

    c = q_ref.shape[0]
    cos = cos_ref[...]
    sin = sin_ref[...]
    q = q_ref[...]
    k = k_ref[...]
    q = q * cos + _swap_halves(q, RET_DK, RET_DK // 2) * sin
    k = (k * cos + _swap_halves(k, RET_DK, RET_DK // 2) * sin) * (RET_DK ** -0.5)
    ti = lax.broadcasted_iota(jnp.int32, (c, c), 0)
    si = lax.broadcasted_iota(jnp.int32, (c, c), 1)
    tri = ti >= si
    rel = jnp.where(tri, ti - si, 0).astype(F32)
    idx = lax.broadcasted_iota(jnp.int32, (c, 1), 0).astype(F32)
    gain = gain_ref[...]
    for h in range(RET_HEADS):
        log_gamma = math.log1p(-(2.0 ** (-5.0 - h)))
        qh = q[:, h * RET_DK:(h + 1) * RET_DK]
        kh = k[:, h * RET_DK:(h + 1) * RET_DK]
        vh = v_ref[:, h * RET_DV:(h + 1) * RET_DV]
        d_intra = jnp.where(tri, jnp.exp(rel * log_gamma), 0.0)
        scores = _dot_nt(qh.astype(BF16), kh.astype(BF16)) * d_intra
        r = r_ref[h]
        xi = jnp.exp((idx + 1.0) * log_gamma)
        o = _dot(scores.astype(BF16), vh.astype(BF16)) + _dot(qh.astype(BF16), r.astype(BF16)) * xi
        zeta = jnp.exp((c - 1.0 - idx) * log_gamma)
        r_ref[h] = math.exp(c * log_gamma) * r + _dot_tn((kh * zeta).astype(BF16), vh.astype(BF16))
        sl = slice(h * RET_DV, (h + 1) * RET_DV)
        y = _rms(o) * gain[:, sl] * _silu(g_ref[:, sl])
        o_ref[:, sl] = y.astype(o_ref.dtype)


def _retention(proj, cos_r, sin_r, gain, *, c=RET_C):
    s = proj.shape[0]
    c = min(c, s)
    wq, wv = RET_HEADS * RET_DK, RET_HEADS * RET_DV
    return pl.pallas_call(
        _ret_kernel,
        out_shape=jax.ShapeDtypeStruct((s, wv), BF16),
        grid=(s // c,),
        in_specs=[pl.BlockSpec((c, wq), lambda i: (i, OFF_RQ // wq)),
                  pl.BlockSpec((c, wq), lambda i: (i, OFF_RK // wq)),
                  pl.BlockSpec((c, wv), lambda i: (i, OFF_RV // wv)),
                  pl.BlockSpec((c, wv), lambda i: (i, OFF_RG // wv)),
                  pl.BlockSpec((c, wq), lambda i: (i, 0)),
                  pl.BlockSpec((c, wq), lambda i: (i, 0)),
                  pl.BlockSpec((1, wv), lambda i: (0, 0))],
        out_specs=pl.BlockSpec((c, wv), lambda i: (i, 0)),
        scratch_shapes=[pltpu.VMEM((RET_HEADS, RET_DK, RET_DV), F32)],
        compiler_params=_cparams(("arbitrary",)),
        name="retention",
    )(proj, proj, proj, proj, cos_r, sin_r, gain.reshape(1, -1))


def _diff_prep_kernel(q_ref, k_ref, v_ref, cos_ref, sin_ref, qo_ref, ko_ref, vo_ref):
    reps = q_ref.shape[1] // cos_ref.shape[1]
    cos = jnp.concatenate([cos_ref[...]] * reps, axis=-1)
    sin = jnp.concatenate([sin_ref[...]] * reps, axis=-1)
    q = q_ref[...]
    k = k_ref[...]
    q = (q * cos + _swap_halves(q, DIFF_DQK, ROPE_DIM // 2) * sin) * (DIFF_DQK ** -0.5 * LOG2_E)
    k = k * cos + _swap_halves(k, DIFF_DQK, ROPE_DIM // 2) * sin
    qo_ref[...] = q.T.astype(BF16)
    ko_ref[...] = k.astype(BF16)
    vo_ref[...] = v_ref[...].T.astype(BF16)


def _diff_prep(proj, cos_d, sin_d, *, tb=512):
    s = proj.shape[0]
    tb = min(tb, s)
    w = 2 * DIFF_HEADS * DIFF_DQK
    wt = cos_d.shape[1]
    out_t = jax.ShapeDtypeStruct((w, s), BF16)
    return pl.pallas_call(
        _diff_prep_kernel,
        out_shape=[out_t, jax.ShapeDtypeStruct((s, w), BF16), out_t],
        grid=(s // tb,),
        in_specs=[pl.BlockSpec((tb, w), lambda i: (i, OFF_DQ // w)),
                  pl.BlockSpec((tb, w), lambda i: (i, OFF_DK // w)),
                  pl.BlockSpec((tb, w), lambda i: (i, OFF_DV // w)),
                  pl.BlockSpec((tb, wt), lambda i: (i, 0)),
                  pl.BlockSpec((tb, wt), lambda i: (i, 0))],
        out_specs=[pl.BlockSpec((w, tb), lambda i: (0, i)),
                   pl.BlockSpec((tb, w), lambda i: (i, 0)),
                   pl.BlockSpec((w, tb), lambda i: (0, i))],
        compiler_params=_cparams(("parallel",)),
        name="diff_prep",
    )(proj, proj, proj, cos_d, sin_d)


def _diff_attn_kernel(qt_ref, k_ref, vt_ref, lq1_ref, lk1_ref, lq2_ref, lk2_ref, gain_ref, o_ref,
                      m0_ref, m1_ref, l0_ref, l1_ref, a0_ref, a1_ref, *, tk, lambda_init):
    i = pl.program_id(1)
    tq = qt_ref.shape[1]
    qt = qt_ref[...]
    feat = lax.broadcasted_iota(jnp.int32, qt.shape, 0)
    zero = jnp.zeros_like(qt)
    qts = (jnp.where(feat < DIFF_DQK, qt, zero), jnp.where(feat >= DIFF_DQK, qt, zero))
    m_refs, l_refs, a_refs = (m0_ref, m1_ref), (l0_ref, l1_ref), (a0_ref, a1_ref)
    for mp in range(2):
        m_refs[mp][...] = jnp.full_like(m_refs[mp], MASK_VALUE)
        l_refs[mp][...] = jnp.zeros_like(l_refs[mp])
        a_refs[mp][...] = jnp.zeros_like(a_refs[mp])

    def step(j, diag):
        r0 = pl.multiple_of(j * tk, tk)
        kb = k_ref[pl.ds(r0, tk), :]
        vtb = vt_ref[:, pl.ds(r0, tk)]
        for mp in range(2):
            s = _dot(kb, qts[mp])
            if diag is not None:
                ki = lax.broadcasted_iota(jnp.int32, (tk, tq), 0) + diag * tk
                qi = lax.broadcasted_iota(jnp.int32, (tk, tq), 1)
                s = jnp.where(ki <= qi, s, MASK_VALUE)
            m_old = m_refs[mp][...]
            m_new = jnp.maximum(m_old, jnp.max(s, axis=0, keepdims=True))
            alpha = jnp.exp2(m_old - m_new)
            p = jnp.exp2(s - m_new)
            l_refs[mp][...] = alpha * l_refs[mp][...] + jnp.sum(p.reshape(tk // 8, 8, tq), axis=0)
            a_refs[mp][...] = alpha * a_refs[mp][...] + _dot(vtb, p.astype(BF16))
            m_refs[mp][...] = m_new

    def full_step(j, carry):
        step(j, None)
        return carry

    n_diag = tq // tk
    n_full = i * n_diag
    lax.fori_loop(0, n_full, full_step, 0)
    for dd in range(n_diag):
        step(n_full + dd, dd)

    lam = (jnp.exp(jnp.sum(lq1_ref[...] * lk1_ref[...], axis=-1, keepdims=True))
           - jnp.exp(jnp.sum(lq2_ref[...] * lk2_ref[...], axis=-1, keepdims=True)) + lambda_init)

    def normalised(mp):
        return a_refs[mp][...] / jnp.sum(l_refs[mp][...], axis=0, keepdims=True)

    o = normalised(0) - lam * normalised(1)
    o = o * lax.rsqrt(jnp.mean(o * o, axis=0, keepdims=True) + RMS_EPS)
    y = o * gain_ref[...] * (1.0 - lambda_init)
    o_ref[...] = y.T.astype(o_ref.dtype)


def _diff_attention(qt, kr, vt, lq1, lk1, lq2, lk2, gain, lambda_init, *, tq=ATT_TQ, tk=ATT_TK):
    s = kr.shape[0]
    tq = min(tq, s)
    tk = min(tk, tq)
    wh = 2 * DIFF_DQK
    lam_spec = pl.BlockSpec((1, DIFF_DQK), lambda h, i: (0, 0))
    stat = pltpu.VMEM((1, tq), F32)
    part = pltpu.VMEM((8, tq), F32)
    acc = pltpu.VMEM((DIFF_DV, tq), F32)
    return pl.pallas_call(
        functools.partial(_diff_attn_kernel, tk=tk, lambda_init=lambda_init),
        out_shape=jax.ShapeDtypeStruct((s, DIFF_HEADS * DIFF_DV), BF16),
        grid=(DIFF_HEADS, s // tq),
        in_specs=[pl.BlockSpec((wh, tq), lambda h, i: (h, i)),
                  pl.BlockSpec((s, wh), lambda h, i: (0, h)),
                  pl.BlockSpec((DIFF_DV, s), lambda h, i: (h, 0)),
                  lam_spec, lam_spec, lam_spec, lam_spec,
                  pl.BlockSpec((DIFF_DV, 1), lambda h, i: (h, 0))],
        out_specs=pl.BlockSpec((tq, DIFF_DV), lambda h, i: (i, h)),
        scratch_shapes=[stat, stat, part, part, acc, acc],
        compiler_params=_cparams(("parallel", "arbitrary")),
        name="diff_attention",
    )(qt, kr, vt, lq1.reshape(1, -1), lk1.reshape(1, -1), lq2.reshape(1, -1), lk2.reshape(1, -1),
      gain.reshape(-1, 1))


def _merge_kernel(x_ref, yh_ref, ys_ref, yr_ref, yd_ref, gt_ref, wh_ref, ws_ref, wr_ref, wd_ref,
                  wo_ref, g_ref, b_ref, o_ref, mg_ref, *, tn):
    d = D_MODEL
    yh = yh_ref[...]
    ys = ys_ref[...]
    yr = yr_ref[...]
    yd = yd_ref[...]
    for n in range(d // tn):
        c = slice(n * tn, (n + 1) * tn)
        c2 = slice(d + n * tn, d + (n + 1) * tn)
        up_h = _dot(yh, wh_ref[:, c])
        up_s = _dot(ys, ws_ref[:, c]) * jax.nn.sigmoid(_dot(ys, ws_ref[:, c2]))
        up_r = _dot(yr, wr_ref[:, c])
        up_d = _dot(yd, wd_ref[:, c])

        def gate(b):
            return gt_ref[:, b * d + n * tn:b * d + (n + 1) * tn].astype(F32)

        mg = gate(0) * up_h + gate(1) * up_s + gate(2) * up_r + gate(3) * up_d
        mg_ref[:, c] = mg.astype(BF16)
    y = DN_ALPHA * x_ref[...] + _dot(mg_ref[...], wo_ref[...])
    o_ref[...] = _layer_norm(y, g_ref[...], b_ref[...])


def _merge(x, yh, ys, yr, yd, gates, wh, ws, wr, wd, wo, g, b, l, *, tm=256, tn=512):
    s, d = x.shape
    tm = min(tm, s)
    wy = yh.shape[1]

    def rows(width):
        return pl.BlockSpec((tm, width), lambda i: (i, 0))

    def whole(arr):
        if arr.ndim == 3:
            return pl.BlockSpec((None,) + arr.shape[1:], lambda i: (l, 0, 0),
                                pipeline_mode=pl.Buffered(1))
        return pl.BlockSpec(arr.shape, lambda i: (0, 0), pipeline_mode=pl.Buffered(1))

    g2, b2 = g.reshape(1, d), b.reshape(1, d)
    return pl.pallas_call(
        functools.partial(_merge_kernel, tn=tn),
        out_shape=jax.ShapeDtypeStruct((s, d), F32),
        grid=(s // tm,),
        in_specs=[rows(d), rows(wy), rows(wy), rows(wy), rows(wy), rows(GATE_WIDTH),
                  whole(wh), whole(ws), whole(wr), whole(wd), whole(wo), whole(g2), whole(b2)],
        out_specs=rows(d),
        scratch_shapes=[pltpu.VMEM((tm, d), BF16)],
        compiler_params=_cparams(("parallel",)),
        name="merge_out_ln",
    )(x, yh, ys, yr, yd, gates, wh, ws, wr, wd, wo, g2, b2)


def kernel(x, positions, ffa_w1, ffa_w3, ffa_w2, ln_a_g, ln_a_b, w_in, hg_lb_logits, hg_norm_g, s5_lam_re, s5_lam_im, s5_log_dt, s5_b_re, s5_b_im, s5_c_re, s5_c_im, s5_d, ret_norm_g, diff_lam_q1, diff_lam_k1, diff_lam_q2, diff_lam_k2, diff_norm_g, w_up_hg, w_up_s5, w_up_ret, w_up_diff, w_out, ln_m_g, ln_m_b, ffb_w1, ffb_w3, ffb_w2, ln_b_g, ln_b_b):
    bsz, s, d = x.shape
    assert bsz == 1 and d == D_MODEL
    depth = w_in.shape[0]
    x = x.reshape(s, d)

    p_lb = jax.nn.softmax(hg_lb_logits.astype(F32), axis=0)
    lower_bounds = jnp.maximum(jnp.cumsum(p_lb, axis=0) - p_lb[0], 0.0)
    cos_r, sin_r, cos_d, sin_d = _rope_tables(positions.reshape(s))

    (ffa_w1, ffa_w3, ffa_w2, w_in, w_up_hg, w_up_s5, w_up_ret, w_up_diff, w_out,
     ffb_w1, ffb_w3, ffb_w2) = [w.astype(BF16) for w in (
         ffa_w1, ffa_w3, ffa_w2, w_in, w_up_hg, w_up_s5, w_up_ret, w_up_diff, w_out,
         ffb_w1, ffb_w3, ffb_w2)]

    for l in range(depth):
        x = _ffn_ln(x, ffa_w1, ffa_w3, ffa_w2, ln_a_g[l], ln_a_b[l], l)
        proj = _in_proj(x, w_in, l, 0, MIX_WIDTH, F32, gate=False)
        gates = _in_proj(x, w_in, l, OFF_GL, GATE_WIDTH, BF16, gate=True)

        y_hg = _hgrn2(proj, lower_bounds[l], hg_norm_g[l])
        tables = _s5_tables(s5_lam_re[l], s5_lam_im[l], s5_log_dt[l], s5_b_re[l], s5_b_im[l],
                            s5_c_re[l], s5_c_im[l], s5_d[l])
        y_s5 = _s5(proj[:, OFF_SU:OFF_SU + S5_CH], tables)
        y_r = _retention(proj, cos_r, sin_r, ret_norm_g[l])
        lambda_init = 0.8 - 0.6 * math.exp(-0.3 * l)
        qt, kr, vt = _diff_prep(proj, cos_d, sin_d)
        y_d = _diff_attention(qt, kr, vt, diff_lam_q1[l], diff_lam_k1[l], diff_lam_q2[l],
                              diff_lam_k2[l], diff_norm_g[l], lambda_init)

        x = _merge(x, y_hg, y_s5, y_r, y_d, gates, w_up_hg, w_up_s5, w_up_ret, w_up_diff, w_out,
                   ln_m_g[l], ln_m_b[l], l)
        x = _ffn_ln(x, ffb_w1, ffb_w3, ffb_w2, ln_b_g[l], ln_b_b[l], l)
    return x.reshape(bsz, s, d)
```

```python
import functools
import math

import jax
import jax.numpy as jnp
from jax import lax
from jax.experimental import pallas as pl
from jax.experimental.pallas import tpu as pltpu

F32 = jnp.float32
BF16 = jnp.bfloat16

D_MODEL = 2048
DEPTH = 2
N_BRANCH = 4
HG_HEADS, HG_DK, HG_DV = 4, 128, 128
S5_CH, S5_GROUP, S5_STATE = 512, 16, 64
S5_GROUPS = S5_CH // S5_GROUP
RET_HEADS, RET_DK, RET_DV = 4, 64, 128
RET_ROPE_BASE = 10000.0
DIFF_HEADS, DIFF_DQK, DIFF_DV = 4, 64, 128
ROPE_THETA = 500000.0
ROPE_DIM = DIFF_DQK // 4
D_FF = 5632
LN_EPS = 1e-5
RMS_EPS = 1e-6
MASK_VALUE = -1e30
DN_ALPHA = (2 * DEPTH) ** 0.25

OFF_HQ, OFF_HF, OFF_HI, OFF_HG = 0, 512, 1024, 1536
OFF_SU = 2048
OFF_RQ, OFF_RK, OFF_RV, OFF_RG = 2560, 2816, 3072, 3584
OFF_DQ, OFF_DK, OFF_DV = 4096, 4608, 5120
OFF_GL = 5632
MIX_WIDTH = OFF_GL
GATE_WIDTH = N_BRANCH * D_MODEL

VMEM_LIMIT_BYTES = 56 * 1024 * 1024

S5_T = 16
HG_SUB = 16
HG_TC = 512
RET_C = 128
ATT_T = 512
LOG2_E = math.log2(math.e)


def _cparams(sem):
    return pltpu.CompilerParams(dimension_semantics=sem, vmem_limit_bytes=VMEM_LIMIT_BYTES)


def _layer_norm(y, g, b):
    mu = jnp.mean(y, axis=-1, keepdims=True)
    d = y - mu
    var = jnp.mean(d * d, axis=-1, keepdims=True)
    return d * lax.rsqrt(var + LN_EPS) * g + b


def _rms(o):
    return o * lax.rsqrt(jnp.mean(o * o, axis=-1, keepdims=True) + RMS_EPS)


def _silu(x):
    return x * jax.nn.sigmoid(x)


def _dot(a, b):
    return jnp.dot(a, b, preferred_element_type=F32)


def _dot_nt(a, b):
    return lax.dot_general(a, b, (((1,), (1,)), ((), ())), preferred_element_type=F32)


def _dot_tn(a, b):
    return lax.dot_general(a, b, (((0,), (0,)), ((), ())), preferred_element_type=F32)


def _ffn_kernel(x_ref, w1_ref, w3_ref, w2_ref, g_ref, b_ref, o_ref, xb_ref):
    f = pl.program_id(1)

    @pl.when(f == 0)
    def _():
        xb_ref[...] = x_ref[...].astype(BF16)
        o_ref[...] = jnp.zeros_like(o_ref)

    xb = xb_ref[...]
    h1 = _dot(xb, w1_ref[...].astype(BF16))
    h3 = _dot(xb, w3_ref[...].astype(BF16))
    h = (_silu(h1) * h3).astype(BF16)
    o_ref[...] += _dot(h, w2_ref[...].astype(BF16))

    @pl.when(f == pl.num_programs(1) - 1)
    def _():
        y = DN_ALPHA * x_ref[...] + 0.5 * o_ref[...]
        o_ref[...] = _layer_norm(y, g_ref[...], b_ref[...])


def _ffn_ln(x, w1, w3, w2, g, b, l, *, tm=1024, tf=256):
    s, d = x.shape
    f = w1.shape[2]
    tm = min(tm, s)
    return pl.pallas_call(
        _ffn_kernel,
        out_shape=jax.ShapeDtypeStruct((s, d), F32),
        grid=(s // tm, f // tf),
        in_specs=[
            pl.BlockSpec((tm, d), lambda i, j: (i, 0), pipeline_mode=pl.Buffered(1)),
            pl.BlockSpec((None, d, tf), lambda i, j: (l, 0, j)),
            pl.BlockSpec((None, d, tf), lambda i, j: (l, 0, j)),
            pl.BlockSpec((None, tf, d), lambda i, j: (l, j, 0)),
            pl.BlockSpec((1, d), lambda i, j: (0, 0)),
            pl.BlockSpec((1, d), lambda i, j: (0, 0)),
        ],
        out_specs=pl.BlockSpec((tm, d), lambda i, j: (i, 0)),
        scratch_shapes=[pltpu.VMEM((tm, d), BF16)],
        compiler_params=_cparams(("parallel", "arbitrary")),
        name="ffn_ln",
    )(x, w1, w3, w2, g.reshape(1, d), b.reshape(1, d))


def _proj_kernel(x_ref, w_ref, o_ref, xb_ref, *, gate):
    @pl.when(pl.program_id(1) == 0)
    def _():
        xb_ref[...] = x_ref[...].astype(BF16)

    y = _dot(xb_ref[...], w_ref[...].astype(BF16))
    if gate:
        y = jax.nn.sigmoid(y)
    o_ref[...] = y.astype(o_ref.dtype)


def _in_proj(x, w_in, l, col0, width, out_dtype, gate, *, tm=1024, tn=512):
    s, d = x.shape
    tm = min(tm, s)
    cb0 = col0 // tn
    return pl.pallas_call(
        functools.partial(_proj_kernel, gate=gate),
        out_shape=jax.ShapeDtypeStruct((s, width), out_dtype),
        grid=(s // tm, width // tn),
        in_specs=[
            pl.BlockSpec((tm, d), lambda i, j: (i, 0)),
            pl.BlockSpec((None, d, tn), lambda i, j: (l, 0, cb0 + j)),
        ],
        out_specs=pl.BlockSpec((tm, tn), lambda i, j: (i, j)),
        scratch_shapes=[pltpu.VMEM((tm, d), BF16)],
        compiler_params=_cparams(("parallel", "arbitrary")),
        name="in_proj_gate" if gate else "in_proj_mix",
    )(x, w_in)


def _cumsum_rows(x, row):
    n = x.shape[0]
    shift = 1
    while shift < n:
        x = x + jnp.where(row >= shift, pltpu.roll(x, shift, axis=0), 0.0)
        shift *= 2
    return x


def _hgrn2_kernel(q_ref, f_ref, i_ref, g_ref, lb_ref, gain_ref, o_ref, st_ref, *, sub, unroll):
    @pl.when(pl.program_id(0) == 0)
    def _():
        st_ref[...] = jnp.zeros_like(st_ref)

    half = sub // 2
    row = lax.broadcasted_iota(jnp.int32, (sub, HG_DK), 0)
    row_h = lax.broadcasted_iota(jnp.int32, (half, HG_DK), 0)
    n_sub = q_ref.shape[0] // sub

    def pair_sum(qh, bh, ks, bs, vs, mask_from):
        dec = jnp.exp(bh - bs)
        if mask_from is not None:
            dec = jnp.where(row_h >= mask_from, dec, 0.0)
        return jnp.sum(qh * ks * dec, axis=-1, keepdims=True) * vs

    def head_step(rows, h):
        cs = slice(h * HG_DK, (h + 1) * HG_DK)
        lb = lb_ref[:, cs]
        fl = f_ref[rows, cs]
        q = _silu(q_ref[rows, cs])
        v = i_ref[rows, cs]
        log_f = jax.nn.log_sigmoid(fl) + jnp.log1p(lb * jnp.exp(-fl))
        k = (1.0 - lb) * jax.nn.sigmoid(-fl)
        b = _cumsum_rows(log_f, row)

        st = st_ref[h]
        o = _dot_nt((q * jnp.exp(b)).astype(BF16), st.astype(BF16))
        q_t, q_b, b_t, b_b = q[:half], q[half:], b[:half], b[half:]
        o_t, o_b = o[:half], o[half:]
        for s_ in range(half):
            ks, bs, vs = k[s_:s_ + 1], b[s_:s_ + 1], v[s_:s_ + 1]
            o_t = o_t + pair_sum(q_t, b_t, ks, bs, vs, s_ if s_ > 0 else None)
            o_b = o_b + pair_sum(q_b, b_b, ks, bs, vs, None)
        for s_ in range(half):
            r = half + s_
            ks, bs, vs = k[r:r + 1], b[r:r + 1], v[r:r + 1]
            o_b = o_b + pair_sum(q_b, b_b, ks, bs, vs, s_ if s_ > 0 else None)
        o = jnp.concatenate([o_t, o_b], axis=0)

        b_end = b[sub - 1:sub]
        kd = (k * jnp.exp(b_end - b)).astype(BF16)
        st_ref[h] = st * jnp.exp(b_end) + _dot_tn(v.astype(BF16), kd)

        y = _rms(o) * gain_ref[:, cs] * _silu(g_ref[rows, cs])
        o_ref[rows, cs] = y.astype(o_ref.dtype)

    def body(c, carry):
        rows = pl.ds(pl.multiple_of(c * sub, sub), sub)
        for h in range(HG_HEADS):
            head_step(rows, h)
        return carry

    lax.fori_loop(0, n_sub, body, 0, unroll=unroll)


def _hgrn2(proj, lb, gain, *, tc=HG_TC, sub=HG_SUB, unroll=2):
    s = proj.shape[0]
    tc = min(tc, s)
    w = HG_HEADS * HG_DK

    def col(off):
        return pl.BlockSpec((tc, w), lambda t: (t, off // w))

    return pl.pallas_call(
        functools.partial(_hgrn2_kernel, sub=sub, unroll=unroll),
        out_shape=jax.ShapeDtypeStruct((s, HG_HEADS * HG_DV), BF16),
        grid=(s // tc,),
        in_specs=[col(OFF_HQ), col(OFF_HF), col(OFF_HI), col(OFF_HG),
                  pl.BlockSpec((1, w), lambda t: (0, 0)),
                  pl.BlockSpec((1, w), lambda t: (0, 0))],
        out_specs=pl.BlockSpec((tc, w), lambda t: (t, 0)),
        scratch_shapes=[pltpu.VMEM((HG_HEADS, HG_DV, HG_DK), F32)],
        compiler_params=_cparams(("arbitrary",)),
        name="hgrn2",
    )(proj, proj, proj, proj, lb.reshape(1, -1), gain.reshape(1, -1))


def _s5_tables(lam_re, lam_im, log_dt, b_re, b_im, c_re, c_im, d_skip):
    hi = lax.Precision.HIGHEST
    g, p, cg, t = S5_GROUPS, S5_STATE, S5_GROUP, S5_T
    dt = jnp.exp(log_dt.astype(F32))[:, None]
    lam_re = lam_re.astype(F32)
    lam_im = lam_im.astype(F32)
    mag = jnp.exp(dt * lam_re)
    ab_re = mag * jnp.cos(dt * lam_im)
    ab_im = mag * jnp.sin(dt * lam_im)
    den = jnp.square(lam_re) + jnp.square(lam_im)
    nr = ab_re - 1.0
    coef_re = (nr * lam_re + ab_im * lam_im) / den
    coef_im = (ab_im * lam_re - nr * lam_im) / den
    b_re = b_re.astype(F32)
    b_im = b_im.astype(F32)
    bb_re = coef_re[..., None] * b_re - coef_im[..., None] * b_im
    bb_im = coef_re[..., None] * b_im + coef_im[..., None] * b_re
    c_re = c_re.astype(F32)
    c_im = c_im.astype(F32)
    tau = jnp.arange(t + 1, dtype=F32)[:, None, None]
    pmag = jnp.exp(tau * (dt * lam_re)[None])
    pw_re = pmag * jnp.cos(tau * (dt * lam_im)[None])
    pw_im = pmag * jnp.sin(tau * (dt * lam_im)[None])
    lb_re = pw_re[:t, :, :, None] * bb_re[None] - pw_im[:t, :, :, None] * bb_im[None]
    lb_im = pw_re[:t, :, :, None] * bb_im[None] + pw_im[:t, :, :, None] * bb_re[None]
    ktau = (jnp.einsum('gcp,tgpd->tgcd', c_re, lb_re, precision=hi)
            - jnp.einsum('gcp,tgpd->tgcd', c_im, lb_im, precision=hi))
    idx = jnp.arange(t)
    diff = idx[None, :] - idx[:, None]
    toep = jnp.where((diff >= 0)[:, :, None, None, None],
                     ktau[jnp.maximum(diff, 0)], 0.0)
    m_op = toep.transpose(2, 0, 4, 1, 3).reshape(g, t * cg, t * cg)
    rev = lb_re[::-1], lb_im[::-1]
    p_op = jnp.concatenate([rev[0].transpose(1, 0, 3, 2), rev[1].transpose(1, 0, 3, 2)],
                           axis=-1).reshape(g, t * cg, 2 * p)
    cl_re = c_re[None] * pw_re[1:, :, None, :] - c_im[None] * pw_im[1:, :, None, :]
    cl_im = c_re[None] * pw_im[1:, :, None, :] + c_im[None] * pw_re[1:, :, None, :]
    q_op = jnp.concatenate([cl_re.transpose(1, 3, 0, 2), -cl_im.transpose(1, 3, 0, 2)],
                           axis=1).reshape(g, 2 * p, t * cg)
    a_blk = jnp.concatenate([pw_re[t], pw_im[t]], axis=-1).reshape(g, 1, 2 * p)
    d_blk = jnp.tile(d_skip.astype(F32).reshape(g, 1, cg), (1, t, 1)).reshape(g, 1, t * cg)
    return m_op.astype(BF16), p_op.astype(BF16), q_op.astype(BF16), a_blk, d_blk


def _s5_kernel(u_ref, m_ref, p_ref, q_ref, a_ref, d_ref, o_ref):
    u = u_ref[...]
    ub = u.astype(BF16)
    n = u.shape[0]
    half = S5_STATE
    h = _dot(ub, p_ref[...])
    row = lax.broadcasted_iota(jnp.int32, h.shape, 0)
    lane = lax.broadcasted_iota(jnp.int32, (1, 2 * half), 1)
    a = a_ref[...]
    shift = 1
    while shift < n:
        a_sw = pltpu.roll(a, half, axis=1)
        a1 = jnp.where(lane < half, a, a_sw)
        a2 = jnp.where(lane < half, -a_sw, a)
        x = jnp.where(row >= shift, pltpu.roll(h, shift, axis=0), 0.0)
        h = h + a1 * x + a2 * pltpu.roll(x, half, axis=1)
        a = a1 * a + a2 * a_sw
        shift *= 2
    h_prev = jnp.where(row >= 1, pltpu.roll(h, 1, axis=0), 0.0)
    y = _dot(ub, m_ref[...]) + _dot(h_prev.astype(BF16), q_ref[...]) + d_ref[...] * u
    o_ref[...] = jax.nn.gelu(y, approximate=True).astype(o_ref.dtype)


def _s5(su, tables):
    s = su.shape[0]
    g, cg, t = S5_GROUPS, S5_GROUP, S5_T
    n = s // t
    w = t * cg
    u = su.reshape(n, t, g, cg).transpose(2, 0, 1, 3).reshape(g, n, w)
    m_op, p_op, q_op, a_blk, d_blk = tables

    def per_group(shape):
        return pl.BlockSpec((None,) + shape, lambda i: (i, 0, 0))

    y = pl.pallas_call(
        _s5_kernel,
        out_shape=jax.ShapeDtypeStruct((g, n, w), BF16),
        grid=(g,),
        in_specs=[per_group((n, w)), per_group((w, w)), per_group((w, 2 * S5_STATE)),
                  per_group((2 * S5_STATE, w)), per_group((1, 2 * S5_STATE)), per_group((1, w))],
        out_specs=per_group((n, w)),
        compiler_params=_cparams(("parallel",)),
        name="s5",
    )(u, m_op, p_op, q_op, a_blk, d_blk)
    return y.reshape(g, n, t, cg).transpose(1, 2, 0, 3).reshape(s, g * cg)


def _rope_kernel(pos_ref, fr_ref, fd_ref, cr_ref, sr_ref, cd_ref, sd_ref):
    pos = pos_ref[...].astype(F32)
    ang_r = pos * fr_ref[...]
    lane_r = lax.broadcasted_iota(jnp.int32, ang_r.shape, 1)
    cr_ref[...] = jnp.cos(ang_r)
    sr_ref[...] = jnp.where(lane_r % RET_DK < RET_DK // 2, -1.0, 1.0) * jnp.sin(ang_r)
    ang_d = pos * fd_ref[...]
    lane_d = lax.broadcasted_iota(jnp.int32, ang_d.shape, 1)
    cd_ref[...] = jnp.cos(ang_d)
    sd_ref[...] = jnp.where(lane_d % DIFF_DQK < ROPE_DIM // 2, -1.0, 1.0) * jnp.sin(ang_d)


def _rope_tables(positions, *, tb=512):
    s = positions.shape[0]
    tb = min(tb, s)
    half_r = RET_DK // 2
    inv_r = 1.0 / jnp.power(RET_ROPE_BASE, jnp.arange(half_r, dtype=F32) * (2.0 / RET_DK))
    fr = jnp.tile(inv_r, RET_HEADS * RET_DK // half_r).reshape(1, RET_HEADS * RET_DK)
    half_d = ROPE_DIM // 2
    inv_d = 1.0 / jnp.power(ROPE_THETA, jnp.arange(half_d, dtype=F32) * (2.0 / ROPE_DIM))
    fd_head = jnp.concatenate([inv_d, inv_d, jnp.zeros((DIFF_DQK - ROPE_DIM,), F32)])
    fd = jnp.tile(fd_head, 2).reshape(1, 2 * DIFF_DQK)
    wr, wd = fr.shape[1], fd.shape[1]
    return pl.pallas_call(
        _rope_kernel,
        out_shape=[jax.ShapeDtypeStruct((s, wr), F32), jax.ShapeDtypeStruct((s, wr), F32),
                   jax.ShapeDtypeStruct((s, wd), F32), jax.ShapeDtypeStruct((s, wd), F32)],
        grid=(s // tb,),
        in_specs=[pl.BlockSpec((tb, 1), lambda i: (i, 0)),
                  pl.BlockSpec((1, wr), lambda i: (0, 0)),
                  pl.BlockSpec((1, wd), lambda i: (0, 0))],
        out_specs=[pl.BlockSpec((tb, wr), lambda i: (i, 0)), pl.BlockSpec((tb, wr), lambda i: (i, 0)),
                   pl.BlockSpec((tb, wd), lambda i: (i, 0)), pl.BlockSpec((tb, wd), lambda i: (i, 0))],
        compiler_params=_cparams(("parallel",)),
        name="rope_tables",
    )(positions.reshape(s, 1), fr, fd)


def _swap_halves(x, group, half):
    n = x.shape[-1]
    lane = lax.broadcasted_iota(jnp.int32, x.shape, x.ndim - 1)
    return jnp.where(lane % group < half,
                     pltpu.roll(x, n - half, axis=x.ndim - 1),
                     pltpu.roll(x, half, axis=x.ndim - 1))


def _ret_kernel(q_ref, k_ref, v_ref, g_ref, cos_ref, sin_ref, gain_ref, o_ref, r_ref):
    @pl.when(pl.program_id(0) == 0)
    def _():
        r_ref[...] = jnp.zeros_like(r_ref)

    c = q_ref.shape[0]
    cos = cos_ref[...]
    sin = sin_ref[...]
    q = q_ref[...]
    k = k_ref[...]
    q = q * cos + _swap_halves(q, RET_DK, RET_DK // 2) * sin
    k = (k * cos + _swap_halves(k, RET_DK, RET_DK // 2) * sin) * (RET_DK ** -0.5)
    ti = lax.broadcasted_iota(jnp.int32, (c, c), 0)
    si = lax.broadcasted_iota(jnp.int32, (c, c), 1)
    tri = ti >= si
    rel = jnp.where(tri, ti - si, 0).astype(F32)
    idx = lax.broadcasted_iota(jnp.int32, (c, 1), 0).astype(F32)
    gain = gain_ref[...]
    for h in range(RET_HEADS):
        log_gamma = math.log1p(-(2.0 ** (-5.0 - h)))
        qh = q[:, h * RET_DK:(h + 1) * RET_DK]
        kh = k[:, h * RET_DK:(h + 1) * RET_DK]
        vh = v_ref[:, h * RET_DV:(h + 1) * RET_DV]
        d_intra = jnp.where(tri, jnp.exp(rel * log_gamma), 0.0)
        scores = _dot_nt(qh.astype(BF16), kh.astype(BF16)) * d_intra
        r = r_ref[h]
        xi = jnp.exp((idx + 1.0) * log_gamma)
        o = _dot(scores.astype(BF16), vh.astype(BF16)) + _dot(qh.astype(BF16), r.astype(BF16)) * xi
        zeta = jnp.exp((c - 1.0 - idx) * log_gamma)
        r_ref[h] = math.exp(c * log_gamma) * r + _dot_tn((kh * zeta).astype(BF16), vh.astype(BF16))
        sl = slice(h * RET_DV, (h + 1) * RET_DV)
        y = _rms(o) * gain[:, sl] * _silu(g_ref[:, sl])
        o_ref[:, sl] = y.astype(o_ref.dtype)


def _retention(proj, cos_r, sin_r, gain, *, c=RET_C):
    s = proj.shape[0]
    c = min(c, s)
    wq, wv = RET_HEADS * RET_DK, RET_HEADS * RET_DV
    return pl.pallas_call(
        _ret_kernel,
        out_shape=jax.ShapeDtypeStruct((s, wv), BF16),
        grid=(s // c,),
        in_specs=[pl.BlockSpec((c, wq), lambda i: (i, OFF_RQ // wq)),
                  pl.BlockSpec((c, wq), lambda i: (i, OFF_RK // wq)),
                  pl.BlockSpec((c, wv), lambda i: (i, OFF_RV // wv)),
                  pl.BlockSpec((c, wv), lambda i: (i, OFF_RG // wv)),
                  pl.BlockSpec((c, wq), lambda i: (i, 0)),
                  pl.BlockSpec((c, wq), lambda i: (i, 0)),
                  pl.BlockSpec((1, wv), lambda i: (0, 0))],
        out_specs=pl.BlockSpec((c, wv), lambda i: (i, 0)),
        scratch_shapes=[pltpu.VMEM((RET_HEADS, RET_DK, RET_DV), F32)],
        compiler_params=_cparams(("arbitrary",)),
        name="retention",
    )(proj, proj, proj, proj, cos_r, sin_r, gain.reshape(1, -1))


def _diff_prep_kernel(q_ref, k_ref, v_ref, cos_ref, sin_ref, qo_ref, ko_ref, vo_ref):
    reps = q_ref.shape[1] // cos_ref.shape[1]
    cos = jnp.concatenate([cos_ref[...]] * reps, axis=-1)
    sin = jnp.concatenate([sin_ref[...]] * reps, axis=-1)
    q = q_ref[...]
    k = k_ref[...]
    q = (q * cos + _swap_halves(q, DIFF_DQK, ROPE_DIM // 2) * sin) * (DIFF_DQK ** -0.5 * LOG2_E)
    k = k * cos + _swap_halves(k, DIFF_DQK, ROPE_DIM // 2) * sin
    qo_ref[...] = q.T.astype(BF16)
    ko_ref[...] = k.astype(BF16)
    vo_ref[...] = v_ref[...].T.astype(BF16)


def _diff_prep(proj, cos_d, sin_d, *, tb=512):
    s = proj.shape[0]
    tb = min(tb, s)
    w = 2 * DIFF_HEADS * DIFF_DQK
    wt = cos_d.shape[1]
    out_t = jax.ShapeDtypeStruct((w, s), BF16)
    return pl.pallas_call(
        _diff_prep_kernel,
        out_shape=[out_t, jax.ShapeDtypeStruct((s, w), BF16), out_t],
        grid=(s // tb,),
        in_specs=[pl.BlockSpec((tb, w), lambda i: (i, OFF_DQ // w)),
                  pl.BlockSpec((tb, w), lambda i: (i, OFF_DK // w)),
                  pl.BlockSpec((tb, w), lambda i: (i, OFF_DV // w)),
                  pl.BlockSpec((tb, wt), lambda i: (i, 0)),
                  pl.BlockSpec((tb, wt), lambda i: (i, 0))],
        out_specs=[pl.BlockSpec((w, tb), lambda i: (0, i)),
                   pl.BlockSpec((tb, w), lambda i: (i, 0)),
                   pl.BlockSpec((w, tb), lambda i: (0, i))],
        compiler_params=_cparams(("parallel",)),
        name="diff_prep",
    )(proj, proj, proj, cos_d, sin_d)


def _diff_attn_kernel(qt_ref, k_ref, vt_ref, lq1_ref, lk1_ref, lq2_ref, lk2_ref, gain_ref, o_ref,
                      sa_ref, sb_ref, m0_ref, m1_ref, l0_ref, l1_ref, a0_ref, a1_ref, *, lambda_init):
    i = pl.program_id(1)
    tq = qt_ref.shape[1]
    qt = qt_ref[...]
    feat = lax.broadcasted_iota(jnp.int32, qt.shape, 0)
    zero = jnp.zeros_like(qt)
    qts = (jnp.where(feat < DIFF_DQK, qt, zero), jnp.where(feat >= DIFF_DQK, qt, zero))
    m_refs, l_refs, a_refs = (m0_ref, m1_ref), (l0_ref, l1_ref), (a0_ref, a1_ref)
    for mp in range(2):
        m_refs[mp][...] = jnp.full_like(m_refs[mp], MASK_VALUE)
        l_refs[mp][...] = jnp.zeros_like(l_refs[mp])
        a_refs[mp][...] = jnp.zeros_like(a_refs[mp])

    def scores(t, s_ref):
        kb = k_ref[pl.ds(pl.multiple_of(t * tq, tq), tq), :]
        for mp in range(2):
            s_ref[mp] = _dot(kb, qts[mp])

    def accumulate(t, s_ref, diagonal):
        vtb = vt_ref[:, pl.ds(pl.multiple_of(t * tq, tq), tq)]
        for mp in range(2):
            s = s_ref[mp]
            if diagonal:
                ki = lax.broadcasted_iota(jnp.int32, (tq, tq), 0)
                qi = lax.broadcasted_iota(jnp.int32, (tq, tq), 1)
                s = jnp.where(ki <= qi, s, MASK_VALUE)
            m_old = m_refs[mp][...]
            m_new = jnp.maximum(m_old, jnp.max(s, axis=0, keepdims=True))
            alpha = jnp.exp2(m_old - m_new)
            p = jnp.exp2(s - m_new)
            l_refs[mp][...] = alpha * l_refs[mp][...] + jnp.sum(p.reshape(tq // 8, 8, tq), axis=0)
            a_refs[mp][...] = alpha * a_refs[mp][...] + _dot(vtb, p.astype(BF16))
            m_refs[mp][...] = m_new

    def pair(jj, carry):
        t = 2 * jj
        scores(t + 1, sb_ref)
        accumulate(t, sa_ref, False)
        scores(t + 2, sa_ref)
        accumulate(t + 1, sb_ref, False)
        return carry

    scores(0, sa_ref)
    lax.fori_loop(0, i // 2, pair, 0)

    @pl.when(i % 2 == 0)
    def _():
        accumulate(i, sa_ref, True)

    @pl.when(i % 2 == 1)
    def _():
        scores(i, sb_ref)
        accumulate(i - 1, sa_ref, False)
        accumulate(i, sb_ref, True)

    lam = (jnp.exp(jnp.sum(lq1_ref[...] * lk1_ref[...], axis=-1, keepdims=True))
           - jnp.exp(jnp.sum(lq2_ref[...] * lk2_ref[...], axis=-1, keepdims=True)) + lambda_init)

    def normalised(mp):
        return a_refs[mp][...] / jnp.sum(l_refs[mp][...], axis=0, keepdims=True)

    o = normalised(0) - lam * normalised(1)
    o = o * lax.rsqrt(jnp.mean(o * o, axis=0, keepdims=True) + RMS_EPS)
    y = o * gain_ref[...] * (1.0 - lambda_init)
    o_ref[...] = y.T.astype(o_ref.dtype)


def _diff_attention(qt, kr, vt, lq1, lk1, lq2, lk2, gain, lambda_init, *, tq=ATT_T):
    s = kr.shape[0]
    tq = min(tq, s)
    wh = 2 * DIFF_DQK
    lam_spec = pl.BlockSpec((1, DIFF_DQK), lambda h, i: (0, 0))
    score = pltpu.VMEM((2, tq, tq), F32)
    stat = pltpu.VMEM((1, tq), F32)
    part = pltpu.VMEM((8, tq), F32)
    acc = pltpu.VMEM((DIFF_DV, tq), F32)
    return pl.pallas_call(
        functools.partial(_diff_attn_kernel, lambda_init=lambda_init),
        out_shape=jax.ShapeDtypeStruct((s, DIFF_HEADS * DIFF_DV), BF16),
        grid=(DIFF_HEADS, s // tq),
        in_specs=[pl.BlockSpec((wh, tq), lambda h, i: (h, i)),
                  pl.BlockSpec((s, wh), lambda h, i: (0, h)),
                  pl.BlockSpec((DIFF_DV, s), lambda h, i: (h, 0)),
                  lam_spec, lam_spec, lam_spec, lam_spec,
                  pl.BlockSpec((DIFF_DV, 1), lambda h, i: (h, 0))],
        out_specs=pl.BlockSpec((tq, DIFF_DV), lambda h, i: (i, h)),
        scratch_shapes=[score, score, stat, stat, part, part, acc, acc],
        compiler_params=_cparams(("parallel", "arbitrary")),
        name="diff_attention",
    )(qt, kr, vt, lq1.reshape(1, -1), lk1.reshape(1, -1), lq2.reshape(1, -1), lk2.reshape(1, -1),
      gain.reshape(-1, 1))


def _merge_kernel(x_ref, yh_ref, ys_ref, yr_ref, yd_ref, gt_ref, wh_ref, ws_ref, wr_ref, wd_ref,
                  wo_ref, g_ref, b_ref, o_ref, mg_ref, *, tn):
    d = D_MODEL
    yh = yh_ref[...]
    ys = ys_ref[...]
    yr = yr_ref[...]
    yd = yd_ref[...]
    for n in range(d // tn):
        c = slice(n * tn, (n + 1) * tn)
        c2 = slice(d + n * tn, d + (n + 1) * tn)
        up_h = _dot(yh, wh_ref[:, c])
        up_s = _dot(ys, ws_ref[:, c]) * jax.nn.sigmoid(_dot(ys, ws_ref[:, c2]))
        up_r = _dot(yr, wr_ref[:, c])
        up_d = _dot(yd, wd_ref[:, c])

        def gate(b):
            return gt_ref[:, b * d + n * tn:b * d + (n + 1) * tn].astype(F32)

        mg = gate(0) * up_h + gate(1) * up_s + gate(2) * up_r + gate(3) * up_d
        mg_ref[:, c] = mg.astype(BF16)
    y = DN_ALPHA * x_ref[...] + _dot(mg_ref[...], wo_ref[...])
    o_ref[...] = _layer_norm(y, g_ref[...], b_ref[...])


def _merge(x, yh, ys, yr, yd, gates, wh, ws, wr, wd, wo, g, b, l, *, tm=256, tn=512):
    s, d = x.shape
    tm = min(tm, s)
    wy = yh.shape[1]

    def rows(width):
        return pl.BlockSpec((tm, width), lambda i: (i, 0))

    def whole(arr):
        if arr.ndim == 3:
            return pl.BlockSpec((None,) + arr.shape[1:], lambda i: (l, 0, 0),
                                pipeline_mode=pl.Buffered(1))
        return pl.BlockSpec(arr.shape, lambda i: (0, 0), pipeline_mode=pl.Buffered(1))

    g2, b2 = g.reshape(1, d), b.reshape(1, d)
    return pl.pallas_call(
        functools.partial(_merge_kernel, tn=tn),
        out_shape=jax.ShapeDtypeStruct((s, d), F32),
        grid=(s // tm,),
        in_specs=[rows(d), rows(wy), rows(wy), rows(wy), rows(wy), rows(GATE_WIDTH),
                  whole(wh), whole(ws), whole(wr), whole(wd), whole(wo), whole(g2), whole(b2)],
        out_specs=rows(d),
        scratch_shapes=[pltpu.VMEM((tm, d), BF16)],
        compiler_params=_cparams(("parallel",)),
        name="merge_out_ln",
    )(x, yh, ys, yr, yd, gates, wh, ws, wr, wd, wo, g2, b2)


def kernel(x, positions, ffa_w1, ffa_w3, ffa_w2, ln_a_g, ln_a_b, w_in, hg_lb_logits, hg_norm_g, s5_lam_re, s5_lam_im, s5_log_dt, s5_b_re, s5_b_im, s5_c_re, s5_c_im, s5_d, ret_norm_g, diff_lam_q1, diff_lam_k1, diff_lam_q2, diff_lam_k2, diff_norm_g, w_up_hg, w_up_s5, w_up_ret, w_up_diff, w_out, ln_m_g, ln_m_b, ffb_w1, ffb_w3, ffb_w2, ln_b_g, ln_b_b):
    bsz, s, d = x.shape
    assert bsz == 1 and d == D_MODEL
    depth = w_in.shape[0]
    x = x.reshape(s, d)

    p_lb = jax.nn.softmax(hg_lb_logits.astype(F32), axis=0)
    lower_bounds = jnp.maximum(jnp.cumsum(p_lb, axis=0) - p_lb[0], 0.0)
    cos_r, sin_r, cos_d, sin_d = _rope_tables(positions.reshape(s))

    w_up_hg, w_up_s5, w_up_ret, w_up_diff, w_out = [
        w.astype(BF16) for w in (w_up_hg, w_up_s5, w_up_ret, w_up_diff, w_out)]

    for l in range(depth):
        x = _ffn_ln(x, ffa_w1, ffa_w3, ffa_w2, ln_a_g[l], ln_a_b[l], l)
        proj = _in_proj(x, w_in, l, 0, MIX_WIDTH, F32, gate=False)
        gates = _in_proj(x, w_in, l, OFF_GL, GATE_WIDTH, BF16, gate=True)

        y_hg = _hgrn2(proj, lower_bounds[l], hg_norm_g[l])
        tables = _s5_tables(s5_lam_re[l], s5_lam_im[l], s5_log_dt[l], s5_b_re[l], s5_b_im[l],
                            s5_c_re[l], s5_c_im[l], s5_d[l])
        y_s5 = _s5(proj[:, OFF_SU:OFF_SU + S5_CH], tables)
        y_r = _retention(proj, cos_r, sin_r, ret_norm_g[l])
        lambda_init = 0.8 - 0.6 * math.exp(-0.3 * l)
        qt, kr, vt = _diff_prep(proj, cos_d, sin_d)
        y_d = _diff_attention(qt, kr, vt, diff_lam_q1[l], diff_lam_k1[l], diff_lam_q2[l],
                              diff_lam_k2[l], diff_norm_g[l], lambda_init)

        x = _merge(x, y_hg, y_s5, y_r, y_d, gates, w_up_hg, w_up_s5, w_up_ret, w_up_diff, w_out,
                   ln_m_g[l], ln_m_b[l], l)
        x = _ffn_ln(x, ffb_w1, ffb_w3, ffb_w2, ln_b_g[l], ln_b_b[l], l)
    return x.reshape(bsz, s, d)
```

```python
import functools
import math

import jax
import jax.numpy as jnp
from jax import lax
from jax.experimental import pallas as pl
from jax.experimental.pallas import tpu as pltpu

F32 = jnp.float32
BF16 = jnp.bfloat16

D_MODEL = 2048
DEPTH = 2
N_BRANCH = 4
HG_HEADS, HG_DK, HG_DV = 4, 128, 128
S5_CH, S5_GROUP, S5_STATE = 512, 16, 64
S5_GROUPS = S5_CH // S5_GROUP
RET_HEADS, RET_DK, RET_DV = 4, 64, 128
RET_ROPE_BASE = 10000.0
DIFF_HEADS, DIFF_DQK, DIFF_DV = 4, 64, 128
ROPE_THETA = 500000.0
ROPE_DIM = DIFF_DQK // 4
D_FF = 5632
LN_EPS = 1e-5
RMS_EPS = 1e-6
MASK_VALUE = -1e30
DN_ALPHA = (2 * DEPTH) ** 0.25

OFF_HQ, OFF_HF, OFF_HI, OFF_HG = 0, 512, 1024, 1536
OFF_SU = 2048
OFF_RQ, OFF_RK, OFF_RV, OFF_RG = 2560, 2816, 3072, 3584
OFF_DQ, OFF_DK, OFF_DV = 4096, 4608, 5120
OFF_GL = 5632
MIX_WIDTH = OFF_GL
GATE_WIDTH = N_BRANCH * D_MODEL

VMEM_LIMIT_BYTES = 56 * 1024 * 1024

S5_T = 16
S5_LG = 8
HG_SUB = 16
HG_TC = 512
RET_C = 128
ATT_T = 512
LOG2_E = math.log2(math.e)


def _cparams(sem):
    return pltpu.CompilerParams(dimension_semantics=sem, vmem_limit_bytes=VMEM_LIMIT_BYTES)


def _layer_norm(y, g, b):
    mu = jnp.mean(y, axis=-1, keepdims=True)
    d = y - mu
    var = jnp.mean(d * d, axis=-1, keepdims=True)
    return d * lax.rsqrt(var + LN_EPS) * g + b


def _rms(o):
    return o * lax.rsqrt(jnp.mean(o * o, axis=-1, keepdims=True) + RMS_EPS)


def _silu(x):
    return x * jax.nn.sigmoid(x)


def _dot(a, b):
    return jnp.dot(a, b, preferred_element_type=F32)


def _dot_nt(a, b):
    return lax.dot_general(a, b, (((1,), (1,)), ((), ())), preferred_element_type=F32)


def _dot_tn(a, b):
    return lax.dot_general(a, b, (((0,), (0,)), ((), ())), preferred_element_type=F32)


def _ffn_kernel(x_ref, w1_ref, w3_ref, w2_ref, g_ref, b_ref, o_ref, xb_ref):
    f = pl.program_id(1)

    @pl.when(f == 0)
    def _():
        xb_ref[...] = x_ref[...].astype(BF16)
        o_ref[...] = jnp.zeros_like(o_ref)

    xb = xb_ref[...]
    h1 = _dot(xb, w1_ref[...].astype(BF16))
    h3 = _dot(xb, w3_ref[...].astype(BF16))
    h = (_silu(h1) * h3).astype(BF16)
    o_ref[...] += _dot(h, w2_ref[...].astype(BF16))

    @pl.when(f == pl.num_programs(1) - 1)
    def _():
        y = DN_ALPHA * x_ref[...] + 0.5 * o_ref[...]
        o_ref[...] = _layer_norm(y, g_ref[...], b_ref[...])


def _ffn_ln(x, w1, w3, w2, g, b, l, *, tm=1024, tf=256):
    s, d = x.shape
    f = w1.shape[2]
    tm = min(tm, s)
    return pl.pallas_call(
        _ffn_kernel,
        out_shape=jax.ShapeDtypeStruct((s, d), F32),
        grid=(s // tm, f // tf),
        in_specs=[
            pl.BlockSpec((tm, d), lambda i, j: (i, 0), pipeline_mode=pl.Buffered(1)),
            pl.BlockSpec((None, d, tf), lambda i, j: (l, 0, j)),
            pl.BlockSpec((None, d, tf), lambda i, j: (l, 0, j)),
            pl.BlockSpec((None, tf, d), lambda i, j: (l, j, 0)),
            pl.BlockSpec((1, d), lambda i, j: (0, 0)),
            pl.BlockSpec((1, d), lambda i, j: (0, 0)),
        ],
        out_specs=pl.BlockSpec((tm, d), lambda i, j: (i, 0)),
        scratch_shapes=[pltpu.VMEM((tm, d), BF16)],
        compiler_params=_cparams(("parallel", "arbitrary")),
        name="ffn_ln",
    )(x, w1, w3, w2, g.reshape(1, d), b.reshape(1, d))


def _proj_kernel(x_ref, w_ref, o_ref, xb_ref, *, gate):
    @pl.when(pl.program_id(1) == 0)
    def _():
        xb_ref[...] = x_ref[...].astype(BF16)

    y = _dot(xb_ref[...], w_ref[...].astype(BF16))
    if gate:
        y = jax.nn.sigmoid(y)
    o_ref[...] = y.astype(o_ref.dtype)


def _in_proj(x, w_in, l, col0, width, out_dtype, gate, *, tm=2048, tn=512):
    s, d = x.shape
    tm = min(tm, s)
    cb0 = col0 // tn
    return pl.pallas_call(
        functools.partial(_proj_kernel, gate=gate),
        out_shape=jax.ShapeDtypeStruct((s, width), out_dtype),
        grid=(s // tm, width // tn),
        in_specs=[
            pl.BlockSpec((tm, d), lambda i, j: (i, 0), pipeline_mode=pl.Buffered(1)),
            pl.BlockSpec((None, d, tn), lambda i, j: (l, 0, cb0 + j)),
        ],
        out_specs=pl.BlockSpec((tm, tn), lambda i, j: (i, j)),
        scratch_shapes=[pltpu.VMEM((tm, d), BF16)],
        compiler_params=_cparams(("parallel", "arbitrary")),
        name="in_proj_gate" if gate else "in_proj_mix",
    )(x, w_in)


def _cumsum_rows(x, row):
    n = x.shape[0]
    shift = 1
    while shift < n:
        x = x + jnp.where(row >= shift, pltpu.roll(x, shift, axis=0), 0.0)
        shift *= 2
    return x


def _hgrn2_kernel(q_ref, f_ref, i_ref, g_ref, lb_ref, gain_ref, o_ref, st_ref, *, sub, unroll):
    @pl.when(pl.program_id(0) == 0)
    def _():
        st_ref[...] = jnp.zeros_like(st_ref)

    half = sub // 2
    row = lax.broadcasted_iota(jnp.int32, (sub, HG_DK), 0)
    row_h = lax.broadcasted_iota(jnp.int32, (half, HG_DK), 0)
    n_sub = q_ref.shape[0] // sub

    def pair_sum(qh, bh, ks, bs, vs, mask_from):
        dec = jnp.exp(bh - bs)
        if mask_from is not None:
            dec = jnp.where(row_h >= mask_from, dec, 0.0)
        return jnp.sum(qh * ks * dec, axis=-1, keepdims=True) * vs

    def head_step(rows, h):
        cs = slice(h * HG_DK, (h + 1) * HG_DK)
        lb = lb_ref[:, cs]
        fl = f_ref[rows, cs]
        q = _silu(q_ref[rows, cs])
        v = i_ref[rows, cs]
        log_f = jax.nn.log_sigmoid(fl) + jnp.log1p(lb * jnp.exp(-fl))
        k = (1.0 - lb) * jax.nn.sigmoid(-fl)
        b = _cumsum_rows(log_f, row)

        st = st_ref[h]
        o = _dot_nt((q * jnp.exp(b)).astype(BF16), st.astype(BF16))
        q_t, q_b, b_t, b_b = q[:half], q[half:], b[:half], b[half:]
        o_t, o_b = o[:half], o[half:]
        for s_ in range(half):
            ks, bs, vs = k[s_:s_ + 1], b[s_:s_ + 1], v[s_:s_ + 1]
            o_t = o_t + pair_sum(q_t, b_t, ks, bs, vs, s_ if s_ > 0 else None)
            o_b = o_b + pair_sum(q_b, b_b, ks, bs, vs, None)
        for s_ in range(half):
            r = half + s_
            ks, bs, vs = k[r:r + 1], b[r:r + 1], v[r:r + 1]
            o_b = o_b + pair_sum(q_b, b_b, ks, bs, vs, s_ if s_ > 0 else None)
        o = jnp.concatenate([o_t, o_b], axis=0)

        b_end = b[sub - 1:sub]
        kd = (k * jnp.exp(b_end - b)).astype(BF16)
        st_ref[h] = st * jnp.exp(b_end) + _dot_tn(v.astype(BF16), kd)

        y = _rms(o) * gain_ref[:, cs] * _silu(g_ref[rows, cs])
        o_ref[rows, cs] = y.astype(o_ref.dtype)

    def body(c, carry):
        rows = pl.ds(pl.multiple_of(c * sub, sub), sub)
        for h in range(HG_HEADS):
            head_step(rows, h)
        return carry

    lax.fori_loop(0, n_sub, body, 0, unroll=unroll)


def _hgrn2(proj, lb, gain, *, tc=HG_TC, sub=HG_SUB, unroll=2):
    s = proj.shape[0]
    tc = min(tc, s)
    w = HG_HEADS * HG_DK

    def col(off):
        return pl.BlockSpec((tc, w), lambda t: (t, off // w))

    return pl.pallas_call(
        functools.partial(_hgrn2_kernel, sub=sub, unroll=unroll),
        out_shape=jax.ShapeDtypeStruct((s, HG_HEADS * HG_DV), BF16),
        grid=(s // tc,),
        in_specs=[col(OFF_HQ), col(OFF_HF), col(OFF_HI), col(OFF_HG),
                  pl.BlockSpec((1, w), lambda t: (0, 0)),
                  pl.BlockSpec((1, w), lambda t: (0, 0))],
        out_specs=pl.BlockSpec((tc, w), lambda t: (t, 0)),
        scratch_shapes=[pltpu.VMEM((HG_HEADS, HG_DV, HG_DK), F32)],
        compiler_params=_cparams(("arbitrary",)),
        name="hgrn2",
    )(proj, proj, proj, proj, lb.reshape(1, -1), gain.reshape(1, -1))


def _s5_tables(lam_re, lam_im, log_dt, b_re, b_im, c_re, c_im, d_skip):
    hi = lax.Precision.HIGHEST
    g, p, cg, t = S5_GROUPS, S5_STATE, S5_GROUP, S5_T
    dt = jnp.exp(log_dt.astype(F32))[:, None]
    lam_re = lam_re.astype(F32)
    lam_im = lam_im.astype(F32)
    mag = jnp.exp(dt * lam_re)
    ab_re = mag * jnp.cos(dt * lam_im)
    ab_im = mag * jnp.sin(dt * lam_im)
    den = jnp.square(lam_re) + jnp.square(lam_im)
    nr = ab_re - 1.0
    coef_re = (nr * lam_re + ab_im * lam_im) / den
    coef_im = (ab_im * lam_re - nr * lam_im) / den
    b_re = b_re.astype(F32)
    b_im = b_im.astype(F32)
    bb_re = coef_re[..., None] * b_re - coef_im[..., None] * b_im
    bb_im = coef_re[..., None] * b_im + coef_im[..., None] * b_re
    c_re = c_re.astype(F32)
    c_im = c_im.astype(F32)
    tau = jnp.arange(t + 1, dtype=F32)[:, None, None]
    pmag = jnp.exp(tau * (dt * lam_re)[None])
    pw_re = pmag * jnp.cos(tau * (dt * lam_im)[None])
    pw_im = pmag * jnp.sin(tau * (dt * lam_im)[None])
    lb_re = pw_re[:t, :, :, None] * bb_re[None] - pw_im[:t, :, :, None] * bb_im[None]
    lb_im = pw_re[:t, :, :, None] * bb_im[None] + pw_im[:t, :, :, None] * bb_re[None]
    ktau = (jnp.einsum('gcp,tgpd->tgcd', c_re, lb_re, precision=hi)
            - jnp.einsum('gcp,tgpd->tgcd', c_im, lb_im, precision=hi))
    idx = jnp.arange(t)
    diff = idx[None, :] - idx[:, None]
    toep = jnp.where((diff >= 0)[:, :, None, None, None],
                     ktau[jnp.maximum(diff, 0)], 0.0)
    m_op = toep.transpose(2, 0, 4, 1, 3).reshape(g, t * cg, t * cg)
    rev = lb_re[::-1], lb_im[::-1]
    p_op = jnp.concatenate([rev[0].transpose(1, 0, 3, 2), rev[1].transpose(1, 0, 3, 2)],
                           axis=-1).reshape(g, t * cg, 2 * p)
    cl_re = c_re[None] * pw_re[1:, :, None, :] - c_im[None] * pw_im[1:, :, None, :]
    cl_im = c_re[None] * pw_im[1:, :, None, :] + c_im[None] * pw_re[1:, :, None, :]
    q_op = jnp.concatenate([cl_re.transpose(1, 3, 0, 2), -cl_im.transpose(1, 3, 0, 2)],
                           axis=1).reshape(g, 2 * p, t * cg)
    a_blk = jnp.concatenate([pw_re[t], pw_im[t]], axis=-1)
    lg, nq = S5_LG, S5_GROUPS // S5_LG
    eye = jnp.eye(lg, dtype=F32)
    m6 = m_op.reshape(nq, lg, t, cg, t, cg)
    big_v = jnp.einsum('qjsdtc,jk->qsjdtkc', m6, eye).reshape(nq, t * lg * cg, t * lg * cg)
    p5 = p_op.reshape(nq, lg, t, cg, 2 * p)
    big_p = jnp.einsum('qjsdp,jk->qsjdkp', p5, eye).reshape(nq, t * lg * cg, lg * 2 * p)
    q5 = q_op.reshape(nq, lg, 2 * p, t, cg)
    big_q = jnp.einsum('qjptc,jk->qjptkc', q5, eye).reshape(nq, lg * 2 * p, t * lg * cg)
    a_tile = a_blk.reshape(nq, 1, lg * 2 * p)
    d_tile = d_skip.astype(F32).reshape(nq, 1, lg * cg)
    return big_v.astype(BF16), big_p.astype(BF16), big_q.astype(BF16), a_tile, d_tile


def _s5_kernel(*refs):
    t_blk, w = S5_T, S5_LG * S5_GROUP
    u_refs = refs[:t_blk]
    v_ref, p_ref, q_ref, a_ref, d_ref, o_ref = refs[t_blk:]
    u = [r[...] for r in u_refs]
    lhs = jnp.concatenate(u, axis=-1).astype(BF16)
    n = lhs.shape[0]
    half = S5_STATE
    incr = _dot(lhs, p_ref[...])
    row = lax.broadcasted_iota(jnp.int32, (n, 2 * half), 0)
    lane = lax.broadcasted_iota(jnp.int32, (1, 2 * half), 1)
    carried = []
    for j in range(S5_LG):
        h = incr[:, j * 2 * half:(j + 1) * 2 * half]
        a = a_ref[:, j * 2 * half:(j + 1) * 2 * half]
        shift = 1
        while shift < n:
            a_sw = pltpu.roll(a, half, axis=1)
            a1 = jnp.where(lane < half, a, a_sw)
            a2 = jnp.where(lane < half, -a_sw, a)
            x = jnp.where(row >= shift, pltpu.roll(h, shift, axis=0), 0.0)
            h = h + a1 * x + a2 * pltpu.roll(x, half, axis=1)
            a = a1 * a + a2 * a_sw
            shift *= 2
        carried.append(jnp.where(row >= 1, pltpu.roll(h, 1, axis=0), 0.0).astype(BF16))
    y = _dot(lhs, v_ref[...]) + _dot(jnp.concatenate(carried, axis=-1), q_ref[...])
    d = d_ref[...]
    for t in range(t_blk):
        yt = y[:, t * w:(t + 1) * w] + d * u[t]
        o_ref[:, t * w:(t + 1) * w] = jax.nn.gelu(yt, approximate=True).astype(o_ref.dtype)


def _s5(proj, tables):
    s, width = proj.shape
    t = S5_T
    w = S5_LG * S5_GROUP
    nq = S5_CH // w
    n = s // t
    big_v, big_p, big_q, a_tile, d_tile = tables
    proj_blocks = proj.reshape(n, t * width)

    def step_cols(ti):
        return pl.BlockSpec((n, w), lambda q: (0, ti * (width // w) + OFF_SU // w + q))

    def per_tile(arr):
        return pl.BlockSpec((None,) + arr.shape[1:], lambda q: (q, 0, 0))

    y = pl.pallas_call(
        _s5_kernel,
        out_shape=jax.ShapeDtypeStruct((nq, n, t * w), BF16),
        grid=(nq,),
        in_specs=[step_cols(ti) for ti in range(t)] + [per_tile(big_v), per_tile(big_p),
                                                       per_tile(big_q), per_tile(a_tile),
                                                       per_tile(d_tile)],
        out_specs=pl.BlockSpec((None, n, t * w), lambda q: (q, 0, 0)),
        compiler_params=_cparams(("parallel",)),
        name="s5",
    )(*([proj_blocks] * t), big_v, big_p, big_q, a_tile, d_tile)
    return y.reshape(nq, s, w)


def _rope_kernel(pos_ref, fr_ref, fd_ref, cr_ref, sr_ref, cd_ref, sd_ref):
    pos = pos_ref[...].astype(F32)
    ang_r = pos * fr_ref[...]
    lane_r = lax.broadcasted_iota(jnp.int32, ang_r.shape, 1)
    cr_ref[...] = jnp.cos(ang_r)
    sr_ref[...] = jnp.where(lane_r % RET_DK < RET_DK // 2, -1.0, 1.0) * jnp.sin(ang_r)
    ang_d = pos * fd_ref[...]
    lane_d = lax.broadcasted_iota(jnp.int32, ang_d.shape, 1)
    cd_ref[...] = jnp.cos(ang_d)
    sd_ref[...] = jnp.where(lane_d % DIFF_DQK < ROPE_DIM // 2, -1.0, 1.0) * jnp.sin(ang_d)


def _rope_tables(positions, *, tb=512):
    s = positions.shape[0]
    tb = min(tb, s)
    half_r = RET_DK // 2
    inv_r = 1.0 / jnp.power(RET_ROPE_BASE, jnp.arange(half_r, dtype=F32) * (2.0 / RET_DK))
    fr = jnp.tile(inv_r, RET_HEADS * RET_DK // half_r).reshape(1, RET_HEADS * RET_DK)
    half_d = ROPE_DIM // 2
    inv_d = 1.0 / jnp.power(ROPE_THETA, jnp.arange(half_d, dtype=F32) * (2.0 / ROPE_DIM))
    fd_head = jnp.concatenate([inv_d, inv_d, jnp.zeros((DIFF_DQK - ROPE_DIM,), F32)])
    fd = jnp.tile(fd_head, 2).reshape(1, 2 * DIFF_DQK)
    wr, wd = fr.shape[1], fd.shape[1]
    return pl.pallas_call(
        _rope_kernel,
        out_shape=[jax.ShapeDtypeStruct((s, wr), F32), jax.ShapeDtypeStruct((s, wr), F32),
                   jax.ShapeDtypeStruct((s, wd), F32), jax.ShapeDtypeStruct((s, wd), F32)],
        grid=(s // tb,),
        in_specs=[pl.BlockSpec((tb, 1), lambda i: (i, 0)),
                  pl.BlockSpec((1, wr), lambda i: (0, 0)),
                  pl.BlockSpec((1, wd), lambda i: (0, 0))],
        out_specs=[pl.BlockSpec((tb, wr), lambda i: (i, 0)), pl.BlockSpec((tb, wr), lambda i: (i, 0)),
                   pl.BlockSpec((tb, wd), lambda i: (i, 0)), pl.BlockSpec((tb, wd), lambda i: (i, 0))],
        compiler_params=_cparams(("parallel",)),
        name="rope_tables",
    )(positions.reshape(s, 1), fr, fd)


def _swap_halves(x, group, half):
    n = x.shape[-1]
    lane = lax.broadcasted_iota(jnp.int32, x.shape, x.ndim - 1)
    return jnp.where(lane % group < half,
                     pltpu.roll(x, n - half, axis=x.ndim - 1),
                     pltpu.roll(x, half, axis=x.ndim - 1))


def _ret_kernel(q_ref, k_ref, v_ref, g_ref, cos_ref, sin_ref, gain_ref, o_ref, r_ref):
    @pl.when(pl.program_id(0) == 0)
    def _():
        r_ref[...] = jnp.zeros_like(r_ref)

    c = q_ref.shape[0]
    cos = cos_ref[...]
    sin = sin_ref[...]
    q = q_ref[...]
    k = k_ref[...]
    q = q * cos + _swap_halves(q, RET_DK, RET_DK // 2) * sin
    k = (k * cos + _swap_halves(k, RET_DK, RET_DK // 2) * sin) * (RET_DK ** -0.5)
    ti = lax.broadcasted_iota(jnp.int32, (c, c), 0)
    si = lax.broadcasted_iota(jnp.int32, (c, c), 1)
    tri = ti >= si
    rel = jnp.where(tri, ti - si, 0).astype(F32)
    idx = lax.broadcasted_iota(jnp.int32, (c, 1), 0).astype(F32)
    gain = gain_ref[...]
    for h in range(RET_HEADS):
        log_gamma = math.log1p(-(2.0 ** (-5.0 - h)))
        qh = q[:, h * RET_DK:(h + 1) * RET_DK]
        kh = k[:, h * RET_DK:(h + 1) * RET_DK]
        vh = v_ref[:, h * RET_DV:(h + 1) * RET_DV]
        d_intra = jnp.where(tri, jnp.exp(rel * log_gamma), 0.0)
        scores = _dot_nt(qh.astype(BF16), kh.astype(BF16)) * d_intra
        r = r_ref[h]
        xi = jnp.exp((idx + 1.0) * log_gamma)
        o = _dot(scores.astype(BF16), vh.astype(BF16)) + _dot(qh.astype(BF16), r.astype(BF16)) * xi
        zeta = jnp.exp((c - 1.0 - idx) * log_gamma)
        r_ref[h] = math.exp(c * log_gamma) * r + _dot_tn((kh * zeta).astype(BF16), vh.astype(BF16))
        sl = slice(h * RET_DV, (h + 1) * RET_DV)
        y = _rms(o) * gain[:, sl] * _silu(g_ref[:, sl])
        o_ref[:, sl] = y.astype(o_ref.dtype)


def _retention(proj, cos_r, sin_r, gain, *, c=RET_C):
    s = proj.shape[0]
    c = min(c, s)
    wq, wv = RET_HEADS * RET_DK, RET_HEADS * RET_DV
    return pl.pallas_call(
        _ret_kernel,
        out_shape=jax.ShapeDtypeStruct((s, wv), BF16),
        grid=(s // c,),
        in_specs=[pl.BlockSpec((c, wq), lambda i: (i, OFF_RQ // wq)),
                  pl.BlockSpec((c, wq), lambda i: (i, OFF_RK // wq)),
                  pl.BlockSpec((c, wv), lambda i: (i, OFF_RV // wv)),
                  pl.BlockSpec((c, wv), lambda i: (i, OFF_RG // wv)),
                  pl.BlockSpec((c, wq), lambda i: (i, 0)),
                  pl.BlockSpec((c, wq), lambda i: (i, 0)),
                  pl.BlockSpec((1, wv), lambda i: (0, 0))],
        out_specs=pl.BlockSpec((c, wv), lambda i: (i, 0)),
        scratch_shapes=[pltpu.VMEM((RET_HEADS, RET_DK, RET_DV), F32)],
        compiler_params=_cparams(("arbitrary",)),
        name="retention",
    )(proj, proj, proj, proj, cos_r, sin_r, gain.reshape(1, -1))


def _diff_prep_kernel(q_ref, k_ref, v_ref, cos_ref, sin_ref, qo_ref, ko_ref, vo_ref):
    reps = q_ref.shape[1] // cos_ref.shape[1]
    cos = jnp.concatenate([cos_ref[...]] * reps, axis=-1)
    sin = jnp.concatenate([sin_ref[...]] * reps, axis=-1)
    q = q_ref[...]
    k = k_ref[...]
    q = (q * cos + _swap_halves(q, DIFF_DQK, ROPE_DIM // 2) * sin) * (DIFF_DQK ** -0.5 * LOG2_E)
    k = k * cos + _swap_halves(k, DIFF_DQK, ROPE_DIM // 2) * sin
    qo_ref[...] = q.T.astype(BF16)
    ko_ref[...] = k.astype(BF16)
    vo_ref[...] = v_ref[...].T.astype(BF16)


def _diff_prep(proj, cos_d, sin_d, *, tb=512):
    s = proj.shape[0]
    tb = min(tb, s)
    w = 2 * DIFF_HEADS * DIFF_DQK
    wt = cos_d.shape[1]
    out_t = jax.ShapeDtypeStruct((w, s), BF16)
    return pl.pallas_call(
        _diff_prep_kernel,
        out_shape=[out_t, jax.ShapeDtypeStruct((s, w), BF16), out_t],
        grid=(s // tb,),
        in_specs=[pl.BlockSpec((tb, w), lambda i: (i, OFF_DQ // w)),
                  pl.BlockSpec((tb, w), lambda i: (i, OFF_DK // w)),
                  pl.BlockSpec((tb, w), lambda i: (i, OFF_DV // w)),
                  pl.BlockSpec((tb, wt), lambda i: (i, 0)),
                  pl.BlockSpec((tb, wt), lambda i: (i, 0))],
        out_specs=[pl.BlockSpec((w, tb), lambda i: (0, i)),
                   pl.BlockSpec((tb, w), lambda i: (i, 0)),
                   pl.BlockSpec((w, tb), lambda i: (0, i))],
        compiler_params=_cparams(("parallel",)),
        name="diff_prep",
    )(proj, proj, proj, cos_d, sin_d)


def _diff_attn_kernel(qt_ref, k_ref, vt_ref, lq1_ref, lk1_ref, lq2_ref, lk2_ref, gain_ref, o_ref,
                      sa_ref, sb_ref, m0_ref, m1_ref, l0_ref, l1_ref, a0_ref, a1_ref, *, lambda_init):
    i = pl.program_id(1)
    tq = qt_ref.shape[1]
    qt = qt_ref[...]
    feat = lax.broadcasted_iota(jnp.int32, qt.shape, 0)
    zero = jnp.zeros_like(qt)
    qts = (jnp.where(feat < DIFF_DQK, qt, zero), jnp.where(feat >= DIFF_DQK, qt, zero))
    m_refs, l_refs, a_refs = (m0_ref, m1_ref), (l0_ref, l1_ref), (a0_ref, a1_ref)
    for mp in range(2):
        m_refs[mp][...] = jnp.full_like(m_refs[mp], MASK_VALUE)
        l_refs[mp][...] = jnp.zeros_like(l_refs[mp])
        a_refs[mp][...] = jnp.zeros_like(a_refs[mp])

    def scores(t, s_ref):
        kb = k_ref[pl.ds(pl.multiple_of(t * tq, tq), tq), :]
        for mp in range(2):
            s_ref[mp] = _dot(kb, qts[mp])

    def accumulate(t, s_ref, diagonal):
        vtb = vt_ref[:, pl.ds(pl.multiple_of(t * tq, tq), tq)]
        for mp in range(2):
            s = s_ref[mp]
            if diagonal:
                ki = lax.broadcasted_iota(jnp.int32, (tq, tq), 0)
                qi = lax.broadcasted_iota(jnp.int32, (tq, tq), 1)
                s = jnp.where(ki <= qi, s, MASK_VALUE)
            m_old = m_refs[mp][...]
            m_new = jnp.maximum(m_old, jnp.max(s, axis=0, keepdims=True))
            alpha = jnp.exp2(m_old - m_new)
            p = jnp.exp2(s - m_new)
            l_refs[mp][...] = alpha * l_refs[mp][...] + jnp.sum(p.reshape(tq // 8, 8, tq), axis=0)
            a_refs[mp][...] = alpha * a_refs[mp][...] + _dot(vtb, p.astype(BF16))
            m_refs[mp][...] = m_new

    def pair(jj, carry):
        t = 2 * jj
        scores(t + 1, sb_ref)
        accumulate(t, sa_ref, False)
        scores(t + 2, sa_ref)
        accumulate(t + 1, sb_ref, False)
        return carry

    scores(0, sa_ref)
    lax.fori_loop(0, i // 2, pair, 0)

    @pl.when(i % 2 == 0)
    def _():
        accumulate(i, sa_ref, True)

    @pl.when(i % 2 == 1)
    def _():
        scores(i, sb_ref)
        accumulate(i - 1, sa_ref, False)
        accumulate(i, sb_ref, True)

    lam = (jnp.exp(jnp.sum(lq1_ref[...] * lk1_ref[...], axis=-1, keepdims=True))
           - jnp.exp(jnp.sum(lq2_ref[...] * lk2_ref[...], axis=-1, keepdims=True)) + lambda_init)

    def normalised(mp):
        return a_refs[mp][...] / jnp.sum(l_refs[mp][...], axis=0, keepdims=True)

    o = normalised(0) - lam * normalised(1)
    o = o * lax.rsqrt(jnp.mean(o * o, axis=0, keepdims=True) + RMS_EPS)
    y = o * gain_ref[...] * (1.0 - lambda_init)
    o_ref[...] = y.T.astype(o_ref.dtype)


def _diff_attention(qt, kr, vt, lq1, lk1, lq2, lk2, gain, lambda_init, *, tq=ATT_T):
    s = kr.shape[0]
    tq = min(tq, s)
    wh = 2 * DIFF_DQK
    lam_spec = pl.BlockSpec((1, DIFF_DQK), lambda h, i: (0, 0))
    score = pltpu.VMEM((2, tq, tq), F32)
    stat = pltpu.VMEM((1, tq), F32)
    part = pltpu.VMEM((8, tq), F32)
    acc = pltpu.VMEM((DIFF_DV, tq), F32)
    return pl.pallas_call(
        functools.partial(_diff_attn_kernel, lambda_init=lambda_init),
        out_shape=jax.ShapeDtypeStruct((s, DIFF_HEADS * DIFF_DV), BF16),
        grid=(DIFF_HEADS, s // tq),
        in_specs=[pl.BlockSpec((wh, tq), lambda h, i: (h, i)),
                  pl.BlockSpec((s, wh), lambda h, i: (0, h)),
                  pl.BlockSpec((DIFF_DV, s), lambda h, i: (h, 0)),
                  lam_spec, lam_spec, lam_spec, lam_spec,
                  pl.BlockSpec((DIFF_DV, 1), lambda h, i: (h, 0))],
        out_specs=pl.BlockSpec((tq, DIFF_DV), lambda h, i: (i, h)),
        scratch_shapes=[score, score, stat, stat, part, part, acc, acc],
        compiler_params=_cparams(("parallel", "arbitrary")),
        name="diff_attention",
    )(qt, kr, vt, lq1.reshape(1, -1), lk1.reshape(1, -1), lq2.reshape(1, -1), lk2.reshape(1, -1),
      gain.reshape(-1, 1))


def _merge_kernel(x_ref, yh_ref, ys_ref, yr_ref, yd_ref, gt_ref, wh_ref, ws_ref, wr_ref, wd_ref,
                  wo_ref, g_ref, b_ref, o_ref, mg_ref, *, tn):
    d = D_MODEL
    yh = yh_ref[...]
    ys = jnp.concatenate([ys_ref[q] for q in range(ys_ref.shape[0])], axis=-1)
    yr = yr_ref[...]
    yd = yd_ref[...]
    for n in range(d // tn):
        c = slice(n * tn, (n + 1) * tn)
        c2 = slice(d + n * tn, d + (n + 1) * tn)
        up_h = _dot(yh, wh_ref[:, c])
        up_s = _dot(ys, ws_ref[:, c]) * jax.nn.sigmoid(_dot(ys, ws_ref[:, c2]))
        up_r = _dot(yr, wr_ref[:, c])
        up_d = _dot(yd, wd_ref[:, c])

        def gate(b):
            return gt_ref[:, b * d + n * tn:b * d + (n + 1) * tn].astype(F32)

        mg = gate(0) * up_h + gate(1) * up_s + gate(2) * up_r + gate(3) * up_d
        mg_ref[:, c] = mg.astype(BF16)
    y = DN_ALPHA * x_ref[...] + _dot(mg_ref[...], wo_ref[...])
    o_ref[...] = _layer_norm(y, g_ref[...], b_ref[...])


def _merge(x, yh, ys, yr, yd, gates, wh, ws, wr, wd, wo, g, b, l, *, tm=256, tn=512):
    s, d = x.shape
    tm = min(tm, s)
    wy = yh.shape[1]

    def rows(width):
        return pl.BlockSpec((tm, width), lambda i: (i, 0))

    def whole(arr):
        if arr.ndim == 3:
            return pl.BlockSpec((None,) + arr.shape[1:], lambda i: (l, 0, 0),
                                pipeline_mode=pl.Buffered(1))
        return pl.BlockSpec(arr.shape, lambda i: (0, 0), pipeline_mode=pl.Buffered(1))

    g2, b2 = g.reshape(1, d), b.reshape(1, d)
    return pl.pallas_call(
        functools.partial(_merge_kernel, tn=tn),
        out_shape=jax.ShapeDtypeStruct((s, d), F32),
        grid=(s // tm,),
        in_specs=[rows(d), rows(wy), pl.BlockSpec((ys.shape[0], tm, ys.shape[2]), lambda i: (0, i, 0)),
                  rows(wy), rows(wy), rows(GATE_WIDTH),
                  whole(wh), whole(ws), whole(wr), whole(wd), whole(wo), whole(g2), whole(b2)],
        out_specs=rows(d),
        scratch_shapes=[pltpu.VMEM((tm, d), BF16)],
        compiler_params=_cparams(("parallel",)),
        name="merge_out_ln",
    )(x, yh, ys, yr, yd, gates, wh, ws, wr, wd, wo, g2, b2)


def kernel(x, positions, ffa_w1, ffa_w3, ffa_w2, ln_a_g, ln_a_b, w_in, hg_lb_logits, hg_norm_g, s5_lam_re, s5_lam_im, s5_log_dt, s5_b_re, s5_b_im, s5_c_re, s5_c_im, s5_d, ret_norm_g, diff_lam_q1, diff_lam_k1, diff_lam_q2, diff_lam_k2, diff_norm_g, w_up_hg, w_up_s5, w_up_ret, w_up_diff, w_out, ln_m_g, ln_m_b, ffb_w1, ffb_w3, ffb_w2, ln_b_g, ln_b_b):
    bsz, s, d = x.shape
    assert bsz == 1 and d == D_MODEL
    depth = w_in.shape[0]
    x = x.reshape(s, d)

    p_lb = jax.nn.softmax(hg_lb_logits.astype(F32), axis=0)
    lower_bounds = jnp.maximum(jnp.cumsum(p_lb, axis=0) - p_lb[0], 0.0)
    cos_r, sin_r, cos_d, sin_d = _rope_tables(positions.reshape(s))

    w_up_hg, w_up_s5, w_up_ret, w_up_diff, w_out = [
        w.astype(BF16) for w in (w_up_hg, w_up_s5, w_up_ret, w_up_diff, w_out)]

    for l in range(depth):
        x = _ffn_ln(x, ffa_w1, ffa_w3, ffa_w2, ln_a_g[l], ln_a_b[l], l)
        proj = _in_proj(x, w_in, l, 0, MIX_WIDTH, F32, gate=False)
        gates = _in_proj(x, w_in, l, OFF_GL, GATE_WIDTH, BF16, gate=True)

        y_hg = _hgrn2(proj, lower_bounds[l], hg_norm_g[l])
        tables = _s5_tables(s5_lam_re[l], s5_lam_im[l], s5_log_dt[l], s5_b_re[l], s5_b_im[l],
                            s5_c_re[l], s5_c_im[l], s5_d[l])
        y_s5 = _s5(proj, tables)
        y_r = _retention(proj, cos_r, sin_r, ret_norm_g[l])
        lambda_init = 0.8 - 0.6 * math.exp(-0.3 * l)
        qt, kr, vt = _diff_prep(proj, cos_d, sin_d)
        y_d = _diff_attention(qt, kr, vt, diff_lam_q1[l], diff_lam_k1[l], diff_lam_q2[l],
                              diff_lam_k2[l], diff_norm_g[l], lambda_init)

        x = _merge(x, y_hg, y_s5, y_r, y_d, gates, w_up_hg, w_up_s5, w_up_ret, w_up_diff, w_out,
                   ln_m_g[l], ln_m_b[l], l)
        x = _ffn_ln(x, ffb_w1, ffb_w3, ffb_w2, ln_b_g[l], ln_b_b[l], l)
    return x.reshape(bsz, s, d)
```

```python
import functools
import math

import jax
import jax.numpy as jnp
from jax import lax
from jax.experimental import pallas as pl
from jax.experimental.pallas import tpu as pltpu

F32 = jnp.float32
BF16 = jnp.bfloat16

D_MODEL = 2048
DEPTH = 2
N_BRANCH = 4
HG_HEADS, HG_DK, HG_DV = 4, 128, 128
S5_CH, S5_GROUP, S5_STATE = 512, 16, 64
S5_GROUPS = S5_CH // S5_GROUP
RET_HEADS, RET_DK, RET_DV = 4, 64, 128
RET_ROPE_BASE = 10000.0
DIFF_HEADS, DIFF_DQK, DIFF_DV = 4, 64, 128
ROPE_THETA = 500000.0
ROPE_DIM = DIFF_DQK // 4
D_FF = 5632
LN_EPS = 1e-5
RMS_EPS = 1e-6
MASK_VALUE = -1e30
DN_ALPHA = (2 * DEPTH) ** 0.25

OFF_HQ, OFF_HF, OFF_HI, OFF_HG = 0, 512, 1024, 1536
OFF_SU = 2048
OFF_RQ, OFF_RK, OFF_RV, OFF_RG = 2560, 2816, 3072, 3584
OFF_DQ, OFF_DK, OFF_DV = 4096, 4608, 5120
OFF_GL = 5632
MIX_WIDTH = OFF_GL
GATE_WIDTH = N_BRANCH * D_MODEL

VMEM_LIMIT_BYTES = 56 * 1024 * 1024

S5_T = 16
S5_LG = 8
HG_SUB = 16
HG_TC = 512
RET_C = 128
ATT_T = 512
LOG2_E = math.log2(math.e)


def _cparams(sem):
    return pltpu.CompilerParams(dimension_semantics=sem, vmem_limit_bytes=VMEM_LIMIT_BYTES)


def _layer_norm(y, g, b):
    mu = jnp.mean(y, axis=-1, keepdims=True)
    d = y - mu
    var = jnp.mean(d * d, axis=-1, keepdims=True)
    return d * lax.rsqrt(var + LN_EPS) * g + b


def _rms(o):
    return o * lax.rsqrt(jnp.mean(o * o, axis=-1, keepdims=True) + RMS_EPS)


def _silu(x):
    return x * jax.nn.sigmoid(x)


def _dot(a, b):
    return jnp.dot(a, b, preferred_element_type=F32)


def _dot_nt(a, b):
    return lax.dot_general(a, b, (((1,), (1,)), ((), ())), preferred_element_type=F32)


def _dot_tn(a, b):
    return lax.dot_general(a, b, (((0,), (0,)), ((), ())), preferred_element_type=F32)


def _ffn_kernel(x_ref, w1_ref, w3_ref, w2_ref, g_ref, b_ref, o_ref, xb_ref):
    f = pl.program_id(1)

    @pl.when(f == 0)
    def _():
        xb_ref[...] = x_ref[...].astype(BF16)
        o_ref[...] = jnp.zeros_like(o_ref)

    xb = xb_ref[...]
    h1 = _dot(xb, w1_ref[...].astype(BF16))
    h3 = _dot(xb, w3_ref[...].astype(BF16))
    h = (_silu(h1) * h3).astype(BF16)
    o_ref[...] += _dot(h, w2_ref[...].astype(BF16))

    @pl.when(f == pl.num_programs(1) - 1)
    def _():
        y = DN_ALPHA * x_ref[...] + 0.5 * o_ref[...]
        o_ref[...] = _layer_norm(y, g_ref[...], b_ref[...])


def _ffn_ln(x, w1, w3, w2, g, b, l, *, tm=1024, tf=256):
    s, d = x.shape
    f = w1.shape[2]
    tm = min(tm, s)
    return pl.pallas_call(
        _ffn_kernel,
        out_shape=jax.ShapeDtypeStruct((s, d), F32),
        grid=(s // tm, f // tf),
        in_specs=[
            pl.BlockSpec((tm, d), lambda i, j: (i, 0), pipeline_mode=pl.Buffered(1)),
            pl.BlockSpec((None, d, tf), lambda i, j: (l, 0, j)),
            pl.BlockSpec((None, d, tf), lambda i, j: (l, 0, j)),
            pl.BlockSpec((None, tf, d), lambda i, j: (l, j, 0)),
            pl.BlockSpec((1, d), lambda i, j: (0, 0)),
            pl.BlockSpec((1, d), lambda i, j: (0, 0)),
        ],
        out_specs=pl.BlockSpec((tm, d), lambda i, j: (i, 0)),
        scratch_shapes=[pltpu.VMEM((tm, d), BF16)],
        compiler_params=_cparams(("parallel", "arbitrary")),
        name="ffn_ln",
    )(x, w1, w3, w2, g.reshape(1, d), b.reshape(1, d))


def _proj_kernel(x_ref, w_ref, o_ref, xb_ref, *, gate):
    @pl.when(pl.program_id(1) == 0)
    def _():
        xb_ref[...] = x_ref[...].astype(BF16)

    y = _dot(xb_ref[...], w_ref[...].astype(BF16))
    if gate:
        y = jax.nn.sigmoid(y)
    o_ref[...] = y.astype(o_ref.dtype)


def _in_proj(x, w_in, l, col0, width, out_dtype, gate, *, tm=2048, tn=512):
    s, d = x.shape
    tm = min(tm, s)
    cb0 = col0 // tn
    return pl.pallas_call(
        functools.partial(_proj_kernel, gate=gate),
        out_shape=jax.ShapeDtypeStruct((s, width), out_dtype),
        grid=(s // tm, width // tn),
        in_specs=[
            pl.BlockSpec((tm, d), lambda i, j: (i, 0), pipeline_mode=pl.Buffered(1)),
            pl.BlockSpec((None, d, tn), lambda i, j: (l, 0, cb0 + j)),
        ],
        out_specs=pl.BlockSpec((tm, tn), lambda i, j: (i, j)),
        scratch_shapes=[pltpu.VMEM((tm, d), BF16)],
        compiler_params=_cparams(("parallel", "arbitrary")),
        name="in_proj_gate" if gate else "in_proj_mix",
    )(x, w_in)


def _cumsum_rows(x, row):
    n = x.shape[0]
    shift = 1
    while shift < n:
        x = x + jnp.where(row >= shift, pltpu.roll(x, shift, axis=0), 0.0)
        shift *= 2
    return x


def _hgrn2_kernel(q_ref, f_ref, i_ref, g_ref, lb_ref, gain_ref, o_ref, st_ref, *, sub, unroll):
    @pl.when(pl.program_id(0) == 0)
    def _():
        st_ref[...] = jnp.zeros_like(st_ref)

    half = sub // 2
    row = lax.broadcasted_iota(jnp.int32, (sub, HG_DK), 0)
    row_h = lax.broadcasted_iota(jnp.int32, (half, HG_DK), 0)
    n_sub = q_ref.shape[0] // sub

    def pair_sum(qh, bh, ks, bs, vs, mask_from):
        dec = jnp.exp(bh - bs)
        if mask_from is not None:
            dec = jnp.where(row_h >= mask_from, dec, 0.0)
        return jnp.sum(qh * ks * dec, axis=-1, keepdims=True) * vs

    def head_step(rows, h):
        cs = slice(h * HG_DK, (h + 1) * HG_DK)
        lb = lb_ref[:, cs]
        fl = f_ref[rows, cs]
        q = _silu(q_ref[rows, cs])
        v = i_ref[rows, cs]
        log_f = jax.nn.log_sigmoid(fl) + jnp.log1p(lb * jnp.exp(-fl))
        k = (1.0 - lb) * jax.nn.sigmoid(-fl)
        b = _cumsum_rows(log_f, row)

        st = st_ref[h]
        o = _dot_nt((q * jnp.exp(b)).astype(BF16), st.astype(BF16))
        q_t, q_b, b_t, b_b = q[:half], q[half:], b[:half], b[half:]
        o_t, o_b = o[:half], o[half:]
        for s_ in range(half):
            ks, bs, vs = k[s_:s_ + 1], b[s_:s_ + 1], v[s_:s_ + 1]
            o_t = o_t + pair_sum(q_t, b_t, ks, bs, vs, s_ if s_ > 0 else None)
            o_b = o_b + pair_sum(q_b, b_b, ks, bs, vs, None)
        for s_ in range(half):
            r = half + s_
            ks, bs, vs = k[r:r + 1], b[r:r + 1], v[r:r + 1]
            o_b = o_b + pair_sum(q_b, b_b, ks, bs, vs, s_ if s_ > 0 else None)
        o = jnp.concatenate([o_t, o_b], axis=0)

        b_end = b[sub - 1:sub]
        kd = (k * jnp.exp(b_end - b)).astype(BF16)
        st_ref[h] = st * jnp.exp(b_end) + _dot_tn(v.astype(BF16), kd)

        y = _rms(o) * gain_ref[:, cs] * _silu(g_ref[rows, cs])
        o_ref[rows, cs] = y.astype(o_ref.dtype)

    def body(c, carry):
        rows = pl.ds(pl.multiple_of(c * sub, sub), sub)
        for h in range(HG_HEADS):
            head_step(rows, h)
        return carry

    lax.fori_loop(0, n_sub, body, 0, unroll=unroll)


def _hgrn2(proj, lb, gain, *, tc=HG_TC, sub=HG_SUB, unroll=2):
    s = proj.shape[0]
    tc = min(tc, s)
    w = HG_HEADS * HG_DK

    def col(off):
        return pl.BlockSpec((tc, w), lambda t: (t, off // w))

    return pl.pallas_call(
        functools.partial(_hgrn2_kernel, sub=sub, unroll=unroll),
        out_shape=jax.ShapeDtypeStruct((s, HG_HEADS * HG_DV), BF16),
        grid=(s // tc,),
        in_specs=[col(OFF_HQ), col(OFF_HF), col(OFF_HI), col(OFF_HG),
                  pl.BlockSpec((1, w), lambda t: (0, 0)),
                  pl.BlockSpec((1, w), lambda t: (0, 0))],
        out_specs=pl.BlockSpec((tc, w), lambda t: (t, 0)),
        scratch_shapes=[pltpu.VMEM((HG_HEADS, HG_DV, HG_DK), F32)],
        compiler_params=_cparams(("arbitrary",)),
        name="hgrn2",
    )(proj, proj, proj, proj, lb.reshape(1, -1), gain.reshape(1, -1))


def _s5_tables(lam_re, lam_im, log_dt, b_re, b_im, c_re, c_im, d_skip):
    hi = lax.Precision.HIGHEST
    g, p, cg, t = S5_GROUPS, S5_STATE, S5_GROUP, S5_T
    dt = jnp.exp(log_dt.astype(F32))[:, None]
    lam_re = lam_re.astype(F32)
    lam_im = lam_im.astype(F32)
    mag = jnp.exp(dt * lam_re)
    ab_re = mag * jnp.cos(dt * lam_im)
    ab_im = mag * jnp.sin(dt * lam_im)
    den = jnp.square(lam_re) + jnp.square(lam_im)
    nr = ab_re - 1.0
    coef_re = (nr * lam_re + ab_im * lam_im) / den
    coef_im = (ab_im * lam_re - nr * lam_im) / den
    b_re = b_re.astype(F32)
    b_im = b_im.astype(F32)
    bb_re = coef_re[..., None] * b_re - coef_im[..., None] * b_im
    bb_im = coef_re[..., None] * b_im + coef_im[..., None] * b_re
    c_re = c_re.astype(F32)
    c_im = c_im.astype(F32)
    tau = jnp.arange(t + 1, dtype=F32)[:, None, None]
    pmag = jnp.exp(tau * (dt * lam_re)[None])
    pw_re = pmag * jnp.cos(tau * (dt * lam_im)[None])
    pw_im = pmag * jnp.sin(tau * (dt * lam_im)[None])
    lb_re = pw_re[:t, :, :, None] * bb_re[None] - pw_im[:t, :, :, None] * bb_im[None]
    lb_im = pw_re[:t, :, :, None] * bb_im[None] + pw_im[:t, :, :, None] * bb_re[None]
    ktau = (jnp.einsum('gcp,tgpd->tgcd', c_re, lb_re, precision=hi)
            - jnp.einsum('gcp,tgpd->tgcd', c_im, lb_im, precision=hi))
    rev = lb_re[::-1], lb_im[::-1]
    p_op = jnp.concatenate([rev[0].transpose(1, 0, 3, 2), rev[1].transpose(1, 0, 3, 2)],
                           axis=-1).reshape(g, t * cg, 2 * p)
    cl_re = c_re[None] * pw_re[1:, :, None, :] - c_im[None] * pw_im[1:, :, None, :]
    cl_im = c_re[None] * pw_im[1:, :, None, :] + c_im[None] * pw_re[1:, :, None, :]
    q_op = jnp.concatenate([cl_re.transpose(1, 3, 0, 2), -cl_im.transpose(1, 3, 0, 2)],
                           axis=1).reshape(g, 2 * p, t * cg)
    a_blk = jnp.concatenate([pw_re[t], pw_im[t]], axis=-1)
    lg, nq = S5_LG, S5_GROUPS // S5_LG
    eye = jnp.eye(lg, dtype=F32)
    k5 = ktau.reshape(t, nq, lg, cg, cg)
    v_tau = jnp.einsum('tqjcd,jk->qtjdkc', k5, eye).reshape(nq, t, lg * cg, lg * cg)
    p_c = p_op.reshape(nq, lg, t, cg, 2 * p)
    q_c = q_op.reshape(nq, lg, 2 * p, t * cg)
    a_tile = a_blk.reshape(nq, 1, lg * 2 * p)
    d_tile = d_skip.astype(F32).reshape(nq, 1, lg * cg)
    return v_tau.astype(BF16), p_c.astype(BF16), q_c, a_tile, d_tile


def _s5_kernel(u_ref, v_ref, pc_ref, qc_ref, a_ref, d_ref, o_ref, bigv_ref, bigp_ref, bigq_ref):
    t_blk, w, lg = S5_T, S5_LG * S5_GROUP, S5_LG
    half = S5_STATE
    n = u_ref.shape[0] // t_blk

    lane_w = lax.broadcasted_iota(jnp.int32, (2 * half, w), 1)
    bigp_ref[...] = jnp.zeros_like(bigp_ref)
    for s in range(t_blk):
        for t in range(t_blk):
            blk = v_ref[t - s] if t >= s else jnp.zeros((w, w), BF16)
            bigv_ref[s * w:(s + 1) * w, t * w:(t + 1) * w] = blk
        for j in range(lg):
            bigp_ref[s * w + j * S5_GROUP:s * w + (j + 1) * S5_GROUP,
                     j * 2 * half:(j + 1) * 2 * half] = pc_ref[j, s]
    for j in range(lg):
        for t in range(t_blk):
            blk = qc_ref[j, :, (t // lg) * w:(t // lg + 1) * w]
            shift = ((j - t % lg) % lg) * S5_GROUP
            if shift:
                blk = pltpu.roll(blk, shift, axis=1)
            bigq_ref[j * 2 * half:(j + 1) * 2 * half, t * w:(t + 1) * w] = jnp.where(
                lane_w // S5_GROUP == j, blk, 0.0).astype(BF16)

    u = [u_ref[pl.ds(t, n, stride=t_blk), :] for t in range(t_blk)]
    lhs = jnp.concatenate(u, axis=-1).astype(BF16)
    incr = _dot(lhs, bigp_ref[...])
    row = lax.broadcasted_iota(jnp.int32, (n, 2 * half), 0)
    lane = lax.broadcasted_iota(jnp.int32, (1, 2 * half), 1)
    carried = []
    for j in range(S5_LG):
        h = incr[:, j * 2 * half:(j + 1) * 2 * half]
        a = a_ref[:, j * 2 * half:(j + 1) * 2 * half]
        shift = 1
        while shift < n:
            a_sw = pltpu.roll(a, half, axis=1)
            a1 = jnp.where(lane < half, a, a_sw)
            a2 = jnp.where(lane < half, -a_sw, a)
            x = jnp.where(row >= shift, pltpu.roll(h, shift, axis=0), 0.0)
            h = h + a1 * x + a2 * pltpu.roll(x, half, axis=1)
            a = a1 * a + a2 * a_sw
            shift *= 2
        carried.append(jnp.where(row >= 1, pltpu.roll(h, 1, axis=0), 0.0).astype(BF16))
    y = _dot(lhs, bigv_ref[...]) + _dot(jnp.concatenate(carried, axis=-1), bigq_ref[...])
    d = d_ref[...]
    for t in range(t_blk):
        yt = y[:, t * w:(t + 1) * w] + d * u[t]
        o_ref[pl.ds(t, n, stride=t_blk), :] = jax.nn.gelu(yt, approximate=True)


def _s5(proj, tables):
    s = proj.shape[0]
    t = S5_T
    w = S5_LG * S5_GROUP
    nq = S5_CH // w
    two_p = 2 * S5_STATE
    v_tau, p_c, q_c, a_tile, d_tile = tables

    def per_tile(arr):
        zeros = (0,) * (arr.ndim - 1)
        return pl.BlockSpec((None,) + arr.shape[1:], lambda q: (q,) + zeros)

    return pl.pallas_call(
        _s5_kernel,
        out_shape=jax.ShapeDtypeStruct((s, S5_CH), F32),
        grid=(nq,),
        in_specs=[pl.BlockSpec((s, w), lambda q: (0, OFF_SU // w + q)),
                  per_tile(v_tau), per_tile(p_c), per_tile(q_c), per_tile(a_tile), per_tile(d_tile)],
        out_specs=pl.BlockSpec((s, w), lambda q: (0, q)),
        scratch_shapes=[pltpu.VMEM((t * w, t * w), BF16), pltpu.VMEM((t * w, S5_LG * two_p), BF16),
                        pltpu.VMEM((S5_LG * two_p, t * w), BF16)],
        compiler_params=_cparams(("parallel",)),
        name="s5",
    )(proj, v_tau, p_c, q_c, a_tile, d_tile)


def _rope_kernel(pos_ref, fr_ref, fd_ref, cr_ref, sr_ref, cd_ref, sd_ref):
    pos = pos_ref[...].astype(F32)
    ang_r = pos * fr_ref[...]
    lane_r = lax.broadcasted_iota(jnp.int32, ang_r.shape, 1)
    cr_ref[...] = jnp.cos(ang_r)
    sr_ref[...] = jnp.where(lane_r % RET_DK < RET_DK // 2, -1.0, 1.0) * jnp.sin(ang_r)
    ang_d = pos * fd_ref[...]
    lane_d = lax.broadcasted_iota(jnp.int32, ang_d.shape, 1)
    cd_ref[...] = jnp.cos(ang_d)
    sd_ref[...] = jnp.where(lane_d % DIFF_DQK < ROPE_DIM // 2, -1.0, 1.0) * jnp.sin(ang_d)


def _rope_tables(positions, *, tb=512):
    s = positions.shape[0]
    tb = min(tb, s)
    half_r = RET_DK // 2
    inv_r = 1.0 / jnp.power(RET_ROPE_BASE, jnp.arange(half_r, dtype=F32) * (2.0 / RET_DK))
    fr = jnp.tile(inv_r, RET_HEADS * RET_DK // half_r).reshape(1, RET_HEADS * RET_DK)
    half_d = ROPE_DIM // 2
    inv_d = 1.0 / jnp.power(ROPE_THETA, jnp.arange(half_d, dtype=F32) * (2.0 / ROPE_DIM))
    fd_head = jnp.concatenate([inv_d, inv_d, jnp.zeros((DIFF_DQK - ROPE_DIM,), F32)])
    fd = jnp.tile(fd_head, 2).reshape(1, 2 * DIFF_DQK)
    wr, wd = fr.shape[1], fd.shape[1]
    return pl.pallas_call(
        _rope_kernel,
        out_shape=[jax.ShapeDtypeStruct((s, wr), F32), jax.ShapeDtypeStruct((s, wr), F32),
                   jax.ShapeDtypeStruct((s, wd), F32), jax.ShapeDtypeStruct((s, wd), F32)],
        grid=(s // tb,),
        in_specs=[pl.BlockSpec((tb, 1), lambda i: (i, 0)),
                  pl.BlockSpec((1, wr), lambda i: (0, 0)),
                  pl.BlockSpec((1, wd), lambda i: (0, 0))],
        out_specs=[pl.BlockSpec((tb, wr), lambda i: (i, 0)), pl.BlockSpec((tb, wr), lambda i: (i, 0)),
                   pl.BlockSpec((tb, wd), lambda i: (i, 0)), pl.BlockSpec((tb, wd), lambda i: (i, 0))],
        compiler_params=_cparams(("parallel",)),
        name="rope_tables",
    )(positions.reshape(s, 1), fr, fd)


def _swap_halves(x, group, half):
    n = x.shape[-1]
    lane = lax.broadcasted_iota(jnp.int32, x.shape, x.ndim - 1)
    return jnp.where(lane % group < half,
                     pltpu.roll(x, n - half, axis=x.ndim - 1),
                     pltpu.roll(x, half, axis=x.ndim - 1))


def _ret_kernel(q_ref, k_ref, v_ref, g_ref, cos_ref, sin_ref, gain_ref, o_ref, r_ref):
    @pl.when(pl.program_id(0) == 0)
    def _():
        r_ref[...] = jnp.zeros_like(r_ref)

    c = q_ref.shape[0]
    cos = cos_ref[...]
    sin = sin_ref[...]
    q = q_ref[...]
    k = k_ref[...]
    q = q * cos + _swap_halves(q, RET_DK, RET_DK // 2) * sin
    k = (k * cos + _swap_halves(k, RET_DK, RET_DK // 2) * sin) * (RET_DK ** -0.5)
    ti = lax.broadcasted_iota(jnp.int32, (c, c), 0)
    si = lax.broadcasted_iota(jnp.int32, (c, c), 1)
    tri = ti >= si
    rel = jnp.where(tri, ti - si, 0).astype(F32)
    idx = lax.broadcasted_iota(jnp.int32, (c, 1), 0).astype(F32)
    gain = gain_ref[...]
    for h in range(RET_HEADS):
        log_gamma = math.log1p(-(2.0 ** (-5.0 - h)))
        qh = q[:, h * RET_DK:(h + 1) * RET_DK]
        kh = k[:, h * RET_DK:(h + 1) * RET_DK]
        vh = v_ref[:, h * RET_DV:(h + 1) * RET_DV]
        d_intra = jnp.where(tri, jnp.exp(rel * log_gamma), 0.0)
        scores = _dot_nt(qh.astype(BF16), kh.astype(BF16)) * d_intra
        r = r_ref[h]
        xi = jnp.exp((idx + 1.0) * log_gamma)
        o = _dot(scores.astype(BF16), vh.astype(BF16)) + _dot(qh.astype(BF16), r.astype(BF16)) * xi
        zeta = jnp.exp((c - 1.0 - idx) * log_gamma)
        r_ref[h] = math.exp(c * log_gamma) * r + _dot_tn((kh * zeta).astype(BF16), vh.astype(BF16))
        sl = slice(h * RET_DV, (h + 1) * RET_DV)
        y = _rms(o) * gain[:, sl] * _silu(g_ref[:, sl])
        o_ref[:, sl] = y.astype(o_ref.dtype)


def _retention(proj, cos_r, sin_r, gain, *, c=RET_C):
    s = proj.shape[0]
    c = min(c, s)
    wq, wv = RET_HEADS * RET_DK, RET_HEADS * RET_DV
    return pl.pallas_call(
        _ret_kernel,
        out_shape=jax.ShapeDtypeStruct((s, wv), BF16),
        grid=(s // c,),
        in_specs=[pl.BlockSpec((c, wq), lambda i: (i, OFF_RQ // wq)),
                  pl.BlockSpec((c, wq), lambda i: (i, OFF_RK // wq)),
                  pl.BlockSpec((c, wv), lambda i: (i, OFF_RV // wv)),
                  pl.BlockSpec((c, wv), lambda i: (i, OFF_RG // wv)),
                  pl.BlockSpec((c, wq), lambda i: (i, 0)),
                  pl.BlockSpec((c, wq), lambda i: (i, 0)),
                  pl.BlockSpec((1, wv), lambda i: (0, 0))],
        out_specs=pl.BlockSpec((c, wv), lambda i: (i, 0)),
        scratch_shapes=[pltpu.VMEM((RET_HEADS, RET_DK, RET_DV), F32)],
        compiler_params=_cparams(("arbitrary",)),
        name="retention",
    )(proj, proj, proj, proj, cos_r, sin_r, gain.reshape(1, -1))


def _diff_prep_kernel(q_ref, k_ref, v_ref, cos_ref, sin_ref, qo_ref, ko_ref, vo_ref):
    reps = q_ref.shape[1] // cos_ref.shape[1]
    cos = jnp.concatenate([cos_ref[...]] * reps, axis=-1)
    sin = jnp.concatenate([sin_ref[...]] * reps, axis=-1)
    q = q_ref[...]
    k = k_ref[...]
    q = (q * cos + _swap_halves(q, DIFF_DQK, ROPE_DIM // 2) * sin) * (DIFF_DQK ** -0.5 * LOG2_E)
    k = k * cos + _swap_halves(k, DIFF_DQK, ROPE_DIM // 2) * sin
    qo_ref[...] = q.T.astype(BF16)
    ko_ref[...] = k.astype(BF16)
    vo_ref[...] = v_ref[...].T.astype(BF16)


def _diff_prep(proj, cos_d, sin_d, *, tb=512):
    s = proj.shape[0]
    tb = min(tb, s)
    w = 2 * DIFF_HEADS * DIFF_DQK
    wt = cos_d.shape[1]
    out_t = jax.ShapeDtypeStruct((w, s), BF16)
    return pl.pallas_call(
        _diff_prep_kernel,
        out_shape=[out_t, jax.ShapeDtypeStruct((s, w), BF16), out_t],
        grid=(s // tb,),
        in_specs=[pl.BlockSpec((tb, w), lambda i: (i, OFF_DQ // w)),
                  pl.BlockSpec((tb, w), lambda i: (i, OFF_DK // w)),
                  pl.BlockSpec((tb, w), lambda i: (i, OFF_DV // w)),
                  pl.BlockSpec((tb, wt), lambda i: (i, 0)),
                  pl.BlockSpec((tb, wt), lambda i: (i, 0))],
        out_specs=[pl.BlockSpec((w, tb), lambda i: (0, i)),
                   pl.BlockSpec((tb, w), lambda i: (i, 0)),
                   pl.BlockSpec((w, tb), lambda i: (0, i))],
        compiler_params=_cparams(("parallel",)),
        name="diff_prep",
    )(proj, proj, proj, cos_d, sin_d)


def _diff_attn_kernel(qt_ref, k_ref, vt_ref, lq1_ref, lk1_ref, lq2_ref, lk2_ref, gain_ref, o_ref,
                      sa_ref, sb_ref, m0_ref, m1_ref, l0_ref, l1_ref, a0_ref, a1_ref, *, lambda_init):
    i = pl.program_id(1)
    tq = qt_ref.shape[1]
    qt = qt_ref[...]
    feat = lax.broadcasted_iota(jnp.int32, qt.shape, 0)
    zero = jnp.zeros_like(qt)
    qts = (jnp.where(feat < DIFF_DQK, qt, zero), jnp.where(feat >= DIFF_DQK, qt, zero))
    m_refs, l_refs, a_refs = (m0_ref, m1_ref), (l0_ref, l1_ref), (a0_ref, a1_ref)
    for mp in range(2):
        m_refs[mp][...] = jnp.full_like(m_refs[mp], MASK_VALUE)
        l_refs[mp][...] = jnp.zeros_like(l_refs[mp])
        a_refs[mp][...] = jnp.zeros_like(a_refs[mp])

    def scores(t, s_ref):
        kb = k_ref[pl.ds(pl.multiple_of(t * tq, tq), tq), :]
        for mp in range(2):
            s_ref[mp] = _dot(kb, qts[mp])

    def accumulate(t, s_ref, diagonal):
        vtb = vt_ref[:, pl.ds(pl.multiple_of(t * tq, tq), tq)]
        for mp in range(2):
            s = s_ref[mp]
            if diagonal:
                ki = lax.broadcasted_iota(jnp.int32, (tq, tq), 0)
                qi = lax.broadcasted_iota(jnp.int32, (tq, tq), 1)
                s = jnp.where(ki <= qi, s, MASK_VALUE)
            m_old = m_refs[mp][...]
            m_new = jnp.maximum(m_old, jnp.max(s, axis=0, keepdims=True))
            alpha = jnp.exp2(m_old - m_new)
            p = jnp.exp2(s - m_new)
            l_refs[mp][...] = alpha * l_refs[mp][...] + jnp.sum(p.reshape(tq // 8, 8, tq), axis=0)
            a_refs[mp][...] = alpha * a_refs[mp][...] + _dot(vtb, p.astype(BF16))
            m_refs[mp][...] = m_new

    def pair(jj, carry):
        t = 2 * jj
        scores(t + 1, sb_ref)
        accumulate(t, sa_ref, False)
        scores(t + 2, sa_ref)
        accumulate(t + 1, sb_ref, False)
        return carry

    scores(0, sa_ref)
    lax.fori_loop(0, i // 2, pair, 0)

    @pl.when(i % 2 == 0)
    def _():
        accumulate(i, sa_ref, True)

    @pl.when(i % 2 == 1)
    def _():
        scores(i, sb_ref)
        accumulate(i - 1, sa_ref, False)
        accumulate(i, sb_ref, True)

    lam = (jnp.exp(jnp.sum(lq1_ref[...] * lk1_ref[...], axis=-1, keepdims=True))
           - jnp.exp(jnp.sum(lq2_ref[...] * lk2_ref[...], axis=-1, keepdims=True)) + lambda_init)

    def normalised(mp):
        return a_refs[mp][...] / jnp.sum(l_refs[mp][...], axis=0, keepdims=True)

    o = normalised(0) - lam * normalised(1)
    o = o * lax.rsqrt(jnp.mean(o * o, axis=0, keepdims=True) + RMS_EPS)
    y = o * gain_ref[...] * (1.0 - lambda_init)
    o_ref[...] = y.T.astype(o_ref.dtype)


def _diff_attention(qt, kr, vt, lq1, lk1, lq2, lk2, gain, lambda_init, *, tq=ATT_T):
    s = kr.shape[0]
    tq = min(tq, s)
    wh = 2 * DIFF_DQK
    lam_spec = pl.BlockSpec((1, DIFF_DQK), lambda h, i: (0, 0))
    score = pltpu.VMEM((2, tq, tq), F32)
    stat = pltpu.VMEM((1, tq), F32)
    part = pltpu.VMEM((8, tq), F32)
    acc = pltpu.VMEM((DIFF_DV, tq), F32)
    return pl.pallas_call(
        functools.partial(_diff_attn_kernel, lambda_init=lambda_init),
        out_shape=jax.ShapeDtypeStruct((s, DIFF_HEADS * DIFF_DV), BF16),
        grid=(DIFF_HEADS, s // tq),
        in_specs=[pl.BlockSpec((wh, tq), lambda h, i: (h, i)),
                  pl.BlockSpec((s, wh), lambda h, i: (0, h)),
                  pl.BlockSpec((DIFF_DV, s), lambda h, i: (h, 0)),
                  lam_spec, lam_spec, lam_spec, lam_spec,
                  pl.BlockSpec((DIFF_DV, 1), lambda h, i: (h, 0))],
        out_specs=pl.BlockSpec((tq, DIFF_DV), lambda h, i: (i, h)),
        scratch_shapes=[score, score, stat, stat, part, part, acc, acc],
        compiler_params=_cparams(("parallel", "arbitrary")),
        name="diff_attention",
    )(qt, kr, vt, lq1.reshape(1, -1), lk1.reshape(1, -1), lq2.reshape(1, -1), lk2.reshape(1, -1),
      gain.reshape(-1, 1))


def _merge_kernel(x_ref, yh_ref, ys_ref, yr_ref, yd_ref, gt_ref, wh_ref, ws_ref, wr_ref, wd_ref,
                  wo_ref, g_ref, b_ref, o_ref, mg_ref, *, tn):
    d = D_MODEL
    yh = yh_ref[...]
    ys = ys_ref[...].astype(BF16)
    yr = yr_ref[...]
    yd = yd_ref[...]
    for n in range(d // tn):
        c = slice(n * tn, (n + 1) * tn)
        c2 = slice(d + n * tn, d + (n + 1) * tn)
        up_h = _dot(yh, wh_ref[:, c])
        up_s = _dot(ys, ws_ref[:, c]) * jax.nn.sigmoid(_dot(ys, ws_ref[:, c2]))
        up_r = _dot(yr, wr_ref[:, c])
        up_d = _dot(yd, wd_ref[:, c])

        def gate(b):
            return gt_ref[:, b * d + n * tn:b * d + (n + 1) * tn].astype(F32)

        mg = gate(0) * up_h + gate(1) * up_s + gate(2) * up_r + gate(3) * up_d
        mg_ref[:, c] = mg.astype(BF16)
    y = DN_ALPHA * x_ref[...] + _dot(mg_ref[...], wo_ref[...])
    o_ref[...] = _layer_norm(y, g_ref[...], b_ref[...])


def _merge(x, yh, ys, yr, yd, gates, wh, ws, wr, wd, wo, g, b, l, *, tm=256, tn=512):
    s, d = x.shape
    tm = min(tm, s)
    wy = yh.shape[1]

    def rows(width):
        return pl.BlockSpec((tm, width), lambda i: (i, 0))

    def whole(arr):
        if arr.ndim == 3:
            return pl.BlockSpec((None,) + arr.shape[1:], lambda i: (l, 0, 0),
                                pipeline_mode=pl.Buffered(1))
        return pl.BlockSpec(arr.shape, lambda i: (0, 0), pipeline_mode=pl.Buffered(1))

    g2, b2 = g.reshape(1, d), b.reshape(1, d)
    return pl.pallas_call(
        functools.partial(_merge_kernel, tn=tn),
        out_shape=jax.ShapeDtypeStruct((s, d), F32),
        grid=(s // tm,),
        in_specs=[rows(d), rows(wy), rows(wy), rows(wy), rows(wy), rows(GATE_WIDTH),
                  whole(wh), whole(ws), whole(wr), whole(wd), whole(wo), whole(g2), whole(b2)],
        out_specs=rows(d),
        scratch_shapes=[pltpu.VMEM((tm, d), BF16)],
        compiler_params=_cparams(("parallel",)),
        name="merge_out_ln",
    )(x, yh, ys, yr, yd, gates, wh, ws, wr, wd, wo, g2, b2)


def kernel(x, positions, ffa_w1, ffa_w3, ffa_w2, ln_a_g, ln_a_b, w_in, hg_lb_logits, hg_norm_g, s5_lam_re, s5_lam_im, s5_log_dt, s5_b_re, s5_b_im, s5_c_re, s5_c_im, s5_d, ret_norm_g, diff_lam_q1, diff_lam_k1, diff_lam_q2, diff_lam_k2, diff_norm_g, w_up_hg, w_up_s5, w_up_ret, w_up_diff, w_out, ln_m_g, ln_m_b, ffb_w1, ffb_w3, ffb_w2, ln_b_g, ln_b_b):
    bsz, s, d = x.shape
    assert bsz == 1 and d == D_MODEL
    depth = w_in.shape[0]
    x = x.reshape(s, d)

    p_lb = jax.nn.softmax(hg_lb_logits.astype(F32), axis=0)
    lower_bounds = jnp.maximum(jnp.cumsum(p_lb, axis=0) - p_lb[0], 0.0)
    cos_r, sin_r, cos_d, sin_d = _rope_tables(positions.reshape(s))

    w_up_hg, w_up_s5, w_up_ret, w_up_diff, w_out = [
        w.astype(BF16) for w in (w_up_hg, w_up_s5, w_up_ret, w_up_diff, w_out)]

    for l in range(depth):
        x = _ffn_ln(x, ffa_w1, ffa_w3, ffa_w2, ln_a_g[l], ln_a_b[l], l)
        proj = _in_proj(x, w_in, l, 0, MIX_WIDTH, F32, gate=False)
        gates = _in_proj(x, w_in, l, OFF_GL, GATE_WIDTH, BF16, gate=True)

        y_hg = _hgrn2(proj, lower_bounds[l], hg_norm_g[l])
        tables = _s5_tables(s5_lam_re[l], s5_lam_im[l], s5_log_dt[l], s5_b_re[l], s5_b_im[l],
                            s5_c_re[l], s5_c_im[l], s5_d[l])
        y_s5 = _s5(proj, tables)
        y_r = _retention(proj, cos_r, sin_r, ret_norm_g[l])
        lambda_init = 0.8 - 0.6 * math.exp(-0.3 * l)
        qt, kr, vt = _diff_prep(proj, cos_d, sin_d)
        y_d = _diff_attention(qt, kr, vt, diff_lam_q1[l], diff_lam_k1[l], diff_lam_q2[l],
                              diff_lam_k2[l], diff_norm_g[l], lambda_init)

        x = _merge(x, y_hg, y_s5, y_r, y_d, gates, w_up_hg, w_up_s5, w_up_ret, w_up_diff, w_out,
                   ln_m_g[l], ln_m_b[l], l)
        x = _ffn_ln(x, ffb_w1, ffb_w3, ffb_w2, ln_b_g[l], ln_b_b[l], l)
    return x.reshape(bsz, s, d)
```

```python
import functools
import math

import jax
import jax.numpy as jnp
from jax import lax
from jax.experimental import pallas as pl
from jax.experimental.pallas import tpu as pltpu

F32 = jnp.float32
BF16 = jnp.bfloat16

D_MODEL = 2048
DEPTH = 2
N_BRANCH = 4
HG_HEADS, HG_DK, HG_DV = 4, 128, 128
S5_CH, S5_GROUP, S5_STATE = 512, 16, 64
S5_GROUPS = S5_CH // S5_GROUP
RET_HEADS, RET_DK, RET_DV = 4, 64, 128
RET_ROPE_BASE = 10000.0
DIFF_HEADS, DIFF_DQK, DIFF_DV = 4, 64, 128
ROPE_THETA = 500000.0
ROPE_DIM = DIFF_DQK // 4
D_FF = 5632
LN_EPS = 1e-5
RMS_EPS = 1e-6
MASK_VALUE = -1e30
DN_ALPHA = (2 * DEPTH) ** 0.25

OFF_HQ, OFF_HF, OFF_HI, OFF_HG = 0, 512, 1024, 1536
OFF_SU = 2048
OFF_RQ, OFF_RK, OFF_RV, OFF_RG = 2560, 2816, 3072, 3584
OFF_DQ, OFF_DK, OFF_DV = 4096, 4608, 5120
OFF_GL = 5632
MIX_WIDTH = OFF_GL
GATE_WIDTH = N_BRANCH * D_MODEL

VMEM_LIMIT_BYTES = 56 * 1024 * 1024

S5_T = 16
S5_LG = 8
HG_SUB = 16
HG_TC = 512
RET_C = 256
ATT_T = 512
PROJ_ROW_SPLIT = 4
LOG2_E = math.log2(math.e)


def _cparams(sem):
    return pltpu.CompilerParams(dimension_semantics=sem, vmem_limit_bytes=VMEM_LIMIT_BYTES)


def _layer_norm(y, g, b):
    mu = jnp.mean(y, axis=-1, keepdims=True)
    d = y - mu
    var = jnp.mean(d * d, axis=-1, keepdims=True)
    return d * lax.rsqrt(var + LN_EPS) * g + b


def _rms(o):
    return o * lax.rsqrt(jnp.mean(o * o, axis=-1, keepdims=True) + RMS_EPS)


def _silu(x):
    return x * jax.nn.sigmoid(x)


def _dot(a, b):
    return jnp.dot(a, b, preferred_element_type=F32)


def _dot_nt(a, b):
    return lax.dot_general(a, b, (((1,), (1,)), ((), ())), preferred_element_type=F32)


def _dot_tn(a, b):
    return lax.dot_general(a, b, (((0,), (0,)), ((), ())), preferred_element_type=F32)


def _ffn_kernel(x_ref, w1_ref, w3_ref, w2_ref, g_ref, b_ref, o_ref, xb_ref):
    f = pl.program_id(1)

    @pl.when(f == 0)
    def _():
        xb_ref[...] = x_ref[...].astype(BF16)
        o_ref[...] = jnp.zeros_like(o_ref)

    xb = xb_ref[...]
    h1 = _dot(xb, w1_ref[...].astype(BF16))
    h3 = _dot(xb, w3_ref[...].astype(BF16))
    h = (_silu(h1) * h3).astype(BF16)
    o_ref[...] += _dot(h, w2_ref[...].astype(BF16))

    @pl.when(f == pl.num_programs(1) - 1)
    def _():
        y = DN_ALPHA * x_ref[...] + 0.5 * o_ref[...]
        o_ref[...] = _layer_norm(y, g_ref[...], b_ref[...])


def _ffn_ln(x, w1, w3, w2, g, b, l, *, tm=1024, tf=256):
    s, d = x.shape
    f = w1.shape[2]
    tm = min(tm, s)
    return pl.pallas_call(
        _ffn_kernel,
        out_shape=jax.ShapeDtypeStruct((s, d), F32),
        grid=(s // tm, f // tf),
        in_specs=[
            pl.BlockSpec((tm, d), lambda i, j: (i, 0), pipeline_mode=pl.Buffered(1)),
            pl.BlockSpec((None, d, tf), lambda i, j: (l, 0, j)),
            pl.BlockSpec((None, d, tf), lambda i, j: (l, 0, j)),
            pl.BlockSpec((None, tf, d), lambda i, j: (l, j, 0)),
            pl.BlockSpec((1, d), lambda i, j: (0, 0)),
            pl.BlockSpec((1, d), lambda i, j: (0, 0)),
        ],
        out_specs=pl.BlockSpec((tm, d), lambda i, j: (i, 0)),
        scratch_shapes=[pltpu.VMEM((tm, d), BF16)],
        compiler_params=_cparams(("parallel", "arbitrary")),
        name="ffn_ln",
    )(x, w1, w3, w2, g.reshape(1, d), b.reshape(1, d))


def _proj_kernel(x_ref, w_ref, o_ref, xb_ref, *, gate):
    @pl.when(pl.program_id(1) == 0)
    def _():
        xb_ref[...] = x_ref[...].astype(BF16)

    w = w_ref[...].astype(BF16)
    rows = xb_ref.shape[0] // PROJ_ROW_SPLIT
    for r in range(PROJ_ROW_SPLIT):
        sl = slice(r * rows, (r + 1) * rows)
        y = _dot(xb_ref[sl, :], w)
        if gate:
            y = 0.5 * jnp.tanh(0.5 * y) + 0.5
        o_ref[sl, :] = y.astype(o_ref.dtype)


def _in_proj(x, w_in, l, col0, width, out_dtype, gate, *, tm=2048, tn=512):
    s, d = x.shape
    tm = min(tm, s)
    cb0 = col0 // tn
    return pl.pallas_call(
        functools.partial(_proj_kernel, gate=gate),
        out_shape=jax.ShapeDtypeStruct((s, width), out_dtype),
        grid=(s // tm, width // tn),
        in_specs=[
            pl.BlockSpec((tm, d), lambda i, j: (i, 0), pipeline_mode=pl.Buffered(1)),
            pl.BlockSpec((None, d, tn), lambda i, j: (l, 0, cb0 + j)),
        ],
        out_specs=pl.BlockSpec((tm, tn), lambda i, j: (i, j)),
        scratch_shapes=[pltpu.VMEM((tm, d), BF16)],
        compiler_params=_cparams(("parallel", "arbitrary")),
        name="in_proj_gate" if gate else "in_proj_mix",
    )(x, w_in)


def _cumsum_rows(x, row):
    n = x.shape[0]
    shift = 1
    while shift < n:
        x = x + jnp.where(row >= shift, pltpu.roll(x, shift, axis=0), 0.0)
        shift *= 2
    return x


def _hgrn2_kernel(q_ref, f_ref, i_ref, g_ref, lb_ref, gain_ref, o_ref, st_ref, *, sub, unroll):
    @pl.when(pl.program_id(0) == 0)
    def _():
        st_ref[...] = jnp.zeros_like(st_ref)

    half = sub // 2
    row = lax.broadcasted_iota(jnp.int32, (sub, HG_DK), 0)
    row_h = lax.broadcasted_iota(jnp.int32, (half, HG_DK), 0)
    n_sub = q_ref.shape[0] // sub

    def pair_sum(qh, bh, ks, bs, vs, mask_from):
        dec = jnp.exp(bh - bs)
        if mask_from is not None:
            dec = jnp.where(row_h >= mask_from, dec, 0.0)
        return jnp.sum(qh * ks * dec, axis=-1, keepdims=True) * vs

    def head_step(rows, h):
        cs = slice(h * HG_DK, (h + 1) * HG_DK)
        lb = lb_ref[:, cs]
        fl = f_ref[rows, cs]
        q = _silu(q_ref[rows, cs])
        v = i_ref[rows, cs]
        log_f = jax.nn.log_sigmoid(fl) + jnp.log1p(lb * jnp.exp(-fl))
        k = (1.0 - lb) * jax.nn.sigmoid(-fl)
        b = _cumsum_rows(log_f, row)

        st = st_ref[h]
        o = _dot_nt((q * jnp.exp(b)).astype(BF16), st.astype(BF16))
        q_t, q_b, b_t, b_b = q[:half], q[half:], b[:half], b[half:]
        o_t, o_b = o[:half], o[half:]
        for s_ in range(half):
            ks, bs, vs = k[s_:s_ + 1], b[s_:s_ + 1], v[s_:s_ + 1]
            o_t = o_t + pair_sum(q_t, b_t, ks, bs, vs, s_ if s_ > 0 else None)
            o_b = o_b + pair_sum(q_b, b_b, ks, bs, vs, None)
        for s_ in range(half):
            r = half + s_
            ks, bs, vs = k[r:r + 1], b[r:r + 1], v[r:r + 1]
            o_b = o_b + pair_sum(q_b, b_b, ks, bs, vs, s_ if s_ > 0 else None)
        o = jnp.concatenate([o_t, o_b], axis=0)

        b_end = b[sub - 1:sub]
        kd = (k * jnp.exp(b_end - b)).astype(BF16)
        st_ref[h] = st * jnp.exp(b_end) + _dot_tn(v.astype(BF16), kd)

        y = _rms(o) * gain_ref[:, cs] * _silu(g_ref[rows, cs])
        o_ref[rows, cs] = y.astype(o_ref.dtype)

    def body(c, carry):
        rows = pl.ds(pl.multiple_of(c * sub, sub), sub)
        for h in range(HG_HEADS):
            head_step(rows, h)
        return carry

    lax.fori_loop(0, n_sub, body, 0, unroll=unroll)


def _hgrn2(proj, lb, gain, *, tc=HG_TC, sub=HG_SUB, unroll=2):
    s = proj.shape[0]
    tc = min(tc, s)
    w = HG_HEADS * HG_DK

    def col(off):
        return pl.BlockSpec((tc, w), lambda t: (t, off // w))

    return pl.pallas_call(
        functools.partial(_hgrn2_kernel, sub=sub, unroll=unroll),
        out_shape=jax.ShapeDtypeStruct((s, HG_HEADS * HG_DV), BF16),
        grid=(s // tc,),
        in_specs=[col(OFF_HQ), col(OFF_HF), col(OFF_HI), col(OFF_HG),
                  pl.BlockSpec((1, w), lambda t: (0, 0)),
                  pl.BlockSpec((1, w), lambda t: (0, 0))],
        out_specs=pl.BlockSpec((tc, w), lambda t: (t, 0)),
        scratch_shapes=[pltpu.VMEM((HG_HEADS, HG_DV, HG_DK), F32)],
        compiler_params=_cparams(("arbitrary",)),
        name="hgrn2",
    )(proj, proj, proj, proj, lb.reshape(1, -1), gain.reshape(1, -1))


def _s5_tables(lam_re, lam_im, log_dt, b_re, b_im, c_re, c_im, d_skip):
    hi = lax.Precision.HIGHEST
    g, p, cg, t = S5_GROUPS, S5_STATE, S5_GROUP, S5_T
    dt = jnp.exp(log_dt.astype(F32))[:, None]
    lam_re = lam_re.astype(F32)
    lam_im = lam_im.astype(F32)
    mag = jnp.exp(dt * lam_re)
    ab_re = mag * jnp.cos(dt * lam_im)
    ab_im = mag * jnp.sin(dt * lam_im)
    den = jnp.square(lam_re) + jnp.square(lam_im)
    nr = ab_re - 1.0
    coef_re = (nr * lam_re + ab_im * lam_im) / den
    coef_im = (ab_im * lam_re - nr * lam_im) / den
    b_re = b_re.astype(F32)
    b_im = b_im.astype(F32)
    bb_re = coef_re[..., None] * b_re - coef_im[..., None] * b_im
    bb_im = coef_re[..., None] * b_im + coef_im[..., None] * b_re
    c_re = c_re.astype(F32)
    c_im = c_im.astype(F32)
    tau = jnp.arange(t + 1, dtype=F32)[:, None, None]
    pmag = jnp.exp(tau * (dt * lam_re)[None])
    pw_re = pmag * jnp.cos(tau * (dt * lam_im)[None])
    pw_im = pmag * jnp.sin(tau * (dt * lam_im)[None])
    lb_re = pw_re[:t, :, :, None] * bb_re[None] - pw_im[:t, :, :, None] * bb_im[None]
    lb_im = pw_re[:t, :, :, None] * bb_im[None] + pw_im[:t, :, :, None] * bb_re[None]
    ktau = (jnp.einsum('gcp,tgpd->tgcd', c_re, lb_re, precision=hi)
            - jnp.einsum('gcp,tgpd->tgcd', c_im, lb_im, precision=hi))
    rev = lb_re[::-1], lb_im[::-1]
    p_op = jnp.concatenate([rev[0].transpose(1, 0, 3, 2), rev[1].transpose(1, 0, 3, 2)],
                           axis=-1).reshape(g, t * cg, 2 * p)
    cl_re = c_re[None] * pw_re[1:, :, None, :] - c_im[None] * pw_im[1:, :, None, :]
    cl_im = c_re[None] * pw_im[1:, :, None, :] + c_im[None] * pw_re[1:, :, None, :]
    q_op = jnp.concatenate([cl_re.transpose(1, 3, 0, 2), -cl_im.transpose(1, 3, 0, 2)],
                           axis=1).reshape(g, 2 * p, t * cg)
    a_blk = jnp.concatenate([pw_re[t], pw_im[t]], axis=-1)
    lg, nq = S5_LG, S5_GROUPS // S5_LG
    eye = jnp.eye(lg, dtype=F32)
    k5 = ktau.reshape(t, nq, lg, cg, cg)
    v_tau = jnp.einsum('tqjcd,jk->qtjdkc', k5, eye).reshape(nq, t, lg * cg, lg * cg)
    p_c = p_op.reshape(nq, lg, t, cg, 2 * p)
    q_c = q_op.reshape(nq, lg, 2 * p, t * cg)
    a_tile = a_blk.reshape(nq, 1, lg * 2 * p)
    d_tile = d_skip.astype(F32).reshape(nq, 1, lg * cg)
    return v_tau.astype(BF16), p_c.astype(BF16), q_c, a_tile, d_tile


def _s5_kernel(u_ref, v_ref, pc_ref, qc_ref, a_ref, d_ref, o_ref, bigv_ref, bigp_ref, bigq_ref):
    t_blk, w, lg = S5_T, S5_LG * S5_GROUP, S5_LG
    half = S5_STATE
    n = u_ref.shape[0] // t_blk

    lane_w = lax.broadcasted_iota(jnp.int32, (2 * half, w), 1)
    bigp_ref[...] = jnp.zeros_like(bigp_ref)
    for s in range(t_blk):
        for t in range(t_blk):
            blk = v_ref[t - s] if t >= s else jnp.zeros((w, w), BF16)
            bigv_ref[s * w:(s + 1) * w, t * w:(t + 1) * w] = blk
        for j in range(lg):
            bigp_ref[s * w + j * S5_GROUP:s * w + (j + 1) * S5_GROUP,
                     j * 2 * half:(j + 1) * 2 * half] = pc_ref[j, s]
    for j in range(lg):
        for t in range(t_blk):
            blk = qc_ref[j, :, (t // lg) * w:(t // lg + 1) * w]
            shift = ((j - t % lg) % lg) * S5_GROUP
            if shift:
                blk = pltpu.roll(blk, shift, axis=1)
            bigq_ref[j * 2 * half:(j + 1) * 2 * half, t * w:(t + 1) * w] = jnp.where(
                lane_w // S5_GROUP == j, blk, 0.0).astype(BF16)

    u = [u_ref[pl.ds(t, n, stride=t_blk), :] for t in range(t_blk)]
    lhs = jnp.concatenate(u, axis=-1).astype(BF16)
    incr = _dot(lhs, bigp_ref[...])
    row = lax.broadcasted_iota(jnp.int32, (n, 2 * half), 0)
    lane = lax.broadcasted_iota(jnp.int32, (1, 2 * half), 1)
    carried = []
    for j in range(S5_LG):
        h = incr[:, j * 2 * half:(j + 1) * 2 * half]
        a = a_ref[:, j * 2 * half:(j + 1) * 2 * half]
        shift = 1
        while shift < n:
            a_sw = pltpu.roll(a, half, axis=1)
            a1 = jnp.where(lane < half, a, a_sw)
            a2 = jnp.where(lane < half, -a_sw, a)
            x = jnp.where(row >= shift, pltpu.roll(h, shift, axis=0), 0.0)
            h = h + a1 * x + a2 * pltpu.roll(x, half, axis=1)
            a = a1 * a + a2 * a_sw
            shift *= 2
        carried.append(jnp.where(row >= 1, pltpu.roll(h, 1, axis=0), 0.0).astype(BF16))
    y = _dot(lhs, bigv_ref[...]) + _dot(jnp.concatenate(carried, axis=-1), bigq_ref[...])
    d = d_ref[...]
    for t in range(t_blk):
        yt = y[:, t * w:(t + 1) * w] + d * u[t]
        o_ref[pl.ds(t, n, stride=t_blk), :] = jax.nn.gelu(yt, approximate=True)


def _s5(proj, tables):
    s = proj.shape[0]
    t = S5_T
    w = S5_LG * S5_GROUP
    nq = S5_CH // w
    two_p = 2 * S5_STATE
    v_tau, p_c, q_c, a_tile, d_tile = tables

    def per_tile(arr):
        zeros = (0,) * (arr.ndim - 1)
        return pl.BlockSpec((None,) + arr.shape[1:], lambda q: (q,) + zeros)

    return pl.pallas_call(
        _s5_kernel,
        out_shape=jax.ShapeDtypeStruct((s, S5_CH), F32),
        grid=(nq,),
        in_specs=[pl.BlockSpec((s, w), lambda q: (0, OFF_SU // w + q)),
                  per_tile(v_tau), per_tile(p_c), per_tile(q_c), per_tile(a_tile), per_tile(d_tile)],
        out_specs=pl.BlockSpec((s, w), lambda q: (0, q)),
        scratch_shapes=[pltpu.VMEM((t * w, t * w), BF16), pltpu.VMEM((t * w, S5_LG * two_p), BF16),
                        pltpu.VMEM((S5_LG * two_p, t * w), BF16)],
        compiler_params=_cparams(("parallel",)),
        name="s5",
    )(proj, v_tau, p_c, q_c, a_tile, d_tile)


def _rope_kernel(pos_ref, fr_ref, fd_ref, cr_ref, sr_ref, cd_ref, sd_ref):
    pos = pos_ref[...].astype(F32)
    reps = fr_ref.shape[1] // 128
    ang_r = pos * fr_ref[:, :128]
    lane_r = lax.broadcasted_iota(jnp.int32, ang_r.shape, 1)
    cr_ref[...] = jnp.concatenate([jnp.cos(ang_r)] * reps, axis=-1)
    sin_r = jnp.where(lane_r % RET_DK < RET_DK // 2, -1.0, 1.0) * jnp.sin(ang_r)
    sr_ref[...] = jnp.concatenate([sin_r] * reps, axis=-1)
    ang_d = pos * fd_ref[...]
    lane_d = lax.broadcasted_iota(jnp.int32, ang_d.shape, 1)
    cd_ref[...] = jnp.cos(ang_d)
    sd_ref[...] = jnp.where(lane_d % DIFF_DQK < ROPE_DIM // 2, -1.0, 1.0) * jnp.sin(ang_d)


def _rope_tables(positions, *, tb=512):
    s = positions.shape[0]
    tb = min(tb, s)
    half_r = RET_DK // 2
    inv_r = 1.0 / jnp.power(RET_ROPE_BASE, jnp.arange(half_r, dtype=F32) * (2.0 / RET_DK))
    fr = jnp.tile(inv_r, RET_HEADS * RET_DK // half_r).reshape(1, RET_HEADS * RET_DK)
    half_d = ROPE_DIM // 2
    inv_d = 1.0 / jnp.power(ROPE_THETA, jnp.arange(half_d, dtype=F32) * (2.0 / ROPE_DIM))
    fd_head = jnp.concatenate([inv_d, inv_d, jnp.zeros((DIFF_DQK - ROPE_DIM,), F32)])
    fd = jnp.tile(fd_head, 2).reshape(1, 2 * DIFF_DQK)
    wr, wd = fr.shape[1], fd.shape[1]
    return pl.pallas_call(
        _rope_kernel,
        out_shape=[jax.ShapeDtypeStruct((s, wr), F32), jax.ShapeDtypeStruct((s, wr), F32),
                   jax.ShapeDtypeStruct((s, wd), F32), jax.ShapeDtypeStruct((s, wd), F32)],
        grid=(s // tb,),
        in_specs=[pl.BlockSpec((tb, 1), lambda i: (i, 0)),
                  pl.BlockSpec((1, wr), lambda i: (0, 0)),
                  pl.BlockSpec((1, wd), lambda i: (0, 0))],
        out_specs=[pl.BlockSpec((tb, wr), lambda i: (i, 0)), pl.BlockSpec((tb, wr), lambda i: (i, 0)),
                   pl.BlockSpec((tb, wd), lambda i: (i, 0)), pl.BlockSpec((tb, wd), lambda i: (i, 0))],
        compiler_params=_cparams(("parallel",)),
        name="rope_tables",
    )(positions.reshape(s, 1), fr, fd)


def _swap_halves(x, group, half):
    n = x.shape[-1]
    lane = lax.broadcasted_iota(jnp.int32, x.shape, x.ndim - 1)
    return jnp.where(lane % group < half,
                     pltpu.roll(x, n - half, axis=x.ndim - 1),
                     pltpu.roll(x, half, axis=x.ndim - 1))


def _ret_kernel(q_ref, k_ref, v_ref, g_ref, cos_ref, sin_ref, gain_ref, o_ref, r_ref):
    @pl.when(pl.program_id(0) == 0)
    def _():
        r_ref[...] = jnp.zeros_like(r_ref)

    c = q_ref.shape[0]
    cos = cos_ref[...]
    sin = sin_ref[...]
    q = q_ref[...]
    k = k_ref[...]
    q = q * cos + _swap_halves(q, RET_DK, RET_DK // 2) * sin
    k = (k * cos + _swap_halves(k, RET_DK, RET_DK // 2) * sin) * (RET_DK ** -0.5)
    ti = lax.broadcasted_iota(jnp.int32, (c, c), 0)
    si = lax.broadcasted_iota(jnp.int32, (c, c), 1)
    tri = ti >= si
    rel = jnp.where(tri, ti - si, 0).astype(F32)
    idx = lax.broadcasted_iota(jnp.int32, (c, 1), 0).astype(F32)
    gain = gain_ref[...]
    for h in range(RET_HEADS):
        log_gamma = math.log1p(-(2.0 ** (-5.0 - h)))
        qh = q[:, h * RET_DK:(h + 1) * RET_DK]
        kh = k[:, h * RET_DK:(h + 1) * RET_DK]
        vh = v_ref[:, h * RET_DV:(h + 1) * RET_DV]
        d_intra = jnp.where(tri, jnp.exp(rel * log_gamma), 0.0)
        scores = _dot_nt(qh.astype(BF16), kh.astype(BF16)) * d_intra
        r = r_ref[h]
        xi = jnp.exp((idx + 1.0) * log_gamma)
        o = _dot(scores.astype(BF16), vh.astype(BF16)) + _dot(qh.astype(BF16), r.astype(BF16)) * xi
        zeta = jnp.exp((c - 1.0 - idx) * log_gamma)
        r_ref[h] = math.exp(c * log_gamma) * r + _dot_tn((kh * zeta).astype(BF16), vh.astype(BF16))
        sl = slice(h * RET_DV, (h + 1) * RET_DV)
        y = _rms(o) * gain[:, sl] * _silu(g_ref[:, sl])
        o_ref[:, sl] = y.astype(o_ref.dtype)


def _retention(proj, cos_r, sin_r, gain, *, c=RET_C):
    s = proj.shape[0]
    c = min(c, s)
    wq, wv = RET_HEADS * RET_DK, RET_HEADS * RET_DV
    return pl.pallas_call(
        _ret_kernel,
        out_shape=jax.ShapeDtypeStruct((s, wv), BF16),
        grid=(s // c,),
        in_specs=[pl.BlockSpec((c, wq), lambda i: (i, OFF_RQ // wq)),
                  pl.BlockSpec((c, wq), lambda i: (i, OFF_RK // wq)),
                  pl.BlockSpec((c, wv), lambda i: (i, OFF_RV // wv)),
                  pl.BlockSpec((c, wv), lambda i: (i, OFF_RG // wv)),
                  pl.BlockSpec((c, wq), lambda i: (i, 0)),
                  pl.BlockSpec((c, wq), lambda i: (i, 0)),
                  pl.BlockSpec((1, wv), lambda i: (0, 0))],
        out_specs=pl.BlockSpec((c, wv), lambda i: (i, 0)),
        scratch_shapes=[pltpu.VMEM((RET_HEADS, RET_DK, RET_DV), F32)],
        compiler_params=_cparams(("arbitrary",)),
        name="retention",
    )(proj, proj, proj, proj, cos_r, sin_r, gain.reshape(1, -1))


def _diff_prep_kernel(q_ref, k_ref, v_ref, cos_ref, sin_ref, qo_ref, ko_ref, vo_ref):
    reps = q_ref.shape[1] // cos_ref.shape[1]
    cos = jnp.concatenate([cos_ref[...]] * reps, axis=-1)
    sin = jnp.concatenate([sin_ref[...]] * reps, axis=-1)
    q = q_ref[...]
    k = k_ref[...]
    q = (q * cos + _swap_halves(q, DIFF_DQK, ROPE_DIM // 2) * sin) * (DIFF_DQK ** -0.5 * LOG2_E)
    k = k * cos + _swap_halves(k, DIFF_DQK, ROPE_DIM // 2) * sin
    qo_ref[...] = q.T.astype(BF16)
    ko_ref[...] = k.astype(BF16)
    vo_ref[...] = v_ref[...].T.astype(BF16)


def _diff_prep(proj, cos_d, sin_d, *, tb=512):
    s = proj.shape[0]
    tb = min(tb, s)
    w = 2 * DIFF_HEADS * DIFF_DQK
    wt = cos_d.shape[1]
    out_t = jax.ShapeDtypeStruct((w, s), BF16)
    return pl.pallas_call(
        _diff_prep_kernel,
        out_shape=[out_t, jax.ShapeDtypeStruct((s, w), BF16), out_t],
        grid=(s // tb,),
        in_specs=[pl.BlockSpec((tb, w), lambda i: (i, OFF_DQ // w)),
                  pl.BlockSpec((tb, w), lambda i: (i, OFF_DK // w)),
                  pl.BlockSpec((tb, w), lambda i: (i, OFF_DV // w)),
                  pl.BlockSpec((tb, wt), lambda i: (i, 0)),
                  pl.BlockSpec((tb, wt), lambda i: (i, 0))],
        out_specs=[pl.BlockSpec((w, tb), lambda i: (0, i)),
                   pl.BlockSpec((tb, w), lambda i: (i, 0)),
                   pl.BlockSpec((w, tb), lambda i: (0, i))],
        compiler_params=_cparams(("parallel",)),
        name="diff_prep",
    )(proj, proj, proj, cos_d, sin_d)


def _diff_attn_kernel(qt_ref, k_ref, vt_ref, lq1_ref, lk1_ref, lq2_ref, lk2_ref, gain_ref, o_ref,
                      sa_ref, sb_ref, ca_ref, cb_ref, m0_ref, m1_ref, l0_ref, l1_ref, a0_ref, a1_ref,
                      *, lambda_init):
    i = pl.program_id(1)
    tq = qt_ref.shape[1]
    qt = qt_ref[...]
    feat = lax.broadcasted_iota(jnp.int32, qt.shape, 0)
    zero = jnp.zeros_like(qt)
    qts = (jnp.where(feat < DIFF_DQK, qt, zero), jnp.where(feat >= DIFF_DQK, qt, zero))
    m_refs, l_refs, a_refs = (m0_ref, m1_ref), (l0_ref, l1_ref), (a0_ref, a1_ref)
    for mp in range(2):
        m_refs[mp][...] = jnp.full_like(m_refs[mp], MASK_VALUE)
        l_refs[mp][...] = jnp.zeros_like(l_refs[mp])
        a_refs[mp][...] = jnp.zeros_like(a_refs[mp])

    def scores(t, s_ref, c_ref):
        kb = k_ref[pl.ds(pl.multiple_of(t * tq, tq), tq), :]
        for mp in range(2):
            s = _dot(kb, qts[mp])
            s_ref[mp] = s
            c_ref[mp] = jnp.max(s, axis=0, keepdims=True)

    def accumulate(t, s_ref, c_ref, diagonal):
        vtb = vt_ref[:, pl.ds(pl.multiple_of(t * tq, tq), tq)]
        for mp in range(2):
            s = s_ref[mp]
            if diagonal:
                ki = lax.broadcasted_iota(jnp.int32, (tq, tq), 0)
                qi = lax.broadcasted_iota(jnp.int32, (tq, tq), 1)
                s = jnp.where(ki <= qi, s, MASK_VALUE)
                s_max = jnp.max(s, axis=0, keepdims=True)
            else:
                s_max = c_ref[mp]
            m_old = m_refs[mp][...]
            m_new = jnp.maximum(m_old, s_max)
            alpha = jnp.exp2(m_old - m_new)
            p = jnp.exp2(s - m_new)
            l_refs[mp][...] = alpha * l_refs[mp][...] + jnp.sum(p.reshape(tq // 8, 8, tq), axis=0)
            a_refs[mp][...] = alpha * a_refs[mp][...] + _dot(vtb, p.astype(BF16))
            m_refs[mp][...] = m_new

    def pair(jj, carry):
        t = 2 * jj
        scores(t + 1, sb_ref, cb_ref)
        accumulate(t, sa_ref, ca_ref, False)
        scores(t + 2, sa_ref, ca_ref)
        accumulate(t + 1, sb_ref, cb_ref, False)
        return carry

    scores(0, sa_ref, ca_ref)
    lax.fori_loop(0, i // 2, pair, 0)

    @pl.when(i % 2 == 0)
    def _():
        accumulate(i, sa_ref, ca_ref, True)

    @pl.when(i % 2 == 1)
    def _():
        scores(i, sb_ref, cb_ref)
        accumulate(i - 1, sa_ref, ca_ref, False)
        accumulate(i, sb_ref, cb_ref, True)

    lam = (jnp.exp(jnp.sum(lq1_ref[...] * lk1_ref[...], axis=-1, keepdims=True))
           - jnp.exp(jnp.sum(lq2_ref[...] * lk2_ref[...], axis=-1, keepdims=True)) + lambda_init)

    def normalised(mp):
        return a_refs[mp][...] / jnp.sum(l_refs[mp][...], axis=0, keepdims=True)

    o = normalised(0) - lam * normalised(1)
    o = o * lax.rsqrt(jnp.mean(o * o, axis=0, keepdims=True) + RMS_EPS)
    y = o * gain_ref[...] * (1.0 - lambda_init)
    o_ref[...] = y.T.astype(o_ref.dtype)


def _diff_attention(qt, kr, vt, lq1, lk1, lq2, lk2, gain, lambda_init, *, tq=ATT_T):
    s = kr.shape[0]
    tq = min(tq, s)
    wh = 2 * DIFF_DQK
    lam_spec = pl.BlockSpec((1, DIFF_DQK), lambda h, i: (0, 0))
    score = pltpu.VMEM((2, tq, tq), F32)
    stat = pltpu.VMEM((1, tq), F32)
    part = pltpu.VMEM((8, tq), F32)
    acc = pltpu.VMEM((DIFF_DV, tq), F32)
    return pl.pallas_call(
        functools.partial(_diff_attn_kernel, lambda_init=lambda_init),
        out_shape=jax.ShapeDtypeStruct((s, DIFF_HEADS * DIFF_DV), BF16),
        grid=(DIFF_HEADS, s // tq),
        in_specs=[pl.BlockSpec((wh, tq), lambda h, i: (h, i)),
                  pl.BlockSpec((s, wh), lambda h, i: (0, h)),
                  pl.BlockSpec((DIFF_DV, s), lambda h, i: (h, 0)),
                  lam_spec, lam_spec, lam_spec, lam_spec,
                  pl.BlockSpec((DIFF_DV, 1), lambda h, i: (h, 0))],
        out_specs=pl.BlockSpec((tq, DIFF_DV), lambda h, i: (i, h)),
        scratch_shapes=[score, score, pltpu.VMEM((2, 1, tq), F32), pltpu.VMEM((2, 1, tq), F32),
                        stat, stat, part, part, acc, acc],
        compiler_params=_cparams(("parallel", "arbitrary")),
        name="diff_attention",
    )(qt, kr, vt, lq1.reshape(1, -1), lk1.reshape(1, -1), lq2.reshape(1, -1), lk2.reshape(1, -1),
      gain.reshape(-1, 1))


def _merge_kernel(x_ref, yh_ref, ys_ref, yr_ref, yd_ref, gt_ref, wh_ref, ws_ref, wr_ref, wd_ref,
                  wo_ref, g_ref, b_ref, o_ref, mg_ref, *, tn):
    d = D_MODEL
    yh = yh_ref[...]
    ys = ys_ref[...].astype(BF16)
    yr = yr_ref[...]
    yd = yd_ref[...]
    for n in range(d // tn):
        c = slice(n * tn, (n + 1) * tn)
        c2 = slice(d + n * tn, d + (n + 1) * tn)
        up_h = _dot(yh, wh_ref[:, c])
        up_s = _dot(ys, ws_ref[:, c]) * jax.nn.sigmoid(_dot(ys, ws_ref[:, c2]))
        up_r = _dot(yr, wr_ref[:, c])
        up_d = _dot(yd, wd_ref[:, c])

        def gate(b):
            return gt_ref[:, b * d + n * tn:b * d + (n + 1) * tn].astype(F32)

        mg = gate(0) * up_h + gate(1) * up_s + gate(2) * up_r + gate(3) * up_d
        mg_ref[:, c] = mg.astype(BF16)
    y = DN_ALPHA * x_ref[...] + _dot(mg_ref[...], wo_ref[...])
    o_ref[...] = _layer_norm(y, g_ref[...], b_ref[...])


def _merge(x, yh, ys, yr, yd, gates, wh, ws, wr, wd, wo, g, b, l, *, tm=256, tn=512):
    s, d = x.shape
    tm = min(tm, s)
    wy = yh.shape[1]

    def rows(width):
        return pl.BlockSpec((tm, width), lambda i: (i, 0))

    def whole(arr):
        if arr.ndim == 3:
            return pl.BlockSpec((None,) + arr.shape[1:], lambda i: (l, 0, 0),
                                pipeline_mode=pl.Buffered(1))
        return pl.BlockSpec(arr.shape, lambda i: (0, 0), pipeline_mode=pl.Buffered(1))

    g2, b2 = g.reshape(1, d), b.reshape(1, d)
    return pl.pallas_call(
        functools.partial(_merge_kernel, tn=tn),
        out_shape=jax.ShapeDtypeStruct((s, d), F32),
        grid=(s // tm,),
        in_specs=[rows(d), rows(wy), rows(wy), rows(wy), rows(wy), rows(GATE_WIDTH),
                  whole(wh), whole(ws), whole(wr), whole(wd), whole(wo), whole(g2), whole(b2)],
        out_specs=rows(d),
        scratch_shapes=[pltpu.VMEM((tm, d), BF16)],
        compiler_params=_cparams(("parallel",)),
        name="merge_out_ln",
    )(x, yh, ys, yr, yd, gates, wh, ws, wr, wd, wo, g2, b2)


def kernel(x, positions, ffa_w1, ffa_w3, ffa_w2, ln_a_g, ln_a_b, w_in, hg_lb_logits, hg_norm_g, s5_lam_re, s5_lam_im, s5_log_dt, s5_b_re, s5_b_im, s5_c_re, s5_c_im, s5_d, ret_norm_g, diff_lam_q1, diff_lam_k1, diff_lam_q2, diff_lam_k2, diff_norm_g, w_up_hg, w_up_s5, w_up_ret, w_up_diff, w_out, ln_m_g, ln_m_b, ffb_w1, ffb_w3, ffb_w2, ln_b_g, ln_b_b):
    bsz, s, d = x.shape
    assert bsz == 1 and d == D_MODEL
    depth = w_in.shape[0]
    x = x.reshape(s, d)

    p_lb = jax.nn.softmax(hg_lb_logits.astype(F32), axis=0)
    lower_bounds = jnp.maximum(jnp.cumsum(p_lb, axis=0) - p_lb[0], 0.0)
    cos_r, sin_r, cos_d, sin_d = _rope_tables(positions.reshape(s))

    w_up_hg, w_up_s5, w_up_ret, w_up_diff, w_out = [
        w.astype(BF16) for w in (w_up_hg, w_up_s5, w_up_ret, w_up_diff, w_out)]

    for l in range(depth):
        x = _ffn_ln(x, ffa_w1, ffa_w3, ffa_w2, ln_a_g[l], ln_a_b[l], l)
        proj = _in_proj(x, w_in, l, 0, MIX_WIDTH, F32, gate=False)
        gates = _in_proj(x, w_in, l, OFF_GL, GATE_WIDTH, BF16, gate=True)

        y_hg = _hgrn2(proj, lower_bounds[l], hg_norm_g[l])
        tables = _s5_tables(s5_lam_re[l], s5_lam_im[l], s5_log_dt[l], s5_b_re[l], s5_b_im[l],
                            s5_c_re[l], s5_c_im[l], s5_d[l])
        y_s5 = _s5(proj, tables)
        y_r = _retention(proj, cos_r, sin_r, ret_norm_g[l])
        lambda_init = 0.8 - 0.6 * math.exp(-0.3 * l)
        qt, kr, vt = _diff_prep(proj, cos_d, sin_d)
        y_d = _diff_attention(qt, kr, vt, diff_lam_q1[l], diff_lam_k1[l], diff_lam_q2[l],
                              diff_lam_k2[l], diff_norm_g[l], lambda_init)

        x = _merge(x, y_hg, y_s5, y_r, y_d, gates, w_up_hg, w_up_s5, w_up_ret, w_up_diff, w_out,
                   ln_m_g[l], ln_m_b[l], l)
        x = _ffn_ln(x, ffb_w1, ffb_w3, ffb_w2, ln_b_g[l], ln_b_b[l], l)
    return x.reshape(bsz, s, d)
```

```python
import functools
import math

import jax
import jax.numpy as jnp
from jax import lax
from jax.experimental import pallas as pl
from jax.experimental.pallas import tpu as pltpu

F32 = jnp.float32
BF16 = jnp.bfloat16

D_MODEL = 2048
DEPTH = 2
N_BRANCH = 4
HG_HEADS, HG_DK, HG_DV = 4, 128, 128
S5_CH, S5_GROUP, S5_STATE = 512, 16, 64
S5_GROUPS = S5_CH // S5_GROUP
RET_HEADS, RET_DK, RET_DV = 4, 64, 128
RET_ROPE_BASE = 10000.0
DIFF_HEADS, DIFF_DQK, DIFF_DV = 4, 64, 128
ROPE_THETA = 500000.0
ROPE_DIM = DIFF_DQK // 4
D_FF = 5632
LN_EPS = 1e-5
RMS_EPS = 1e-6
MASK_VALUE = -1e30
DN_ALPHA = (2 * DEPTH) ** 0.25

OFF_HQ, OFF_HF, OFF_HI, OFF_HG = 0, 512, 1024, 1536
OFF_SU = 2048
OFF_RQ, OFF_RK, OFF_RV, OFF_RG = 2560, 2816, 3072, 3584
OFF_DQ, OFF_DK, OFF_DV = 4096, 4608, 5120
OFF_GL = 5632
MIX_WIDTH = OFF_GL
GATE_WIDTH = N_BRANCH * D_MODEL

VMEM_LIMIT_BYTES = 56 * 1024 * 1024

S5_T = 16
S5_LG = 8
HG_SUB = 16
RET_C = 256
ATT_T = 512
ATT_ONES_ROWS = 16
LOG2_E = math.log2(math.e)


def _cparams(sem):
    return pltpu.CompilerParams(dimension_semantics=sem, vmem_limit_bytes=VMEM_LIMIT_BYTES)


def _layer_norm(y, g, b):
    mu = jnp.mean(y, axis=-1, keepdims=True)
    d = y - mu
    var = jnp.mean(d * d, axis=-1, keepdims=True)
    return d * lax.rsqrt(var + LN_EPS) * g + b


def _rms(o):
    return o * lax.rsqrt(jnp.mean(o * o, axis=-1, keepdims=True) + RMS_EPS)


def _silu(x):
    return x * jax.nn.sigmoid(x)


def _dot(a, b):
    return jnp.dot(a, b, preferred_element_type=F32)


def _dot_nt(a, b):
    return lax.dot_general(a, b, (((1,), (1,)), ((), ())), preferred_element_type=F32)


def _dot_tn(a, b):
    return lax.dot_general(a, b, (((0,), (0,)), ((), ())), preferred_element_type=F32)


def _ffn_kernel(x_ref, w1_ref, w3_ref, w2_ref, g_ref, b_ref, o_ref, xb_ref):
    f = pl.program_id(1)

    @pl.when(f == 0)
    def _():
        xb_ref[...] = x_ref[...].astype(BF16)
        o_ref[...] = jnp.zeros_like(o_ref)

    xb = xb_ref[...]
    h1 = _dot(xb, w1_ref[...].astype(BF16))
    h3 = _dot(xb, w3_ref[...].astype(BF16))
    h = (_silu(h1) * h3).astype(BF16)
    o_ref[...] += _dot(h, w2_ref[...].astype(BF16))

    @pl.when(f == pl.num_programs(1) - 1)
    def _():
        y = DN_ALPHA * x_ref[...] + 0.5 * o_ref[...]
        o_ref[...] = _layer_norm(y, g_ref[...], b_ref[...])


def _ffn_ln(x, w1, w3, w2, g, b, l, *, tm=1024, tf=256):
    s, d = x.shape
    f = w1.shape[2]
    tm = min(tm, s)
    return pl.pallas_call(
        _ffn_kernel,
        out_shape=jax.ShapeDtypeStruct((s, d), F32),
        grid=(s // tm, f // tf),
        in_specs=[
            pl.BlockSpec((tm, d), lambda i, j: (i, 0), pipeline_mode=pl.Buffered(1)),
            pl.BlockSpec((None, d, tf), lambda i, j: (l, 0, j)),
            pl.BlockSpec((None, d, tf), lambda i, j: (l, 0, j)),
            pl.BlockSpec((None, tf, d), lambda i, j: (l, j, 0)),
            pl.BlockSpec((1, d), lambda i, j: (0, 0)),
            pl.BlockSpec((1, d), lambda i, j: (0, 0)),
        ],
        out_specs=pl.BlockSpec((tm, d), lambda i, j: (i, 0)),
        scratch_shapes=[pltpu.VMEM((tm, d), BF16)],
        compiler_params=_cparams(("parallel", "arbitrary")),
        name="ffn_ln",
    )(x, w1, w3, w2, g.reshape(1, d), b.reshape(1, d))


def _proj_kernel(x_ref, w_ref, o_ref, xb_ref):
    @pl.when(pl.program_id(1) == 0)
    def _():
        xb_ref[...] = x_ref[...].astype(BF16)

    o_ref[...] = _dot(xb_ref[...], w_ref[...].astype(BF16))


def _in_proj_mix(x, w_in, l, *, tm=2048, tn=512):
    s, d = x.shape
    tm = min(tm, s)
    return pl.pallas_call(
        _proj_kernel,
        out_shape=jax.ShapeDtypeStruct((s, MIX_WIDTH), F32),
        grid=(s // tm, MIX_WIDTH // tn),
        in_specs=[
            pl.BlockSpec((tm, d), lambda i, j: (i, 0), pipeline_mode=pl.Buffered(1)),
            pl.BlockSpec((None, d, tn), lambda i, j: (l, 0, j)),
        ],
        out_specs=pl.BlockSpec((tm, tn), lambda i, j: (i, j)),
        scratch_shapes=[pltpu.VMEM((tm, d), BF16)],
        compiler_params=_cparams(("parallel", "arbitrary")),
        name="in_proj_mix",
    )(x, w_in)


def _cumsum_rows(x, row):
    n = x.shape[0]
    shift = 1
    while shift < n:
        x = x + jnp.where(row >= shift, pltpu.roll(x, shift, axis=0), 0.0)
        shift *= 2
    return x


def _hgrn2_sub_chunk_step(q_ref, f_ref, i_ref, g_ref, lb_ref, gain_ref, o_ref, st_ref, sub):
    half = sub // 2
    row = lax.broadcasted_iota(jnp.int32, (sub, HG_DK), 0)
    row_h = lax.broadcasted_iota(jnp.int32, (half, HG_DK), 0)

    def pair_sum(qh, bh, ks, bs, vs, mask_from):
        dec = jnp.exp(bh - bs)
        if mask_from is not None:
            dec = jnp.where(row_h >= mask_from, dec, 0.0)
        return jnp.sum(qh * ks * dec, axis=-1, keepdims=True) * vs

    def head_step(rows, h):
        cs = slice(h * HG_DK, (h + 1) * HG_DK)
        lb = lb_ref[:, cs]
        fl = f_ref[rows, cs]
        q = _silu(q_ref[rows, cs])
        v = i_ref[rows, cs]
        log_f = jax.nn.log_sigmoid(fl) + jnp.log1p(lb * jnp.exp(-fl))
        k = (1.0 - lb) * jax.nn.sigmoid(-fl)
        b = _cumsum_rows(log_f, row)

        st = st_ref[h]
        o = _dot_nt((q * jnp.exp(b)).astype(BF16), st.astype(BF16))
        q_t, q_b, b_t, b_b = q[:half], q[half:], b[:half], b[half:]
        o_t, o_b = o[:half], o[half:]
        for s_ in range(half):
            ks, bs, vs = k[s_:s_ + 1], b[s_:s_ + 1], v[s_:s_ + 1]
            o_t = o_t + pair_sum(q_t, b_t, ks, bs, vs, s_ if s_ > 0 else None)
            o_b = o_b + pair_sum(q_b, b_b, ks, bs, vs, None)
        for s_ in range(half):
            r = half + s_
            ks, bs, vs = k[r:r + 1], b[r:r + 1], v[r:r + 1]
            o_b = o_b + pair_sum(q_b, b_b, ks, bs, vs, s_ if s_ > 0 else None)
        o = jnp.concatenate([o_t, o_b], axis=0)

        b_end = b[sub - 1:sub]
        kd = (k * jnp.exp(b_end - b)).astype(BF16)
        st_ref[h] = st * jnp.exp(b_end) + _dot_tn(v.astype(BF16), kd)

        y = _rms(o) * gain_ref[:, cs] * _silu(g_ref[rows, cs])
        o_ref[rows, cs] = y.astype(o_ref.dtype)

    def step(c):
        rows = pl.ds(pl.multiple_of(c * sub, sub), sub)
        for h in range(HG_HEADS):
            head_step(rows, h)

    return step


def _gate_hgrn2_kernel(x_ref, w_ref, q_ref, f_ref, i_ref, g_ref, lb_ref, gain_ref,
                       gate_ref, y_ref, xb_ref, wb_ref, st_ref, *, sub):
    i, j = pl.program_id(0), pl.program_id(1)

    @pl.when(j == 0)
    def _():
        xb_ref[...] = x_ref[...].astype(BF16)

    @pl.when((i == 0) & (j == 0))
    def _():
        st_ref[...] = jnp.zeros_like(st_ref)

    wb_ref[...] = w_ref[...].astype(BF16)
    n_it = q_ref.shape[0] // sub
    slab = xb_ref.shape[0] // n_it
    hgrn2_step = _hgrn2_sub_chunk_step(q_ref, f_ref, i_ref, g_ref, lb_ref, gain_ref, y_ref, st_ref, sub)

    def gate_rows(start, size):
        rows = pl.ds(pl.multiple_of(start, size), size)
        y = _dot(xb_ref[rows, :], wb_ref[...])
        gate_ref[rows, :] = (0.5 * jnp.tanh(0.5 * y) + 0.5).astype(gate_ref.dtype)

    def body(c, carry):
        gate_rows(c * slab, slab)
        hgrn2_step(c)
        return carry

    lax.fori_loop(0, n_it, body, 0, unroll=True)


def _gate_hgrn2(x, w_in, l, proj, lb, gain, *, tm=2048, tn=512, sub=HG_SUB):
    s, d = x.shape
    tm = min(tm, s)
    n_j = GATE_WIDTH // tn
    th = tm // n_j
    w = HG_HEADS * HG_DK
    cb0 = OFF_GL // tn

    def hcol(off):
        return pl.BlockSpec((th, w), lambda i, j: (i * n_j + j, off // w))

    return pl.pallas_call(
        functools.partial(_gate_hgrn2_kernel, sub=sub),
        out_shape=[jax.ShapeDtypeStruct((s, GATE_WIDTH), BF16),
                   jax.ShapeDtypeStruct((s, HG_HEADS * HG_DV), BF16)],
        grid=(s // tm, n_j),
        in_specs=[pl.BlockSpec((tm, d), lambda i, j: (i, 0), pipeline_mode=pl.Buffered(1)),
                  pl.BlockSpec((None, d, tn), lambda i, j: (l, 0, cb0 + j)),
                  hcol(OFF_HQ), hcol(OFF_HF), hcol(OFF_HI), hcol(OFF_HG),
                  pl.BlockSpec((1, w), lambda i, j: (0, 0)),
                  pl.BlockSpec((1, w), lambda i, j: (0, 0))],
        out_specs=[pl.BlockSpec((tm, tn), lambda i, j: (i, j)),
                   pl.BlockSpec((th, w), lambda i, j: (i * n_j + j, 0))],
        scratch_shapes=[pltpu.VMEM((tm, d), BF16), pltpu.VMEM((d, tn), BF16),
                        pltpu.VMEM((HG_HEADS, HG_DV, HG_DK), F32)],
        compiler_params=_cparams(("arbitrary", "arbitrary")),
        name="gate_hgrn2",
    )(x, w_in, proj, proj, proj, proj, lb.reshape(1, -1), gain.reshape(1, -1))


def _s5_tables(lam_re, lam_im, log_dt, b_re, b_im, c_re, c_im, d_skip):
    hi = lax.Precision.HIGHEST
    g, p, cg, t = S5_GROUPS, S5_STATE, S5_GROUP, S5_T
    dt = jnp.exp(log_dt.astype(F32))[:, None]
    lam_re = lam_re.astype(F32)
    lam_im = lam_im.astype(F32)
    mag = jnp.exp(dt * lam_re)
    ab_re = mag * jnp.cos(dt * lam_im)
    ab_im = mag * jnp.sin(dt * lam_im)
    den = jnp.square(lam_re) + jnp.square(lam_im)
    nr = ab_re - 1.0
    coef_re = (nr * lam_re + ab_im * lam_im) / den
    coef_im = (ab_im * lam_re - nr * lam_im) / den
    b_re = b_re.astype(F32)
    b_im = b_im.astype(F32)
    bb_re = coef_re[..., None] * b_re - coef_im[..., None] * b_im
    bb_im = coef_re[..., None] * b_im + coef_im[..., None] * b_re
    c_re = c_re.astype(F32)
    c_im = c_im.astype(F32)
    tau = jnp.arange(t + 1, dtype=F32)[:, None, None]
    pmag = jnp.exp(tau * (dt * lam_re)[None])
    pw_re = pmag * jnp.cos(tau * (dt * lam_im)[None])
    pw_im = pmag * jnp.sin(tau * (dt * lam_im)[None])
    lb_re = pw_re[:t, :, :, None] * bb_re[None] - pw_im[:t, :, :, None] * bb_im[None]
    lb_im = pw_re[:t, :, :, None] * bb_im[None] + pw_im[:t, :, :, None] * bb_re[None]
    ktau = (jnp.einsum('gcp,tgpd->tgcd', c_re, lb_re, precision=hi)
            - jnp.einsum('gcp,tgpd->tgcd', c_im, lb_im, precision=hi))
    rev = lb_re[::-1], lb_im[::-1]
    p_op = jnp.concatenate([rev[0].transpose(1, 0, 3, 2), rev[1].transpose(1, 0, 3, 2)],
                           axis=-1).reshape(g, t * cg, 2 * p)
    cl_re = c_re[None] * pw_re[1:, :, None, :] - c_im[None] * pw_im[1:, :, None, :]
    cl_im = c_re[None] * pw_im[1:, :, None, :] + c_im[None] * pw_re[1:, :, None, :]
    q_op = jnp.concatenate([cl_re.transpose(1, 3, 0, 2), -cl_im.transpose(1, 3, 0, 2)],
                           axis=1).reshape(g, 2 * p, t * cg)
    a_blk = jnp.concatenate([pw_re[t], pw_im[t]], axis=-1)
    lg, nq = S5_LG, S5_GROUPS // S5_LG
    eye = jnp.eye(lg, dtype=F32)
    k5 = ktau.reshape(t, nq, lg, cg, cg)
    v_tau = jnp.einsum('tqjcd,jk->qtjdkc', k5, eye).reshape(nq, t, lg * cg, lg * cg)
    p_c = p_op.reshape(nq, lg, t, cg, 2 * p)
    q_c = q_op.reshape(nq, lg, 2 * p, t * cg)
    a_tile = a_blk.reshape(nq, 1, lg * 2 * p)
    d_tile = d_skip.astype(F32).reshape(nq, 1, lg * cg)
    return v_tau.astype(BF16), p_c.astype(BF16), q_c, a_tile, d_tile


def _s5_kernel(u_ref, v_ref, pc_ref, qc_ref, a_ref, d_ref, o_ref, bigv_ref, bigp_ref, bigq_ref):
    t_blk, w, lg = S5_T, S5_LG * S5_GROUP, S5_LG
    half = S5_STATE
    n = u_ref.shape[0] // t_blk

    lane_w = lax.broadcasted_iota(jnp.int32, (2 * half, w), 1)
    bigp_ref[...] = jnp.zeros_like(bigp_ref)
    for s in range(t_blk):
        for t in range(t_blk):
            blk = v_ref[t - s] if t >= s else jnp.zeros((w, w), BF16)
            bigv_ref[s * w:(s + 1) * w, t * w:(t + 1) * w] = blk
        for j in range(lg):
            bigp_ref[s * w + j * S5_GROUP:s * w + (j + 1) * S5_GROUP,
                     j * 2 * half:(j + 1) * 2 * half] = pc_ref[j, s]
    for j in range(lg):
        for t in range(t_blk):
            blk = qc_ref[j, :, (t // lg) * w:(t // lg + 1) * w]
            shift = ((j - t % lg) % lg) * S5_GROUP
            if shift:
                blk = pltpu.roll(blk, shift, axis=1)
            bigq_ref[j * 2 * half:(j + 1) * 2 * half, t * w:(t + 1) * w] = jnp.where(
                lane_w // S5_GROUP == j, blk, 0.0).astype(BF16)

    u = [u_ref[pl.ds(t, n, stride=t_blk), :] for t in range(t_blk)]
    lhs = jnp.concatenate(u, axis=-1).astype(BF16)
    incr = _dot(lhs, bigp_ref[...])
    row = lax.broadcasted_iota(jnp.int32, (n, 2 * half), 0)
    lane = lax.broadcasted_iota(jnp.int32, (1, 2 * half), 1)
    carried = []
    for j in range(S5_LG):
        h = incr[:, j * 2 * half:(j + 1) * 2 * half]
        a = a_ref[:, j * 2 * half:(j + 1) * 2 * half]
        shift = 1
        while shift < n:
            a_sw = pltpu.roll(a, half, axis=1)
            a1 = jnp.where(lane < half, a, a_sw)
            a2 = jnp.where(lane < half, -a_sw, a)
            x = jnp.where(row >= shift, pltpu.roll(h, shift, axis=0), 0.0)
            h = h + a1 * x + a2 * pltpu.roll(x, half, axis=1)
            a = a1 * a + a2 * a_sw
            shift *= 2
        carried.append(jnp.where(row >= 1, pltpu.roll(h, 1, axis=0), 0.0).astype(BF16))
    y = _dot(lhs, bigv_ref[...]) + _dot(jnp.concatenate(carried, axis=-1), bigq_ref[...])
    d = d_ref[...]
    for t in range(t_blk):
        yt = y[:, t * w:(t + 1) * w] + d * u[t]
        o_ref[pl.ds(t, n, stride=t_blk), :] = jax.nn.gelu(yt, approximate=True)


def _s5(proj, tables):
    s = proj.shape[0]
    t = S5_T
    w = S5_LG * S5_GROUP
    nq = S5_CH // w
    two_p = 2 * S5_STATE
    v_tau, p_c, q_c, a_tile, d_tile = tables

    def per_tile(arr):
        zeros = (0,) * (arr.ndim - 1)
        return pl.BlockSpec((None,) + arr.shape[1:], lambda q: (q,) + zeros)

    return pl.pallas_call(
        _s5_kernel,
        out_shape=jax.ShapeDtypeStruct((s, S5_CH), F32),
        grid=(nq,),
        in_specs=[pl.BlockSpec((s, w), lambda q: (0, OFF_SU // w + q)),
                  per_tile(v_tau), per_tile(p_c), per_tile(q_c), per_tile(a_tile), per_tile(d_tile)],
        out_specs=pl.BlockSpec((s, w), lambda q: (0, q)),
        scratch_shapes=[pltpu.VMEM((t * w, t * w), BF16), pltpu.VMEM((t * w, S5_LG * two_p), BF16),
                        pltpu.VMEM((S5_LG * two_p, t * w), BF16)],
        compiler_params=_cparams(("parallel",)),
        name="s5",
    )(proj, v_tau, p_c, q_c, a_tile, d_tile)


def _rope_kernel(pos_ref, fr_ref, fd_ref, cr_ref, sr_ref, cd_ref, sd_ref):
    pos = pos_ref[...].astype(F32)
    reps = fr_ref.shape[1] // 128
    ang_r = pos * fr_ref[:, :128]
    lane_r = lax.broadcasted_iota(jnp.int32, ang_r.shape, 1)
    cr_ref[...] = jnp.concatenate([jnp.cos(ang_r)] * reps, axis=-1)
    sin_r = jnp.where(lane_r % RET_DK < RET_DK // 2, -1.0, 1.0) * jnp.sin(ang_r)
    sr_ref[...] = jnp.concatenate([sin_r] * reps, axis=-1)
    ang_d = pos * fd_ref[...]
    lane_d = lax.broadcasted_iota(jnp.int32, ang_d.shape, 1)
    cd_ref[...] = jnp.cos(ang_d)
    sd_ref[...] = jnp.where(lane_d % DIFF_DQK < ROPE_DIM // 2, -1.0, 1.0) * jnp.sin(ang_d)


def _rope_tables(positions, *, tb=512):
    s = positions.shape[0]
    tb = min(tb, s)
    half_r = RET_DK // 2
    inv_r = 1.0 / jnp.power(RET_ROPE_BASE, jnp.arange(half_r, dtype=F32) * (2.0 / RET_DK))
    fr = jnp.tile(inv_r, RET_HEADS * RET_DK // half_r).reshape(1, RET_HEADS * RET_DK)
    half_d = ROPE_DIM // 2
    inv_d = 1.0 / jnp.power(ROPE_THETA, jnp.arange(half_d, dtype=F32) * (2.0 / ROPE_DIM))
    fd_head = jnp.concatenate([inv_d, inv_d, jnp.zeros((DIFF_DQK - ROPE_DIM,), F32)])
    fd = jnp.tile(fd_head, 2).reshape(1, 2 * DIFF_DQK)
    wr, wd = fr.shape[1], fd.shape[1]
    return pl.pallas_call(
        _rope_kernel,
        out_shape=[jax.ShapeDtypeStruct((s, wr), F32), jax.ShapeDtypeStruct((s, wr), F32),
                   jax.ShapeDtypeStruct((s, wd), F32), jax.ShapeDtypeStruct((s, wd), F32)],
        grid=(s // tb,),
        in_specs=[pl.BlockSpec((tb, 1), lambda i: (i, 0)),
                  pl.BlockSpec((1, wr), lambda i: (0, 0)),
                  pl.BlockSpec((1, wd), lambda i: (0, 0))],
        out_specs=[pl.BlockSpec((tb, wr), lambda i: (i, 0)), pl.BlockSpec((tb, wr), lambda i: (i, 0)),
                   pl.BlockSpec((tb, wd), lambda i: (i, 0)), pl.BlockSpec((tb, wd), lambda i: (i, 0))],
        compiler_params=_cparams(("parallel",)),
        name="rope_tables",
    )(positions.reshape(s, 1), fr, fd)


def _swap_halves(x, group, half):
    n = x.shape[-1]
    lane = lax.broadcasted_iota(jnp.int32, x.shape, x.ndim - 1)
    return jnp.where(lane % group < half,
                     pltpu.roll(x, n - half, axis=x.ndim - 1),
                     pltpu.roll(x, half, axis=x.ndim - 1))


def _ret_kernel(q_ref, k_ref, v_ref, g_ref, cos_ref, sin_ref, gain_ref, o_ref, r_ref):
    @pl.when(pl.program_id(0) == 0)
    def _():
        r_ref[...] = jnp.zeros_like(r_ref)

    c = q_ref.shape[0]
    cos = cos_ref[...]
    sin = sin_ref[...]
    q = q_ref[...]
    k = k_ref[...]
    q = q * cos + _swap_halves(q, RET_DK, RET_DK // 2) * sin
    k = (k * cos + _swap_halves(k, RET_DK, RET_DK // 2) * sin) * (RET_DK ** -0.5)
    ti = lax.broadcasted_iota(jnp.int32, (c, c), 0)
    si = lax.broadcasted_iota(jnp.int32, (c, c), 1)
    tri = ti >= si
    rel = jnp.where(tri, ti - si, 0).astype(F32)
    idx = lax.broadcasted_iota(jnp.int32, (c, 1), 0).astype(F32)
    gain = gain_ref[...]
    for h in range(RET_HEADS):
        log_gamma = math.log1p(-(2.0 ** (-5.0 - h)))
        qh = q[:, h * RET_DK:(h + 1) * RET_DK]
        kh = k[:, h * RET_DK:(h + 1) * RET_DK]
        vh = v_ref[:, h * RET_DV:(h + 1) * RET_DV]
        d_intra = jnp.where(tri, jnp.exp(rel * log_gamma), 0.0)
        scores = _dot_nt(qh.astype(BF16), kh.astype(BF16)) * d_intra
        r = r_ref[h]
        xi = jnp.exp((idx + 1.0) * log_gamma)
        o = _dot(scores.astype(BF16), vh.astype(BF16)) + _dot(qh.astype(BF16), r.astype(BF16)) * xi
        zeta = jnp.exp((c - 1.0 - idx) * log_gamma)
        r_ref[h] = math.exp(c * log_gamma) * r + _dot_tn((kh * zeta).astype(BF16), vh.astype(BF16))
        sl = slice(h * RET_DV, (h + 1) * RET_DV)
        y = _rms(o) * gain[:, sl] * _silu(g_ref[:, sl])
        o_ref[:, sl] = y.astype(o_ref.dtype)


def _retention(proj, cos_r, sin_r, gain, *, c=RET_C):
    s = proj.shape[0]
    c = min(c, s)
    wq, wv = RET_HEADS * RET_DK, RET_HEADS * RET_DV
    return pl.pallas_call(
        _ret_kernel,
        out_shape=jax.ShapeDtypeStruct((s, wv), BF16),
        grid=(s // c,),
        in_specs=[pl.BlockSpec((c, wq), lambda i: (i, OFF_RQ // wq)),
                  pl.BlockSpec((c, wq), lambda i: (i, OFF_RK // wq)),
                  pl.BlockSpec((c, wv), lambda i: (i, OFF_RV // wv)),
                  pl.BlockSpec((c, wv), lambda i: (i, OFF_RG // wv)),
                  pl.BlockSpec((c, wq), lambda i: (i, 0)),
                  pl.BlockSpec((c, wq), lambda i: (i, 0)),
                  pl.BlockSpec((1, wv), lambda i: (0, 0))],
        out_specs=pl.BlockSpec((c, wv), lambda i: (i, 0)),
        scratch_shapes=[pltpu.VMEM((RET_HEADS, RET_DK, RET_DV), F32)],
        compiler_params=_cparams(("arbitrary",)),
        name="retention",
    )(proj, proj, proj, proj, cos_r, sin_r, gain.reshape(1, -1))


def _diff_prep_kernel(q_ref, k_ref, v_ref, cos_ref, sin_ref, qo_ref, ko_ref, vo_ref):
    reps = q_ref.shape[1] // cos_ref.shape[1]
    cos = jnp.concatenate([cos_ref[...]] * reps, axis=-1)
    sin = jnp.concatenate([sin_ref[...]] * reps, axis=-1)
    q = q_ref[...]
    k = k_ref[...]
    q = (q * cos + _swap_halves(q, DIFF_DQK, ROPE_DIM // 2) * sin) * (DIFF_DQK ** -0.5 * LOG2_E)
    k = k * cos + _swap_halves(k, DIFF_DQK, ROPE_DIM // 2) * sin
    qo_ref[...] = q.T.astype(BF16)
    ko_ref[...] = k.astype(BF16)
    vo_ref[...] = v_ref[...].T.astype(BF16)


def _diff_prep(proj, cos_d, sin_d, *, tb=512):
    s = proj.shape[0]
    tb = min(tb, s)
    w = 2 * DIFF_HEADS * DIFF_DQK
    wt = cos_d.shape[1]
    out_t = jax.ShapeDtypeStruct((w, s), BF16)
    return pl.pallas_call(
        _diff_prep_kernel,
        out_shape=[out_t, jax.ShapeDtypeStruct((s, w), BF16), out_t],
        grid=(s // tb,),
        in_specs=[pl.BlockSpec((tb, w), lambda i: (i, OFF_DQ // w)),
                  pl.BlockSpec((tb, w), lambda i: (i, OFF_DK // w)),
                  pl.BlockSpec((tb, w), lambda i: (i, OFF_DV // w)),
                  pl.BlockSpec((tb, wt), lambda i: (i, 0)),
                  pl.BlockSpec((tb, wt), lambda i: (i, 0))],
        out_specs=[pl.BlockSpec((w, tb), lambda i: (0, i)),
                   pl.BlockSpec((tb, w), lambda i: (i, 0)),
                   pl.BlockSpec((w, tb), lambda i: (0, i))],
        compiler_params=_cparams(("parallel",)),
        name="diff_prep",
    )(proj, proj, proj, cos_d, sin_d)


def _diff_attn_kernel(qt_ref, k_ref, vt_ref, lq1_ref, lk1_ref, lq2_ref, lk2_ref, gain_ref, o_ref,
                      sa_ref, sb_ref, ca_ref, cb_ref, m0_ref, m1_ref, a0_ref, a1_ref,
                      *, lambda_init):
    i = pl.program_id(1)
    tq = qt_ref.shape[1]
    qt = qt_ref[...]
    feat = lax.broadcasted_iota(jnp.int32, qt.shape, 0)
    zero = jnp.zeros_like(qt)
    qts = (jnp.where(feat < DIFF_DQK, qt, zero), jnp.where(feat >= DIFF_DQK, qt, zero))
    m_refs, a_refs = (m0_ref, m1_ref), (a0_ref, a1_ref)
    ones_rows = jnp.ones((ATT_ONES_ROWS, tq), BF16)
    for mp in range(2):
        m_refs[mp][...] = jnp.full_like(m_refs[mp], MASK_VALUE)
        a_refs[mp][...] = jnp.zeros_like(a_refs[mp])

    def scores(t, s_ref, c_ref):
        kb = k_ref[pl.ds(pl.multiple_of(t * tq, tq), tq), :]
        for mp in range(2):
            s = _dot(kb, qts[mp])
            s_ref[mp] = s
            c_ref[mp] = jnp.max(s, axis=0, keepdims=True)

    def accumulate(t, s_ref, c_ref, diagonal):
        vtb = jnp.concatenate([vt_ref[:, pl.ds(pl.multiple_of(t * tq, tq), tq)], ones_rows],
                              axis=0)
        for mp in range(2):
            s = s_ref[mp]
            if diagonal:
                ki = lax.broadcasted_iota(jnp.int32, (tq, tq), 0)
                qi = lax.broadcasted_iota(jnp.int32, (tq, tq), 1)
                s = jnp.where(ki <= qi, s, MASK_VALUE)
                s_max = jnp.max(s, axis=0, keepdims=True)
            else:
                s_max = c_ref[mp]
            m_old = m_refs[mp][...]
            m_new = jnp.maximum(m_old, s_max)
            alpha = jnp.exp2(m_old - m_new)
            p = jnp.exp2(s - m_new)
            a_refs[mp][...] = alpha * a_refs[mp][...] + _dot(vtb, p.astype(BF16))
            m_refs[mp][...] = m_new

    def pair(jj, carry):
        t = 2 * jj
        scores(t + 1, sb_ref, cb_ref)
        accumulate(t, sa_ref, ca_ref, False)
        scores(t + 2, sa_ref, ca_ref)
        accumulate(t + 1, sb_ref, cb_ref, False)
        return carry

    scores(0, sa_ref, ca_ref)
    lax.fori_loop(0, i // 2, pair, 0)

    @pl.when(i % 2 == 0)
    def _():
        accumulate(i, sa_ref, ca_ref, True)

    @pl.when(i % 2 == 1)
    def _():
        scores(i, sb_ref, cb_ref)
        accumulate(i - 1, sa_ref, ca_ref, False)
        accumulate(i, sb_ref, cb_ref, True)

    lam = (jnp.exp(jnp.sum(lq1_ref[...] * lk1_ref[...], axis=-1, keepdims=True))
           - jnp.exp(jnp.sum(lq2_ref[...] * lk2_ref[...], axis=-1, keepdims=True)) + lambda_init)

    def normalised(mp):
        acc = a_refs[mp][...]
        return acc[:DIFF_DV] / acc[DIFF_DV:DIFF_DV + 1]

    o = normalised(0) - lam * normalised(1)
    o = o * lax.rsqrt(jnp.mean(o * o, axis=0, keepdims=True) + RMS_EPS)
    y = o * gain_ref[...] * (1.0 - lambda_init)
    o_ref[...] = y.T.astype(o_ref.dtype)


def _diff_attention(qt, kr, vt, lq1, lk1, lq2, lk2, gain, lambda_init, *, tq=ATT_T):
    s = kr.shape[0]
    tq = min(tq, s)
    wh = 2 * DIFF_DQK
    lam_spec = pl.BlockSpec((1, DIFF_DQK), lambda h, i: (0, 0))
    score = pltpu.VMEM((2, tq, tq), F32)
    stat = pltpu.VMEM((1, tq), F32)
    acc = pltpu.VMEM((DIFF_DV + ATT_ONES_ROWS, tq), F32)
    return pl.pallas_call(
        functools.partial(_diff_attn_kernel, lambda_init=lambda_init),
        out_shape=jax.ShapeDtypeStruct((s, DIFF_HEADS * DIFF_DV), BF16),
        grid=(DIFF_HEADS, s // tq),
        in_specs=[pl.BlockSpec((wh, tq), lambda h, i: (h, i)),
                  pl.BlockSpec((s, wh), lambda h, i: (0, h)),
                  pl.BlockSpec((DIFF_DV, s), lambda h, i: (h, 0)),
                  lam_spec, lam_spec, lam_spec, lam_spec,
                  pl.BlockSpec((DIFF_DV, 1), lambda h, i: (h, 0))],
        out_specs=pl.BlockSpec((tq, DIFF_DV), lambda h, i: (i, h)),
        scratch_shapes=[score, score, pltpu.VMEM((2, 1, tq), F32), pltpu.VMEM((2, 1, tq), F32),
                        stat, stat, acc, acc],
        compiler_params=_cparams(("parallel", "arbitrary")),
        name="diff_attention",
    )(qt, kr, vt, lq1.reshape(1, -1), lk1.reshape(1, -1), lq2.reshape(1, -1), lk2.reshape(1, -1),
      gain.reshape(-1, 1))


def _merge_kernel(x_ref, yh_ref, ys_ref, yr_ref, yd_ref, gt_ref, wh_ref, ws_ref, wr_ref, wd_ref,
                  wo_ref, g_ref, b_ref, o_ref, mg_ref, *, tn):
    d = D_MODEL
    yh = yh_ref[...]
    ys = ys_ref[...].astype(BF16)
    yr = yr_ref[...]
    yd = yd_ref[...]
    for n in range(d // tn):
        c = slice(n * tn, (n + 1) * tn)
        c2 = slice(d + n * tn, d + (n + 1) * tn)
        up_h = _dot(yh, wh_ref[:, c])
        up_s = _dot(ys, ws_ref[:, c]) * jax.nn.sigmoid(_dot(ys, ws_ref[:, c2]))
        up_r = _dot(yr, wr_ref[:, c])
        up_d = _dot(yd, wd_ref[:, c])

        def gate(b):
            return gt_ref[:, b * d + n * tn:b * d + (n + 1) * tn].astype(F32)

        mg = gate(0) * up_h + gate(1) * up_s + gate(2) * up_r + gate(3) * up_d
        mg_ref[:, c] = mg.astype(BF16)
    y = DN_ALPHA * x_ref[...] + _dot(mg_ref[...], wo_ref[...])
    o_ref[...] = _layer_norm(y, g_ref[...], b_ref[...])


def _merge(x, yh, ys, yr, yd, gates, wh, ws, wr, wd, wo, g, b, l, *, tm=256, tn=512):
    s, d = x.shape
    tm = min(tm, s)
    wy = yh.shape[1]

    def rows(width):
        return pl.BlockSpec((tm, width), lambda i: (i, 0))

    def whole(arr):
        if arr.ndim == 3:
            return pl.BlockSpec((None,) + arr.shape[1:], lambda i: (l, 0, 0),
                                pipeline_mode=pl.Buffered(1))
        return pl.BlockSpec(arr.shape, lambda i: (0, 0), pipeline_mode=pl.Buffered(1))

    g2, b2 = g.reshape(1, d), b.reshape(1, d)
    return pl.pallas_call(
        functools.partial(_merge_kernel, tn=tn),
        out_shape=jax.ShapeDtypeStruct((s, d), F32),
        grid=(s // tm,),
        in_specs=[rows(d), rows(wy), rows(wy), rows(wy), rows(wy), rows(GATE_WIDTH),
                  whole(wh), whole(ws), whole(wr), whole(wd), whole(wo), whole(g2), whole(b2)],
        out_specs=rows(d),
        scratch_shapes=[pltpu.VMEM((tm, d), BF16)],
        compiler_params=_cparams(("parallel",)),
        name="merge_out_ln",
    )(x, yh, ys, yr, yd, gates, wh, ws, wr, wd, wo, g2, b2)


def kernel(x, positions, ffa_w1, ffa_w3, ffa_w2, ln_a_g, ln_a_b, w_in, hg_lb_logits, hg_norm_g, s5_lam_re, s5_lam_im, s5_log_dt, s5_b_re, s5_b_im, s5_c_re, s5_c_im, s5_d, ret_norm_g, diff_lam_q1, diff_lam_k1, diff_lam_q2, diff_lam_k2, diff_norm_g, w_up_hg, w_up_s5, w_up_ret, w_up_diff, w_out, ln_m_g, ln_m_b, ffb_w1, ffb_w3, ffb_w2, ln_b_g, ln_b_b):
    bsz, s, d = x.shape
    assert bsz == 1 and d == D_MODEL
    depth = w_in.shape[0]
    x = x.reshape(s, d)

    p_lb = jax.nn.softmax(hg_lb_logits.astype(F32), axis=0)
    lower_bounds = jnp.maximum(jnp.cumsum(p_lb, axis=0) - p_lb[0], 0.0)
    cos_r, sin_r, cos_d, sin_d = _rope_tables(positions.reshape(s))

    w_up_hg, w_up_s5, w_up_ret, w_up_diff, w_out = [
        w.astype(BF16) for w in (w_up_hg, w_up_s5, w_up_ret, w_up_diff, w_out)]

    for l in range(depth):
        x = _ffn_ln(x, ffa_w1, ffa_w3, ffa_w2, ln_a_g[l], ln_a_b[l], l)
        proj = _in_proj_mix(x, w_in, l)
        gates, y_hg = _gate_hgrn2(x, w_in, l, proj, lower_bounds[l], hg_norm_g[l])
        tables = _s5_tables(s5_lam_re[l], s5_lam_im[l], s5_log_dt[l], s5_b_re[l], s5_b_im[l],
                            s5_c_re[l], s5_c_im[l], s5_d[l])
        y_s5 = _s5(proj, tables)
        y_r = _retention(proj, cos_r, sin_r, ret_norm_g[l])
        lambda_init = 0.8 - 0.6 * math.exp(-0.3 * l)
        qt, kr, vt = _diff_prep(proj, cos_d, sin_d)
        y_d = _diff_attention(qt, kr, vt, diff_lam_q1[l], diff_lam_k1[l], diff_lam_q2[l],
                              diff_lam_k2[l], diff_norm_g[l], lambda_init)

        x = _merge(x, y_hg, y_s5, y_r, y_d, gates, w_up_hg, w_up_s5, w_up_ret, w_up_diff, w_out,
                   ln_m_g[l], ln_m_b[l], l)
        x = _ffn_ln(x, ffb_w1, ffb_w3, ffb_w2, ln_b_g[l], ln_b_b[l], l)
    return x.reshape(bsz, s, d)
```

```python
import functools
import math

import jax
import jax.numpy as jnp
from jax import lax
from jax.experimental import pallas as pl
from jax.experimental.pallas import tpu as pltpu

F32 = jnp.float32
BF16 = jnp.bfloat16

D_MODEL = 2048
DEPTH = 2
N_BRANCH = 4
HG_HEADS, HG_DK, HG_DV = 4, 128, 128
S5_CH, S5_GROUP, S5_STATE = 512, 16, 64
S5_GROUPS = S5_CH // S5_GROUP
RET_HEADS, RET_DK, RET_DV = 4, 64, 128
RET_ROPE_BASE = 10000.0
DIFF_HEADS, DIFF_DQK, DIFF_DV = 4, 64, 128
ROPE_THETA = 500000.0
ROPE_DIM = DIFF_DQK // 4
D_FF = 5632
LN_EPS = 1e-5
RMS_EPS = 1e-6
MASK_VALUE = -1e30
DN_ALPHA = (2 * DEPTH) ** 0.25

OFF_HQ, OFF_HF, OFF_HI, OFF_HG = 0, 512, 1024, 1536
OFF_SU = 2048
OFF_RQ, OFF_RK, OFF_RV, OFF_RG = 2560, 2816, 3072, 3584
OFF_DQ, OFF_DK, OFF_DV = 4096, 4608, 5120
OFF_GL = 5632
MIX_WIDTH = OFF_GL
GATE_WIDTH = N_BRANCH * D_MODEL

VMEM_LIMIT_BYTES = 56 * 1024 * 1024

S5_T = 16
S5_LG = 8
HG_SUB = 16
RET_C = 256
FFN_LAST_SPLIT = 4
ATT_T = 512
ATT_ONES_ROWS = 16
LOG2_E = math.log2(math.e)


def _cparams(sem):
    return pltpu.CompilerParams(dimension_semantics=sem, vmem_limit_bytes=VMEM_LIMIT_BYTES)


def _layer_norm(y, g, b):
    mu = jnp.mean(y, axis=-1, keepdims=True)
    d = y - mu
    var = jnp.mean(d * d, axis=-1, keepdims=True)
    return d * lax.rsqrt(var + LN_EPS) * g + b


def _rms(o):
    return o * lax.rsqrt(jnp.mean(o * o, axis=-1, keepdims=True) + RMS_EPS)


def _silu(x):
    return x * jax.nn.sigmoid(x)


def _dot(a, b):
    return jnp.dot(a, b, preferred_element_type=F32)


def _dot_nt(a, b):
    return lax.dot_general(a, b, (((1,), (1,)), ((), ())), preferred_element_type=F32)


def _dot_tn(a, b):
    return lax.dot_general(a, b, (((0,), (0,)), ((), ())), preferred_element_type=F32)


def _ffn_kernel(x_ref, w1_ref, w3_ref, w2_ref, g_ref, b_ref, o_ref, xb_ref):
    f = pl.program_id(1)

    last = pl.num_programs(1) - 1

    @pl.when(f == 0)
    def _():
        xb_ref[...] = x_ref[...].astype(BF16)
        o_ref[...] = jnp.zeros_like(o_ref)

    def partial_sum(rows, w1, w3, w2):
        xb = xb_ref[rows, :]
        h = (_silu(_dot(xb, w1)) * _dot(xb, w3)).astype(BF16)
        return _dot(h, w2)

    def weights():
        return w1_ref[...].astype(BF16), w3_ref[...].astype(BF16), w2_ref[...].astype(BF16)

    @pl.when(f < last)
    def _():
        o_ref[...] += partial_sum(slice(None), *weights())

    @pl.when(f == last)
    def _():
        w = weights()
        slab = o_ref.shape[0] // FFN_LAST_SPLIT
        for r in range(FFN_LAST_SPLIT):
            rows = slice(r * slab, (r + 1) * slab)
            y = DN_ALPHA * x_ref[rows, :] + 0.5 * (o_ref[rows, :] + partial_sum(rows, *w))
            o_ref[rows, :] = _layer_norm(y, g_ref[...], b_ref[...])


def _ffn_ln(x, w1, w3, w2, g, b, l, *, tm=1024, tf=256):
    s, d = x.shape
    f = w1.shape[2]
    tm = min(tm, s)
    return pl.pallas_call(
        _ffn_kernel,
        out_shape=jax.ShapeDtypeStruct((s, d), F32),
        grid=(s // tm, f // tf),
        in_specs=[
            pl.BlockSpec((tm, d), lambda i, j: (i, 0), pipeline_mode=pl.Buffered(1)),
            pl.BlockSpec((None, d, tf), lambda i, j: (l, 0, j)),
            pl.BlockSpec((None, d, tf), lambda i, j: (l, 0, j)),
            pl.BlockSpec((None, tf, d), lambda i, j: (l, j, 0)),
            pl.BlockSpec((1, d), lambda i, j: (0, 0)),
            pl.BlockSpec((1, d), lambda i, j: (0, 0)),
        ],
        out_specs=pl.BlockSpec((tm, d), lambda i, j: (i, 0)),
        scratch_shapes=[pltpu.VMEM((tm, d), BF16)],
        compiler_params=_cparams(("parallel", "arbitrary")),
        name="ffn_ln",
    )(x, w1, w3, w2, g.reshape(1, d), b.reshape(1, d))


def _proj_kernel(x_ref, w_ref, o_ref, xb_ref):
    @pl.when(pl.program_id(1) == 0)
    def _():
        xb_ref[...] = x_ref[...].astype(BF16)

    o_ref[...] = _dot(xb_ref[...], w_ref[...].astype(BF16))


def _in_proj_mix(x, w_in, l, *, tm=2048, tn=512):
    s, d = x.shape
    tm = min(tm, s)
    return pl.pallas_call(
        _proj_kernel,
        out_shape=[jax.ShapeDtypeStruct((s, MIX_WIDTH), F32), jax.ShapeDtypeStruct((s, d), BF16)],
        grid=(s // tm, MIX_WIDTH // tn),
        in_specs=[
            pl.BlockSpec((tm, d), lambda i, j: (i, 0), pipeline_mode=pl.Buffered(1)),
            pl.BlockSpec((None, d, tn), lambda i, j: (l, 0, j)),
        ],
        out_specs=[pl.BlockSpec((tm, tn), lambda i, j: (i, j)),
                   pl.BlockSpec((tm, d), lambda i, j: (i, 0))],
        compiler_params=_cparams(("parallel", "arbitrary")),
        name="in_proj_mix",
    )(x, w_in)


def _cumsum_rows(x, row):
    n = x.shape[0]
    shift = 1
    while shift < n:
        x = x + jnp.where(row >= shift, pltpu.roll(x, shift, axis=0), 0.0)
        shift *= 2
    return x


def _hgrn2_sub_chunk_step(q_ref, f_ref, i_ref, g_ref, lb_ref, gain_ref, o_ref, st_ref, sub):
    half = sub // 2
    row = lax.broadcasted_iota(jnp.int32, (sub, HG_DK), 0)
    row_h = lax.broadcasted_iota(jnp.int32, (half, HG_DK), 0)

    def pair_sum(qh, bh, ks, bs, vs, mask_from):
        dec = jnp.exp(bh - bs)
        if mask_from is not None:
            dec = jnp.where(row_h >= mask_from, dec, 0.0)
        return jnp.sum(qh * ks * dec, axis=-1, keepdims=True) * vs

    def head_step(rows, h):
        cs = slice(h * HG_DK, (h + 1) * HG_DK)
        lb = lb_ref[:, cs]
        fl = f_ref[rows, cs]
        q = _silu(q_ref[rows, cs])
        v = i_ref[rows, cs]
        log_f = jax.nn.log_sigmoid(fl) + jnp.log1p(lb * jnp.exp(-fl))
        k = (1.0 - lb) * jax.nn.sigmoid(-fl)
        b = _cumsum_rows(log_f, row)

        st = st_ref[h]
        o = _dot_nt((q * jnp.exp(b)).astype(BF16), st.astype(BF16))
        q_t, q_b, b_t, b_b = q[:half], q[half:], b[:half], b[half:]
        o_t, o_b = o[:half], o[half:]
        for s_ in range(half):
            ks, bs, vs = k[s_:s_ + 1], b[s_:s_ + 1], v[s_:s_ + 1]
            o_t = o_t + pair_sum(q_t, b_t, ks, bs, vs, s_ if s_ > 0 else None)
            o_b = o_b + pair_sum(q_b, b_b, ks, bs, vs, None)
        for s_ in range(half):
            r = half + s_
            ks, bs, vs = k[r:r + 1], b[r:r + 1], v[r:r + 1]
            o_b = o_b + pair_sum(q_b, b_b, ks, bs, vs, s_ if s_ > 0 else None)
        o = jnp.concatenate([o_t, o_b], axis=0)

        b_end = b[sub - 1:sub]
        kd = (k * jnp.exp(b_end - b)).astype(BF16)
        st_ref[h] = st * jnp.exp(b_end) + _dot_tn(v.astype(BF16), kd)

        y = _rms(o) * gain_ref[:, cs] * _silu(g_ref[rows, cs])
        o_ref[rows, cs] = y.astype(o_ref.dtype)

    def step(c):
        rows = pl.ds(pl.multiple_of(c * sub, sub), sub)
        for h in range(HG_HEADS):
            head_step(rows, h)

    return step


def _gate_hgrn2_kernel(xb_ref, w_ref, q_ref, f_ref, i_ref, g_ref, lb_ref, gain_ref,
                       gate_ref, y_ref, wb_ref, st_ref, *, sub):
    i, j = pl.program_id(0), pl.program_id(1)

    @pl.when((i == 0) & (j == 0))
    def _():
        st_ref[...] = jnp.zeros_like(st_ref)

    wb_ref[...] = w_ref[...].astype(BF16)
    n_it = q_ref.shape[0] // sub
    slab = xb_ref.shape[0] // n_it
    hgrn2_step = _hgrn2_sub_chunk_step(q_ref, f_ref, i_ref, g_ref, lb_ref, gain_ref, y_ref, st_ref, sub)

    def gate_rows(start, size):
        rows = pl.ds(pl.multiple_of(start, size), size)
        y = _dot(xb_ref[rows, :], wb_ref[...])
        gate_ref[rows, :] = (0.5 * jnp.tanh(0.5 * y) + 0.5).astype(gate_ref.dtype)

    def body(c, carry):
        gate_rows(c * slab, slab)
        hgrn2_step(c)
        return carry

    lax.fori_loop(0, n_it, body, 0, unroll=True)


def _gate_hgrn2(xb, w_in, l, proj, lb, gain, *, tm=2048, tn=512, sub=HG_SUB):
    s, d = xb.shape
    tm = min(tm, s)
    n_j = GATE_WIDTH // tn
    th = tm // n_j
    w = HG_HEADS * HG_DK
    cb0 = OFF_GL // tn

    def hcol(off):
        return pl.BlockSpec((th, w), lambda i, j: (i * n_j + j, off // w))

    return pl.pallas_call(
        functools.partial(_gate_hgrn2_kernel, sub=sub),
        out_shape=[jax.ShapeDtypeStruct((s, GATE_WIDTH), BF16),
                   jax.ShapeDtypeStruct((s, HG_HEADS * HG_DV), BF16)],
        grid=(s // tm, n_j),
        in_specs=[pl.BlockSpec((tm, d), lambda i, j: (i, 0)),
                  pl.BlockSpec((None, d, tn), lambda i, j: (l, 0, cb0 + j)),
                  hcol(OFF_HQ), hcol(OFF_HF), hcol(OFF_HI), hcol(OFF_HG),
                  pl.BlockSpec((1, w), lambda i, j: (0, 0)),
                  pl.BlockSpec((1, w), lambda i, j: (0, 0))],
        out_specs=[pl.BlockSpec((tm, tn), lambda i, j: (i, j)),
                   pl.BlockSpec((th, w), lambda i, j: (i * n_j + j, 0))],
        scratch_shapes=[pltpu.VMEM((d, tn), BF16), pltpu.VMEM((HG_HEADS, HG_DV, HG_DK), F32)],
        compiler_params=_cparams(("arbitrary", "arbitrary")),
        name="gate_hgrn2",
    )(xb, w_in, proj, proj, proj, proj, lb.reshape(1, -1), gain.reshape(1, -1))


def _s5_tables(lam_re, lam_im, log_dt, b_re, b_im, c_re, c_im, d_skip):
    hi = lax.Precision.HIGHEST
    g, p, cg, t = S5_GROUPS, S5_STATE, S5_GROUP, S5_T
    dt = jnp.exp(log_dt.astype(F32))[:, None]
    lam_re = lam_re.astype(F32)
    lam_im = lam_im.astype(F32)
    mag = jnp.exp(dt * lam_re)
    ab_re = mag * jnp.cos(dt * lam_im)
    ab_im = mag * jnp.sin(dt * lam_im)
    den = jnp.square(lam_re) + jnp.square(lam_im)
    nr = ab_re - 1.0
    coef_re = (nr * lam_re + ab_im * lam_im) / den
    coef_im = (ab_im * lam_re - nr * lam_im) / den
    b_re = b_re.astype(F32)
    b_im = b_im.astype(F32)
    bb_re = coef_re[..., None] * b_re - coef_im[..., None] * b_im
    bb_im = coef_re[..., None] * b_im + coef_im[..., None] * b_re
    c_re = c_re.astype(F32)
    c_im = c_im.astype(F32)
    tau = jnp.arange(t + 1, dtype=F32)[:, None, None]
    pmag = jnp.exp(tau * (dt * lam_re)[None])
    pw_re = pmag * jnp.cos(tau * (dt * lam_im)[None])
    pw_im = pmag * jnp.sin(tau * (dt * lam_im)[None])
    lb_re = pw_re[:t, :, :, None] * bb_re[None] - pw_im[:t, :, :, None] * bb_im[None]
    lb_im = pw_re[:t, :, :, None] * bb_im[None] + pw_im[:t, :, :, None] * bb_re[None]
    ktau = (jnp.einsum('gcp,tgpd->tgcd', c_re, lb_re, precision=hi)
            - jnp.einsum('gcp,tgpd->tgcd', c_im, lb_im, precision=hi))
    rev = lb_re[::-1], lb_im[::-1]
    p_op = jnp.concatenate([rev[0].transpose(1, 0, 3, 2), rev[1].transpose(1, 0, 3, 2)],
                           axis=-1).reshape(g, t * cg, 2 * p)
    cl_re = c_re[None] * pw_re[1:, :, None, :] - c_im[None] * pw_im[1:, :, None, :]
    cl_im = c_re[None] * pw_im[1:, :, None, :] + c_im[None] * pw_re[1:, :, None, :]
    q_op = jnp.concatenate([cl_re.transpose(1, 3, 0, 2), -cl_im.transpose(1, 3, 0, 2)],
                           axis=1).reshape(g, 2 * p, t * cg)
    a_blk = jnp.concatenate([pw_re[t], pw_im[t]], axis=-1)
    lg, nq = S5_LG, S5_GROUPS // S5_LG
    eye = jnp.eye(lg, dtype=F32)
    k5 = ktau.reshape(t, nq, lg, cg, cg)
    v_tau = jnp.einsum('tqjcd,jk->qtjdkc', k5, eye).reshape(nq, t, lg * cg, lg * cg)
    p_c = p_op.reshape(nq, lg, t, cg, 2 * p)
    q_c = q_op.reshape(nq, lg, 2 * p, t * cg)
    a_tile = a_blk.reshape(nq, 1, lg * 2 * p)
    d_tile = d_skip.astype(F32).reshape(nq, 1, lg * cg)
    return v_tau.astype(BF16), p_c.astype(BF16), q_c, a_tile, d_tile


def _s5_kernel(u_ref, v_ref, pc_ref, qc_ref, a_ref, d_ref, o_ref, bigv_ref, bigp_ref, bigq_ref):
    t_blk, w, lg = S5_T, S5_LG * S5_GROUP, S5_LG
    half = S5_STATE
    n = u_ref.shape[0] // t_blk

    lane_w = lax.broadcasted_iota(jnp.int32, (2 * half, w), 1)
    bigp_ref[...] = jnp.zeros_like(bigp_ref)
    for s in range(t_blk):
        for t in range(t_blk):
            blk = v_ref[t - s] if t >= s else jnp.zeros((w, w), BF16)
            bigv_ref[s * w:(s + 1) * w, t * w:(t + 1) * w] = blk
        for j in range(lg):
            bigp_ref[s * w + j * S5_GROUP:s * w + (j + 1) * S5_GROUP,
                     j * 2 * half:(j + 1) * 2 * half] = pc_ref[j, s]
    for j in range(lg):
        for t in range(t_blk):
            blk = qc_ref[j, :, (t // lg) * w:(t // lg + 1) * w]
            shift = ((j - t % lg) % lg) * S5_GROUP
            if shift:
                blk = pltpu.roll(blk, shift, axis=1)
            bigq_ref[j * 2 * half:(j + 1) * 2 * half, t * w:(t + 1) * w] = jnp.where(
                lane_w // S5_GROUP == j, blk, 0.0).astype(BF16)

    u = [u_ref[pl.ds(t, n, stride=t_blk), :] for t in range(t_blk)]
    lhs = jnp.concatenate(u, axis=-1).astype(BF16)
    incr = _dot(lhs, bigp_ref[...])
    row = lax.broadcasted_iota(jnp.int32, (n, 2 * half), 0)
    lane = lax.broadcasted_iota(jnp.int32, (1, 2 * half), 1)
    carried = []
    for j in range(S5_LG):
        h = incr[:, j * 2 * half:(j + 1) * 2 * half]
        a = a_ref[:, j * 2 * half:(j + 1) * 2 * half]
        shift = 1
        while shift < n:
            a_sw = pltpu.roll(a, half, axis=1)
            a1 = jnp.where(lane < half, a, a_sw)
            a2 = jnp.where(lane < half, -a_sw, a)
            x = jnp.where(row >= shift, pltpu.roll(h, shift, axis=0), 0.0)
            h = h + a1 * x + a2 * pltpu.roll(x, half, axis=1)
            a = a1 * a + a2 * a_sw
            shift *= 2
        carried.append(jnp.where(row >= 1, pltpu.roll(h, 1, axis=0), 0.0).astype(BF16))
    y = _dot(lhs, bigv_ref[...]) + _dot(jnp.concatenate(carried, axis=-1), bigq_ref[...])
    d = d_ref[...]
    for t in range(t_blk):
        yt = y[:, t * w:(t + 1) * w] + d * u[t]
        o_ref[pl.ds(t, n, stride=t_blk), :] = jax.nn.gelu(yt, approximate=True)


def _s5(proj, tables):
    s = proj.shape[0]
    t = S5_T
    w = S5_LG * S5_GROUP
    nq = S5_CH // w
    two_p = 2 * S5_STATE
    v_tau, p_c, q_c, a_tile, d_tile = tables

    def per_tile(arr):
        zeros = (0,) * (arr.ndim - 1)
        return pl.BlockSpec((None,) + arr.shape[1:], lambda q: (q,) + zeros)

    return pl.pallas_call(
        _s5_kernel,
        out_shape=jax.ShapeDtypeStruct((s, S5_CH), F32),
        grid=(nq,),
        in_specs=[pl.BlockSpec((s, w), lambda q: (0, OFF_SU // w + q)),
                  per_tile(v_tau), per_tile(p_c), per_tile(q_c), per_tile(a_tile), per_tile(d_tile)],
        out_specs=pl.BlockSpec((s, w), lambda q: (0, q)),
        scratch_shapes=[pltpu.VMEM((t * w, t * w), BF16), pltpu.VMEM((t * w, S5_LG * two_p), BF16),
                        pltpu.VMEM((S5_LG * two_p, t * w), BF16)],
        compiler_params=_cparams(("parallel",)),
        name="s5",
    )(proj, v_tau, p_c, q_c, a_tile, d_tile)


def _rope_kernel(pos_ref, fr_ref, fd_ref, cr_ref, sr_ref, cd_ref, sd_ref):
    pos = pos_ref[...].astype(F32)
    reps = fr_ref.shape[1] // 128
    ang_r = pos * fr_ref[:, :128]
    lane_r = lax.broadcasted_iota(jnp.int32, ang_r.shape, 1)
    cr_ref[...] = jnp.concatenate([jnp.cos(ang_r)] * reps, axis=-1)
    sin_r = jnp.where(lane_r % RET_DK < RET_DK // 2, -1.0, 1.0) * jnp.sin(ang_r)
    sr_ref[...] = jnp.concatenate([sin_r] * reps, axis=-1)
    ang_d = pos * fd_ref[...]
    lane_d = lax.broadcasted_iota(jnp.int32, ang_d.shape, 1)
    cd_ref[...] = jnp.cos(ang_d)
    sd_ref[...] = jnp.where(lane_d % DIFF_DQK < ROPE_DIM // 2, -1.0, 1.0) * jnp.sin(ang_d)


def _rope_tables(positions, *, tb=512):
    s = positions.shape[0]
    tb = min(tb, s)
    half_r = RET_DK // 2
    inv_r = 1.0 / jnp.power(RET_ROPE_BASE, jnp.arange(half_r, dtype=F32) * (2.0 / RET_DK))
    fr = jnp.tile(inv_r, RET_HEADS * RET_DK // half_r).reshape(1, RET_HEADS * RET_DK)
    half_d = ROPE_DIM // 2
    inv_d = 1.0 / jnp.power(ROPE_THETA, jnp.arange(half_d, dtype=F32) * (2.0 / ROPE_DIM))
    fd_head = jnp.concatenate([inv_d, inv_d, jnp.zeros((DIFF_DQK - ROPE_DIM,), F32)])
    fd = jnp.tile(fd_head, 2).reshape(1, 2 * DIFF_DQK)
    wr, wd = fr.shape[1], fd.shape[1]
    return pl.pallas_call(
        _rope_kernel,
        out_shape=[jax.ShapeDtypeStruct((s, wr), F32), jax.ShapeDtypeStruct((s, wr), F32),
                   jax.ShapeDtypeStruct((s, wd), F32), jax.ShapeDtypeStruct((s, wd), F32)],
        grid=(s // tb,),
        in_specs=[pl.BlockSpec((tb, 1), lambda i: (i, 0)),
                  pl.BlockSpec((1, wr), lambda i: (0, 0)),
                  pl.BlockSpec((1, wd), lambda i: (0, 0))],
        out_specs=[pl.BlockSpec((tb, wr), lambda i: (i, 0)), pl.BlockSpec((tb, wr), lambda i: (i, 0)),
                   pl.BlockSpec((tb, wd), lambda i: (i, 0)), pl.BlockSpec((tb, wd), lambda i: (i, 0))],
        compiler_params=_cparams(("parallel",)),
        name="rope_tables",
    )(positions.reshape(s, 1), fr, fd)


def _swap_halves(x, group, half):
    n = x.shape[-1]
    lane = lax.broadcasted_iota(jnp.int32, x.shape, x.ndim - 1)
    return jnp.where(lane % group < half,
                     pltpu.roll(x, n - half, axis=x.ndim - 1),
                     pltpu.roll(x, half, axis=x.ndim - 1))


def _ret_kernel(q_ref, k_ref, v_ref, g_ref, cos_ref, sin_ref, gain_ref, o_ref, r_ref):
    @pl.when(pl.program_id(0) == 0)
    def _():
        r_ref[...] = jnp.zeros_like(r_ref)

    c = q_ref.shape[0]
    cos = cos_ref[...]
    sin = sin_ref[...]
    q = q_ref[...]
    k = k_ref[...]
    q = q * cos + _swap_halves(q, RET_DK, RET_DK // 2) * sin
    k = (k * cos + _swap_halves(k, RET_DK, RET_DK // 2) * sin) * (RET_DK ** -0.5)
    ti = lax.broadcasted_iota(jnp.int32, (c, c), 0)
    si = lax.broadcasted_iota(jnp.int32, (c, c), 1)
    tri = ti >= si
    rel = jnp.where(tri, ti - si, 0).astype(F32)
    idx = lax.broadcasted_iota(jnp.int32, (c, 1), 0).astype(F32)
    gain = gain_ref[...]
    for h in range(RET_HEADS):
        log_gamma = math.log1p(-(2.0 ** (-5.0 - h)))
        qh = q[:, h * RET_DK:(h + 1) * RET_DK]
        kh = k[:, h * RET_DK:(h + 1) * RET_DK]
        vh = v_ref[:, h * RET_DV:(h + 1) * RET_DV]
        d_intra = jnp.where(tri, jnp.exp(rel * log_gamma), 0.0)
        scores = _dot_nt(qh.astype(BF16), kh.astype(BF16)) * d_intra
        r = r_ref[h]
        xi = jnp.exp((idx + 1.0) * log_gamma)
        o = _dot(scores.astype(BF16), vh.astype(BF16)) + _dot(qh.astype(BF16), r.astype(BF16)) * xi
        zeta = jnp.exp((c - 1.0 - idx) * log_gamma)
        r_ref[h] = math.exp(c * log_gamma) * r + _dot_tn((kh * zeta).astype(BF16), vh.astype(BF16))
        sl = slice(h * RET_DV, (h + 1) * RET_DV)
        y = _rms(o) * gain[:, sl] * _silu(g_ref[:, sl])
        o_ref[:, sl] = y.astype(o_ref.dtype)


def _retention(proj, cos_r, sin_r, gain, *, c=RET_C):
    s = proj.shape[0]
    c = min(c, s)
    wq, wv = RET_HEADS * RET_DK, RET_HEADS * RET_DV
    return pl.pallas_call(
        _ret_kernel,
        out_shape=jax.ShapeDtypeStruct((s, wv), BF16),
        grid=(s // c,),
        in_specs=[pl.BlockSpec((c, wq), lambda i: (i, OFF_RQ // wq)),
                  pl.BlockSpec((c, wq), lambda i: (i, OFF_RK // wq)),
                  pl.BlockSpec((c, wv), lambda i: (i, OFF_RV // wv)),
                  pl.BlockSpec((c, wv), lambda i: (i, OFF_RG // wv)),
                  pl.BlockSpec((c, wq), lambda i: (i, 0)),
                  pl.BlockSpec((c, wq), lambda i: (i, 0)),
                  pl.BlockSpec((1, wv), lambda i: (0, 0))],
        out_specs=pl.BlockSpec((c, wv), lambda i: (i, 0)),
        scratch_shapes=[pltpu.VMEM((RET_HEADS, RET_DK, RET_DV), F32)],
        compiler_params=_cparams(("arbitrary",)),
        name="retention",
    )(proj, proj, proj, proj, cos_r, sin_r, gain.reshape(1, -1))


def _diff_prep_kernel(q_ref, k_ref, v_ref, cos_ref, sin_ref, qo_ref, ko_ref, vo_ref):
    reps = q_ref.shape[1] // cos_ref.shape[1]
    cos = jnp.concatenate([cos_ref[...]] * reps, axis=-1)
    sin = jnp.concatenate([sin_ref[...]] * reps, axis=-1)
    q = q_ref[...]
    k = k_ref[...]
    q = (q * cos + _swap_halves(q, DIFF_DQK, ROPE_DIM // 2) * sin) * (DIFF_DQK ** -0.5 * LOG2_E)
    k = k * cos + _swap_halves(k, DIFF_DQK, ROPE_DIM // 2) * sin
    qo_ref[...] = q.T.astype(BF16)
    ko_ref[...] = k.astype(BF16)
    vo_ref[...] = v_ref[...].T.astype(BF16)


def _diff_prep(proj, cos_d, sin_d, *, tb=512):
    s = proj.shape[0]
    tb = min(tb, s)
    w = 2 * DIFF_HEADS * DIFF_DQK
    wt = cos_d.shape[1]
    out_t = jax.ShapeDtypeStruct((w, s), BF16)
    return pl.pallas_call(
        _diff_prep_kernel,
        out_shape=[out_t, jax.ShapeDtypeStruct((s, w), BF16), out_t],
        grid=(s // tb,),
        in_specs=[pl.BlockSpec((tb, w), lambda i: (i, OFF_DQ // w)),
                  pl.BlockSpec((tb, w), lambda i: (i, OFF_DK // w)),
                  pl.BlockSpec((tb, w), lambda i: (i, OFF_DV // w)),
                  pl.BlockSpec((tb, wt), lambda i: (i, 0)),
                  pl.BlockSpec((tb, wt), lambda i: (i, 0))],
        out_specs=[pl.BlockSpec((w, tb), lambda i: (0, i)),
                   pl.BlockSpec((tb, w), lambda i: (i, 0)),
                   pl.BlockSpec((w, tb), lambda i: (0, i))],
        compiler_params=_cparams(("parallel",)),
        name="diff_prep",
    )(proj, proj, proj, cos_d, sin_d)


def _diff_attn_kernel(qt_ref, k_ref, vt_ref, lq1_ref, lk1_ref, lq2_ref, lk2_ref, gain_ref, o_ref,
                      sa_ref, sb_ref, ca_ref, cb_ref, m0_ref, m1_ref, a0_ref, a1_ref,
                      *, lambda_init):
    i = pl.program_id(1)
    tq = qt_ref.shape[1]
    qt = qt_ref[...]
    feat = lax.broadcasted_iota(jnp.int32, qt.shape, 0)
    zero = jnp.zeros_like(qt)
    qts = (jnp.where(feat < DIFF_DQK, qt, zero), jnp.where(feat >= DIFF_DQK, qt, zero))
    m_refs, a_refs = (m0_ref, m1_ref), (a0_ref, a1_ref)
    ones_rows = jnp.ones((ATT_ONES_ROWS, tq), BF16)
    for mp in range(2):
        m_refs[mp][...] = jnp.full_like(m_refs[mp], MASK_VALUE)
        a_refs[mp][...] = jnp.zeros_like(a_refs[mp])

    def scores(t, s_ref, c_ref):
        kb = k_ref[pl.ds(pl.multiple_of(t * tq, tq), tq), :]
        for mp in range(2):
            s = _dot(kb, qts[mp])
            s_ref[mp] = s
            c_ref[mp] = jnp.max(s, axis=0, keepdims=True)

    def accumulate(t, s_ref, c_ref, diagonal):
        vtb = jnp.concatenate([vt_ref[:, pl.ds(pl.multiple_of(t * tq, tq), tq)], ones_rows],
                              axis=0)
        for mp in range(2):
            s = s_ref[mp]
            if diagonal:
                ki = lax.broadcasted_iota(jnp.int32, (tq, tq), 0)
                qi = lax.broadcasted_iota(jnp.int32, (tq, tq), 1)
                s = jnp.where(ki <= qi, s, MASK_VALUE)
                s_max = jnp.max(s, axis=0, keepdims=True)
            else:
                s_max = c_ref[mp]
            m_old = m_refs[mp][...]
            m_new = jnp.maximum(m_old, s_max)
            alpha = jnp.exp2(m_old - m_new)
            p = jnp.exp2(s - m_new)
            a_refs[mp][...] = alpha * a_refs[mp][...] + _dot(vtb, p.astype(BF16))
            m_refs[mp][...] = m_new

    def pair(jj, carry):
        t = 2 * jj
        scores(t + 1, sb_ref, cb_ref)
        accumulate(t, sa_ref, ca_ref, False)
        scores(t + 2, sa_ref, ca_ref)
        accumulate(t + 1, sb_ref, cb_ref, False)
        return carry

    scores(0, sa_ref, ca_ref)
    lax.fori_loop(0, i // 2, pair, 0)

    @pl.when(i % 2 == 0)
    def _():
        accumulate(i, sa_ref, ca_ref, True)

    @pl.when(i % 2 == 1)
    def _():
        scores(i, sb_ref, cb_ref)
        accumulate(i - 1, sa_ref, ca_ref, False)
        accumulate(i, sb_ref, cb_ref, True)

    lam = (jnp.exp(jnp.sum(lq1_ref[...] * lk1_ref[...], axis=-1, keepdims=True))
           - jnp.exp(jnp.sum(lq2_ref[...] * lk2_ref[...], axis=-1, keepdims=True)) + lambda_init)

    def normalised(mp):
        acc = a_refs[mp][...]
        return acc[:DIFF_DV] / acc[DIFF_DV:DIFF_DV + 1]

    o = normalised(0) - lam * normalised(1)
    o = o * lax.rsqrt(jnp.mean(o * o, axis=0, keepdims=True) + RMS_EPS)
    y = o * gain_ref[...] * (1.0 - lambda_init)
    o_ref[...] = y.T.astype(o_ref.dtype)


def _diff_attention(qt, kr, vt, lq1, lk1, lq2, lk2, gain, lambda_init, *, tq=ATT_T):
    s = kr.shape[0]
    tq = min(tq, s)
    wh = 2 * DIFF_DQK
    lam_spec = pl.BlockSpec((1, DIFF_DQK), lambda h, i: (0, 0))
    score = pltpu.VMEM((2, tq, tq), F32)
    stat = pltpu.VMEM((1, tq), F32)
    acc = pltpu.VMEM((DIFF_DV + ATT_ONES_ROWS, tq), F32)
    return pl.pallas_call(
        functools.partial(_diff_attn_kernel, lambda_init=lambda_init),
        out_shape=jax.ShapeDtypeStruct((s, DIFF_HEADS * DIFF_DV), BF16),
        grid=(DIFF_HEADS, s // tq),
        in_specs=[pl.BlockSpec((wh, tq), lambda h, i: (h, i)),
                  pl.BlockSpec((s, wh), lambda h, i: (0, h)),
                  pl.BlockSpec((DIFF_DV, s), lambda h, i: (h, 0)),
                  lam_spec, lam_spec, lam_spec, lam_spec,
                  pl.BlockSpec((DIFF_DV, 1), lambda h, i: (h, 0))],
        out_specs=pl.BlockSpec((tq, DIFF_DV), lambda h, i: (i, h)),
        scratch_shapes=[score, score, pltpu.VMEM((2, 1, tq), F32), pltpu.VMEM((2, 1, tq), F32),
                        stat, stat, acc, acc],
        compiler_params=_cparams(("parallel", "arbitrary")),
        name="diff_attention",
    )(qt, kr, vt, lq1.reshape(1, -1), lk1.reshape(1, -1), lq2.reshape(1, -1), lk2.reshape(1, -1),
      gain.reshape(-1, 1))


def _merge_kernel(x_ref, yh_ref, ys_ref, yr_ref, yd_ref, gt_ref, wh_ref, ws_ref, wr_ref, wd_ref,
                  wo_ref, g_ref, b_ref, o_ref, mg_ref, *, tn):
    d = D_MODEL
    yh = yh_ref[...]
    ys = ys_ref[...].astype(BF16)
    yr = yr_ref[...]
    yd = yd_ref[...]
    for n in range(d // tn):
        c = slice(n * tn, (n + 1) * tn)
        c2 = slice(d + n * tn, d + (n + 1) * tn)
        up_h = _dot(yh, wh_ref[:, c])
        up_s = _dot(ys, ws_ref[:, c]) * jax.nn.sigmoid(_dot(ys, ws_ref[:, c2]))
        up_r = _dot(yr, wr_ref[:, c])
        up_d = _dot(yd, wd_ref[:, c])

        def gate(b):
            return gt_ref[:, b * d + n * tn:b * d + (n + 1) * tn].astype(F32)

        mg = gate(0) * up_h + gate(1) * up_s + gate(2) * up_r + gate(3) * up_d
        mg_ref[:, c] = mg.astype(BF16)
    y = DN_ALPHA * x_ref[...] + _dot(mg_ref[...], wo_ref[...])
    o_ref[...] = _layer_norm(y, g_ref[...], b_ref[...])


def _merge(x, yh, ys, yr, yd, gates, wh, ws, wr, wd, wo, g, b, l, *, tm=256, tn=512):
    s, d = x.shape
    tm = min(tm, s)
    wy = yh.shape[1]

    def rows(width):
        return pl.BlockSpec((tm, width), lambda i: (i, 0))

    def whole(arr):
        if arr.ndim == 3:
            return pl.BlockSpec((None,) + arr.shape[1:], lambda i: (l, 0, 0),
                                pipeline_mode=pl.Buffered(1))
        return pl.BlockSpec(arr.shape, lambda i: (0, 0), pipeline_mode=pl.Buffered(1))

    g2, b2 = g.reshape(1, d), b.reshape(1, d)
    return pl.pallas_call(
        functools.partial(_merge_kernel, tn=tn),
        out_shape=jax.ShapeDtypeStruct((s, d), F32),
        grid=(s // tm,),
        in_specs=[rows(d), rows(wy), rows(wy), rows(wy), rows(wy), rows(GATE_WIDTH),
                  whole(wh), whole(ws), whole(wr), whole(wd), whole(wo), whole(g2), whole(b2)],
        out_specs=rows(d),
        scratch_shapes=[pltpu.VMEM((tm, d), BF16)],
        compiler_params=_cparams(("parallel",)),
        name="merge_out_ln",
    )(x, yh, ys, yr, yd, gates, wh, ws, wr, wd, wo, g2, b2)


def kernel(x, positions, ffa_w1, ffa_w3, ffa_w2, ln_a_g, ln_a_b, w_in, hg_lb_logits, hg_norm_g, s5_lam_re, s5_lam_im, s5_log_dt, s5_b_re, s5_b_im, s5_c_re, s5_c_im, s5_d, ret_norm_g, diff_lam_q1, diff_lam_k1, diff_lam_q2, diff_lam_k2, diff_norm_g, w_up_hg, w_up_s5, w_up_ret, w_up_diff, w_out, ln_m_g, ln_m_b, ffb_w1, ffb_w3, ffb_w2, ln_b_g, ln_b_b):
    bsz, s, d = x.shape
    assert bsz == 1 and d == D_MODEL
    depth = w_in.shape[0]
    x = x.reshape(s, d)

    p_lb = jax.nn.softmax(hg_lb_logits.astype(F32), axis=0)
    lower_bounds = jnp.maximum(jnp.cumsum(p_lb, axis=0) - p_lb[0], 0.0)
    cos_r, sin_r, cos_d, sin_d = _rope_tables(positions.reshape(s))

    w_up_hg, w_up_s5, w_up_ret, w_up_diff, w_out = [
        w.astype(BF16) for w in (w_up_hg, w_up_s5, w_up_ret, w_up_diff, w_out)]

    for l in range(depth):
        x = _ffn_ln(x, ffa_w1, ffa_w3, ffa_w2, ln_a_g[l], ln_a_b[l], l)
        proj, xb = _in_proj_mix(x, w_in, l)
        gates, y_hg = _gate_hgrn2(xb, w_in, l, proj, lower_bounds[l], hg_norm_g[l])
        tables = _s5_tables(s5_lam_re[l], s5_lam_im[l], s5_log_dt[l], s5_b_re[l], s5_b_im[l],
                            s5_c_re[l], s5_c_im[l], s5_d[l])
        y_s5 = _s5(proj, tables)
        y_r = _retention(proj, cos_r, sin_r, ret_norm_g[l])
        lambda_init = 0.8 - 0.6 * math.exp(-0.3 * l)
        qt, kr, vt = _diff_prep(proj, cos_d, sin_d)
        y_d = _diff_attention(qt, kr, vt, diff_lam_q1[l], diff_lam_k1[l], diff_lam_q2[l],
                              diff_lam_k2[l], diff_norm_g[l], lambda_init)

        x = _merge(x, y_hg, y_s5, y_r, y_d, gates, w_up_hg, w_up_s5, w_up_ret, w_up_diff, w_out,
                   ln_m_g[l], ln_m_b[l], l)
        x = _ffn_ln(x, ffb_w1, ffb_w3, ffb_w2, ln_b_g[l], ln_b_b[l], l)
    return x.reshape(bsz, s, d)
```

```python
import functools
import math

import jax
import jax.numpy as jnp
from jax import lax
from jax.experimental import pallas as pl
from jax.experimental.pallas import tpu as pltpu

F32 = jnp.float32
BF16 = jnp.bfloat16

D_MODEL = 2048
DEPTH = 2
N_BRANCH = 4
HG_HEADS, HG_DK, HG_DV = 4, 128, 128
S5_CH, S5_GROUP, S5_STATE = 512, 16, 64
S5_GROUPS = S5_CH // S5_GROUP
RET_HEADS, RET_DK, RET_DV = 4, 64, 128
RET_ROPE_BASE = 10000.0
DIFF_HEADS, DIFF_DQK, DIFF_DV = 4, 64, 128
ROPE_THETA = 500000.0
ROPE_DIM = DIFF_DQK // 4
D_FF = 5632
LN_EPS = 1e-5
RMS_EPS = 1e-6
MASK_VALUE = -1e30
DN_ALPHA = (2 * DEPTH) ** 0.25

OFF_HQ, OFF_HF, OFF_HI, OFF_HG = 0, 512, 1024, 1536
OFF_SU = 2048
OFF_RQ, OFF_RK, OFF_RV, OFF_RG = 2560, 2816, 3072, 3584
OFF_DQ, OFF_DK, OFF_DV = 4096, 4608, 5120
OFF_GL = 5632
MIX_WIDTH = OFF_GL
GATE_WIDTH = N_BRANCH * D_MODEL

VMEM_LIMIT_BYTES = 56 * 1024 * 1024

S5_T = 16
S5_LG = 8
HG_SUB = 16
RET_C = 256
FFN_LAST_SPLIT = 4
ATT_T = 512
ATT_ONES_ROWS = 16
LOG2_E = math.log2(math.e)


def _cparams(sem):
    return pltpu.CompilerParams(dimension_semantics=sem, vmem_limit_bytes=VMEM_LIMIT_BYTES)


def _layer_norm(y, g, b):
    mu = jnp.mean(y, axis=-1, keepdims=True)
    d = y - mu
    var = jnp.mean(d * d, axis=-1, keepdims=True)
    return d * lax.rsqrt(var + LN_EPS) * g + b


def _rms(o):
    return o * lax.rsqrt(jnp.mean(o * o, axis=-1, keepdims=True) + RMS_EPS)


def _silu(x):
    return x * jax.nn.sigmoid(x)


def _dot(a, b):
    return jnp.dot(a, b, preferred_element_type=F32)


def _dot_nt(a, b):
    return lax.dot_general(a, b, (((1,), (1,)), ((), ())), preferred_element_type=F32)


def _dot_tn(a, b):
    return lax.dot_general(a, b, (((0,), (0,)), ((), ())), preferred_element_type=F32)


def _ffn_kernel(x_ref, w1_ref, w3_ref, w2_ref, g_ref, b_ref, o_ref, xb_ref, *, emit_bf16):
    f = pl.program_id(1)

    last = pl.num_programs(1) - 1

    @pl.when(f == 0)
    def _():
        xb_ref[...] = x_ref[...].astype(BF16)
        o_ref[...] = jnp.zeros_like(o_ref)

    def partial_sum(rows, w1, w3, w2):
        xb = xb_ref[rows, :]
        h = (_silu(_dot(xb, w1)) * _dot(xb, w3)).astype(BF16)
        return _dot(h, w2)

    def weights():
        return w1_ref[...].astype(BF16), w3_ref[...].astype(BF16), w2_ref[...].astype(BF16)

    @pl.when(f < last)
    def _():
        o_ref[...] += partial_sum(slice(None), *weights())

    @pl.when(f == last)
    def _():
        w = weights()
        slab = o_ref.shape[0] // FFN_LAST_SPLIT
        for r in range(FFN_LAST_SPLIT):
            rows = slice(r * slab, (r + 1) * slab)
            y = DN_ALPHA * x_ref[rows, :] + 0.5 * (o_ref[rows, :] + partial_sum(rows, *w))
            out = _layer_norm(y, g_ref[...], b_ref[...])
            o_ref[rows, :] = out
            if emit_bf16:
                xb_ref[rows, :] = out.astype(BF16)


def _ffn_ln(x, w1, w3, w2, g, b, l, *, emit_bf16=False, tm=1024, tf=256):
    s, d = x.shape
    f = w1.shape[2]
    tm = min(tm, s)
    row_tile = pl.BlockSpec((tm, d), lambda i, j: (i, 0))
    out_f32 = jax.ShapeDtypeStruct((s, d), F32)
    return pl.pallas_call(
        functools.partial(_ffn_kernel, emit_bf16=emit_bf16),
        out_shape=[out_f32, jax.ShapeDtypeStruct((s, d), BF16)] if emit_bf16 else out_f32,
        grid=(s // tm, f // tf),
        in_specs=[
            pl.BlockSpec((tm, d), lambda i, j: (i, 0), pipeline_mode=pl.Buffered(1)),
            pl.BlockSpec((None, d, tf), lambda i, j: (l, 0, j)),
            pl.BlockSpec((None, d, tf), lambda i, j: (l, 0, j)),
            pl.BlockSpec((None, tf, d), lambda i, j: (l, j, 0)),
            pl.BlockSpec((1, d), lambda i, j: (0, 0)),
            pl.BlockSpec((1, d), lambda i, j: (0, 0)),
        ],
        out_specs=[row_tile, row_tile] if emit_bf16 else row_tile,
        scratch_shapes=[] if emit_bf16 else [pltpu.VMEM((tm, d), BF16)],
        compiler_params=_cparams(("parallel", "arbitrary")),
        name="ffn_ln",
    )(x, w1, w3, w2, g.reshape(1, d), b.reshape(1, d))


def _proj_kernel(xb_ref, w_ref, o_ref):
    o_ref[...] = _dot(xb_ref[...], w_ref[...].astype(BF16))


def _in_proj_mix(xb, w_in, l, *, tm=2048, tn=512):
    s, d = xb.shape
    tm = min(tm, s)
    return pl.pallas_call(
        _proj_kernel,
        out_shape=jax.ShapeDtypeStruct((s, MIX_WIDTH), F32),
        grid=(s // tm, MIX_WIDTH // tn),
        in_specs=[pl.BlockSpec((tm, d), lambda i, j: (i, 0)),
                  pl.BlockSpec((None, d, tn), lambda i, j: (l, 0, j))],
        out_specs=pl.BlockSpec((tm, tn), lambda i, j: (i, j)),
        compiler_params=_cparams(("parallel", "arbitrary")),
        name="in_proj_mix",
    )(xb, w_in)


def _cumsum_rows(x, row):
    n = x.shape[0]
    shift = 1
    while shift < n:
        x = x + jnp.where(row >= shift, pltpu.roll(x, shift, axis=0), 0.0)
        shift *= 2
    return x


def _hgrn2_sub_chunk_step(q_ref, f_ref, i_ref, g_ref, lb_ref, gain_ref, o_ref, st_ref, sub):
    half = sub // 2
    row = lax.broadcasted_iota(jnp.int32, (sub, HG_DK), 0)
    row_h = lax.broadcasted_iota(jnp.int32, (half, HG_DK), 0)

    def pair_sum(qh, bh, ks, bs, vs, mask_from):
        dec = jnp.exp(bh - bs)
        if mask_from is not None:
            dec = jnp.where(row_h >= mask_from, dec, 0.0)
        return jnp.sum(qh * ks * dec, axis=-1, keepdims=True) * vs

    def head_step(rows, h):
        cs = slice(h * HG_DK, (h + 1) * HG_DK)
        lb = lb_ref[:, cs]
        fl = f_ref[rows, cs]
        q = _silu(q_ref[rows, cs])
        v = i_ref[rows, cs]
        log_f = jax.nn.log_sigmoid(fl) + jnp.log1p(lb * jnp.exp(-fl))
        k = (1.0 - lb) * jax.nn.sigmoid(-fl)
        b = _cumsum_rows(log_f, row)

        st = st_ref[h]
        o = _dot_nt((q * jnp.exp(b)).astype(BF16), st.astype(BF16))
        q_t, q_b, b_t, b_b = q[:half], q[half:], b[:half], b[half:]
        o_t, o_b = o[:half], o[half:]
        for s_ in range(half):
            ks, bs, vs = k[s_:s_ + 1], b[s_:s_ + 1], v[s_:s_ + 1]
            o_t = o_t + pair_sum(q_t, b_t, ks, bs, vs, s_ if s_ > 0 else None)
            o_b = o_b + pair_sum(q_b, b_b, ks, bs, vs, None)
        for s_ in range(half):
            r = half + s_
            ks, bs, vs = k[r:r + 1], b[r:r + 1], v[r:r + 1]
            o_b = o_b + pair_sum(q_b, b_b, ks, bs, vs, s_ if s_ > 0 else None)
        o = jnp.concatenate([o_t, o_b], axis=0)

        b_end = b[sub - 1:sub]
        kd = (k * jnp.exp(b_end - b)).astype(BF16)
        st_ref[h] = st * jnp.exp(b_end) + _dot_tn(v.astype(BF16), kd)

        y = _rms(o) * gain_ref[:, cs] * _silu(g_ref[rows, cs])
        o_ref[rows, cs] = y.astype(o_ref.dtype)

    def step(c):
        rows = pl.ds(pl.multiple_of(c * sub, sub), sub)
        for h in range(HG_HEADS):
            head_step(rows, h)

    return step


def _gate_hgrn2_kernel(xb_ref, w_ref, q_ref, f_ref, i_ref, g_ref, lb_ref, gain_ref,
                       gate_ref, y_ref, wb_ref, st_ref, *, sub):
    i, j = pl.program_id(0), pl.program_id(1)

    @pl.when((i == 0) & (j == 0))
    def _():
        st_ref[...] = jnp.zeros_like(st_ref)

    wb_ref[...] = w_ref[...].astype(BF16)
    n_it = q_ref.shape[0] // sub
    slab = xb_ref.shape[0] // n_it
    hgrn2_step = _hgrn2_sub_chunk_step(q_ref, f_ref, i_ref, g_ref, lb_ref, gain_ref, y_ref, st_ref, sub)

    def gate_rows(start, size):
        rows = pl.ds(pl.multiple_of(start, size), size)
        y = _dot(xb_ref[rows, :], wb_ref[...])
        gate_ref[rows, :] = (0.5 * jnp.tanh(0.5 * y) + 0.5).astype(gate_ref.dtype)

    def body(c, carry):
        gate_rows(c * slab, slab)
        hgrn2_step(c)
        return carry

    lax.fori_loop(0, n_it, body, 0, unroll=True)


def _gate_hgrn2(xb, w_in, l, proj, lb, gain, *, tm=2048, tn=512, sub=HG_SUB):
    s, d = xb.shape
    tm = min(tm, s)
    n_j = GATE_WIDTH // tn
    th = tm // n_j
    w = HG_HEADS * HG_DK
    cb0 = OFF_GL // tn

    def hcol(off):
        return pl.BlockSpec((th, w), lambda i, j: (i * n_j + j, off // w))

    return pl.pallas_call(
        functools.partial(_gate_hgrn2_kernel, sub=sub),
        out_shape=[jax.ShapeDtypeStruct((s, GATE_WIDTH), BF16),
                   jax.ShapeDtypeStruct((s, HG_HEADS * HG_DV), BF16)],
        grid=(s // tm, n_j),
        in_specs=[pl.BlockSpec((tm, d), lambda i, j: (i, 0)),
                  pl.BlockSpec((None, d, tn), lambda i, j: (l, 0, cb0 + j)),
                  hcol(OFF_HQ), hcol(OFF_HF), hcol(OFF_HI), hcol(OFF_HG),
                  pl.BlockSpec((1, w), lambda i, j: (0, 0)),
                  pl.BlockSpec((1, w), lambda i, j: (0, 0))],
        out_specs=[pl.BlockSpec((tm, tn), lambda i, j: (i, j)),
                   pl.BlockSpec((th, w), lambda i, j: (i * n_j + j, 0))],
        scratch_shapes=[pltpu.VMEM((d, tn), BF16), pltpu.VMEM((HG_HEADS, HG_DV, HG_DK), F32)],
        compiler_params=_cparams(("arbitrary", "arbitrary")),
        name="gate_hgrn2",
    )(xb, w_in, proj, proj, proj, proj, lb.reshape(1, -1), gain.reshape(1, -1))


def _s5_tables(lam_re, lam_im, log_dt, b_re, b_im, c_re, c_im, d_skip):
    hi = lax.Precision.HIGHEST
    g, p, cg, t = S5_GROUPS, S5_STATE, S5_GROUP, S5_T
    dt = jnp.exp(log_dt.astype(F32))[:, None]
    lam_re = lam_re.astype(F32)
    lam_im = lam_im.astype(F32)
    mag = jnp.exp(dt * lam_re)
    ab_re = mag * jnp.cos(dt * lam_im)
    ab_im = mag * jnp.sin(dt * lam_im)
    den = jnp.square(lam_re) + jnp.square(lam_im)
    nr = ab_re - 1.0
    coef_re = (nr * lam_re + ab_im * lam_im) / den
    coef_im = (ab_im * lam_re - nr * lam_im) / den
    b_re = b_re.astype(F32)
    b_im = b_im.astype(F32)
    bb_re = coef_re[..., None] * b_re - coef_im[..., None] * b_im
    bb_im = coef_re[..., None] * b_im + coef_im[..., None] * b_re
    c_re = c_re.astype(F32)
    c_im = c_im.astype(F32)
    tau = jnp.arange(t + 1, dtype=F32)[:, None, None]
    pmag = jnp.exp(tau * (dt * lam_re)[None])
    pw_re = pmag * jnp.cos(tau * (dt * lam_im)[None])
    pw_im = pmag * jnp.sin(tau * (dt * lam_im)[None])
    lb_re = pw_re[:t, :, :, None] * bb_re[None] - pw_im[:t, :, :, None] * bb_im[None]
    lb_im = pw_re[:t, :, :, None] * bb_im[None] + pw_im[:t, :, :, None] * bb_re[None]
    ktau = (jnp.einsum('gcp,tgpd->tgcd', c_re, lb_re, precision=hi)
            - jnp.einsum('gcp,tgpd->tgcd', c_im, lb_im, precision=hi))
    rev = lb_re[::-1], lb_im[::-1]
    p_op = jnp.concatenate([rev[0].transpose(1, 0, 3, 2), rev[1].transpose(1, 0, 3, 2)],
                           axis=-1).reshape(g, t * cg, 2 * p)
    cl_re = c_re[None] * pw_re[1:, :, None, :] - c_im[None] * pw_im[1:, :, None, :]
    cl_im = c_re[None] * pw_im[1:, :, None, :] + c_im[None] * pw_re[1:, :, None, :]
    q_op = jnp.concatenate([cl_re.transpose(1, 3, 0, 2), -cl_im.transpose(1, 3, 0, 2)],
                           axis=1).reshape(g, 2 * p, t * cg)
    a_blk = jnp.concatenate([pw_re[t], pw_im[t]], axis=-1)
    lg, nq = S5_LG, S5_GROUPS // S5_LG
    eye = jnp.eye(lg, dtype=F32)
    k5 = ktau.reshape(t, nq, lg, cg, cg)
    v_tau = jnp.einsum('tqjcd,jk->qtjdkc', k5, eye).reshape(nq, t, lg * cg, lg * cg)
    p_c = p_op.reshape(nq, lg, t, cg, 2 * p)
    q_c = q_op.reshape(nq, lg, 2 * p, t * cg)
    a_tile = a_blk.reshape(nq, 1, lg * 2 * p)
    d_tile = d_skip.astype(F32).reshape(nq, 1, lg * cg)
    return v_tau.astype(BF16), p_c.astype(BF16), q_c, a_tile, d_tile


def _s5_kernel(u_ref, v_ref, pc_ref, qc_ref, a_ref, d_ref, o_ref, bigv_ref, bigp_ref, bigq_ref):
    t_blk, w, lg = S5_T, S5_LG * S5_GROUP, S5_LG
    half = S5_STATE
    n = u_ref.shape[0] // t_blk

    lane_w = lax.broadcasted_iota(jnp.int32, (2 * half, w), 1)
    bigp_ref[...] = jnp.zeros_like(bigp_ref)
    for s in range(t_blk):
        for t in range(t_blk):
            blk = v_ref[t - s] if t >= s else jnp.zeros((w, w), BF16)
            bigv_ref[s * w:(s + 1) * w, t * w:(t + 1) * w] = blk
        for j in range(lg):
            bigp_ref[s * w + j * S5_GROUP:s * w + (j + 1) * S5_GROUP,
                     j * 2 * half:(j + 1) * 2 * half] = pc_ref[j, s]
    for j in range(lg):
        for t in range(t_blk):
            blk = qc_ref[j, :, (t // lg) * w:(t // lg + 1) * w]
            shift = ((j - t % lg) % lg) * S5_GROUP
            if shift:
                blk = pltpu.roll(blk, shift, axis=1)
            bigq_ref[j * 2 * half:(j + 1) * 2 * half, t * w:(t + 1) * w] = jnp.where(
                lane_w // S5_GROUP == j, blk, 0.0).astype(BF16)

    u = [u_ref[pl.ds(t, n, stride=t_blk), :] for t in range(t_blk)]
    lhs = jnp.concatenate(u, axis=-1).astype(BF16)
    incr = _dot(lhs, bigp_ref[...])
    row = lax.broadcasted_iota(jnp.int32, (n, 2 * half), 0)
    lane = lax.broadcasted_iota(jnp.int32, (1, 2 * half), 1)
    carried = []
    for j in range(S5_LG):
        h = incr[:, j * 2 * half:(j + 1) * 2 * half]
        a = a_ref[:, j * 2 * half:(j + 1) * 2 * half]
        shift = 1
        while shift < n:
            a_sw = pltpu.roll(a, half, axis=1)
            a1 = jnp.where(lane < half, a, a_sw)
            a2 = jnp.where(lane < half, -a_sw, a)
            x = jnp.where(row >= shift, pltpu.roll(h, shift, axis=0), 0.0)
            h = h + a1 * x + a2 * pltpu.roll(x, half, axis=1)
            a = a1 * a + a2 * a_sw
            shift *= 2
        carried.append(jnp.where(row >= 1, pltpu.roll(h, 1, axis=0), 0.0).astype(BF16))
    y = _dot(lhs, bigv_ref[...]) + _dot(jnp.concatenate(carried, axis=-1), bigq_ref[...])
    d = d_ref[...]
    for t in range(t_blk):
        yt = y[:, t * w:(t + 1) * w] + d * u[t]
        o_ref[pl.ds(t, n, stride=t_blk), :] = jax.nn.gelu(yt, approximate=True)


def _s5(proj, tables):
    s = proj.shape[0]
    t = S5_T
    w = S5_LG * S5_GROUP
    nq = S5_CH // w
    two_p = 2 * S5_STATE
    v_tau, p_c, q_c, a_tile, d_tile = tables

    def per_tile(arr):
        zeros = (0,) * (arr.ndim - 1)
        return pl.BlockSpec((None,) + arr.shape[1:], lambda q: (q,) + zeros)

    return pl.pallas_call(
        _s5_kernel,
        out_shape=jax.ShapeDtypeStruct((s, S5_CH), F32),
        grid=(nq,),
        in_specs=[pl.BlockSpec((s, w), lambda q: (0, OFF_SU // w + q)),
                  per_tile(v_tau), per_tile(p_c), per_tile(q_c), per_tile(a_tile), per_tile(d_tile)],
        out_specs=pl.BlockSpec((s, w), lambda q: (0, q)),
        scratch_shapes=[pltpu.VMEM((t * w, t * w), BF16), pltpu.VMEM((t * w, S5_LG * two_p), BF16),
                        pltpu.VMEM((S5_LG * two_p, t * w), BF16)],
        compiler_params=_cparams(("parallel",)),
        name="s5",
    )(proj, v_tau, p_c, q_c, a_tile, d_tile)


def _rope_kernel(pos_ref, fr_ref, fd_ref, cr_ref, sr_ref, cd_ref, sd_ref):
    pos = pos_ref[...].astype(F32)
    reps = fr_ref.shape[1] // 128
    ang_r = pos * fr_ref[:, :128]
    lane_r = lax.broadcasted_iota(jnp.int32, ang_r.shape, 1)
    cr_ref[...] = jnp.concatenate([jnp.cos(ang_r)] * reps, axis=-1)
    sin_r = jnp.where(lane_r % RET_DK < RET_DK // 2, -1.0, 1.0) * jnp.sin(ang_r)
    sr_ref[...] = jnp.concatenate([sin_r] * reps, axis=-1)
    ang_d = pos * fd_ref[...]
    lane_d = lax.broadcasted_iota(jnp.int32, ang_d.shape, 1)
    cd_ref[...] = jnp.cos(ang_d)
    sd_ref[...] = jnp.where(lane_d % DIFF_DQK < ROPE_DIM // 2, -1.0, 1.0) * jnp.sin(ang_d)


def _rope_tables(positions, *, tb=512):
    s = positions.shape[0]
    tb = min(tb, s)
    half_r = RET_DK // 2
    inv_r = 1.0 / jnp.power(RET_ROPE_BASE, jnp.arange(half_r, dtype=F32) * (2.0 / RET_DK))
    fr = jnp.tile(inv_r, RET_HEADS * RET_DK // half_r).reshape(1, RET_HEADS * RET_DK)
    half_d = ROPE_DIM // 2
    inv_d = 1.0 / jnp.power(ROPE_THETA, jnp.arange(half_d, dtype=F32) * (2.0 / ROPE_DIM))
    fd_head = jnp.concatenate([inv_d, inv_d, jnp.zeros((DIFF_DQK - ROPE_DIM,), F32)])
    fd = jnp.tile(fd_head, 2).reshape(1, 2 * DIFF_DQK)
    wr, wd = fr.shape[1], fd.shape[1]
    return pl.pallas_call(
        _rope_kernel,
        out_shape=[jax.ShapeDtypeStruct((s, wr), F32), jax.ShapeDtypeStruct((s, wr), F32),
                   jax.ShapeDtypeStruct((s, wd), F32), jax.ShapeDtypeStruct((s, wd), F32)],
        grid=(s // tb,),
        in_specs=[pl.BlockSpec((tb, 1), lambda i: (i, 0)),
                  pl.BlockSpec((1, wr), lambda i: (0, 0)),
                  pl.BlockSpec((1, wd), lambda i: (0, 0))],
        out_specs=[pl.BlockSpec((tb, wr), lambda i: (i, 0)), pl.BlockSpec((tb, wr), lambda i: (i, 0)),
                   pl.BlockSpec((tb, wd), lambda i: (i, 0)), pl.BlockSpec((tb, wd), lambda i: (i, 0))],
        compiler_params=_cparams(("parallel",)),
        name="rope_tables",
    )(positions.reshape(s, 1), fr, fd)


def _swap_halves(x, group, half):
    n = x.shape[-1]
    lane = lax.broadcasted_iota(jnp.int32, x.shape, x.ndim - 1)
    return jnp.where(lane % group < half,
                     pltpu.roll(x, n - half, axis=x.ndim - 1),
                     pltpu.roll(x, half, axis=x.ndim - 1))


def _ret_kernel(q_ref, k_ref, v_ref, g_ref, cos_ref, sin_ref, gain_ref, o_ref, r_ref):
    @pl.when(pl.program_id(0) == 0)
    def _():
        r_ref[...] = jnp.zeros_like(r_ref)

    c = q_ref.shape[0]
    cos = cos_ref[...]
    sin = sin_ref[...]
    q = q_ref[...]
    k = k_ref[...]
    q = q * cos + _swap_halves(q, RET_DK, RET_DK // 2) * sin
    k = (k * cos + _swap_halves(k, RET_DK, RET_DK // 2) * sin) * (RET_DK ** -0.5)
    ti = lax.broadcasted_iota(jnp.int32, (c, c), 0)
    si = lax.broadcasted_iota(jnp.int32, (c, c), 1)
    tri = ti >= si
    rel = jnp.where(tri, ti - si, 0).astype(F32)
    idx = lax.broadcasted_iota(jnp.int32, (c, 1), 0).astype(F32)
    gain = gain_ref[...]
    for h in range(RET_HEADS):
        log_gamma = math.log1p(-(2.0 ** (-5.0 - h)))
        qh = q[:, h * RET_DK:(h + 1) * RET_DK]
        kh = k[:, h * RET_DK:(h + 1) * RET_DK]
        vh = v_ref[:, h * RET_DV:(h + 1) * RET_DV]
        d_intra = jnp.where(tri, jnp.exp(rel * log_gamma), 0.0)
        scores = _dot_nt(qh.astype(BF16), kh.astype(BF16)) * d_intra
        r = r_ref[h]
        xi = jnp.exp((idx + 1.0) * log_gamma)
        o = _dot(scores.astype(BF16), vh.astype(BF16)) + _dot(qh.astype(BF16), r.astype(BF16)) * xi
        zeta = jnp.exp((c - 1.0 - idx) * log_gamma)
        r_ref[h] = math.exp(c * log_gamma) * r + _dot_tn((kh * zeta).astype(BF16), vh.astype(BF16))
        sl = slice(h * RET_DV, (h + 1) * RET_DV)
        y = _rms(o) * gain[:, sl] * _silu(g_ref[:, sl])
        o_ref[:, sl] = y.astype(o_ref.dtype)


def _retention(proj, cos_r, sin_r, gain, *, c=RET_C):
    s = proj.shape[0]
    c = min(c, s)
    wq, wv = RET_HEADS * RET_DK, RET_HEADS * RET_DV
    return pl.pallas_call(
        _ret_kernel,
        out_shape=jax.ShapeDtypeStruct((s, wv), BF16),
        grid=(s // c,),
        in_specs=[pl.BlockSpec((c, wq), lambda i: (i, OFF_RQ // wq)),
                  pl.BlockSpec((c, wq), lambda i: (i, OFF_RK // wq)),
                  pl.BlockSpec((c, wv), lambda i: (i, OFF_RV // wv)),
                  pl.BlockSpec((c, wv), lambda i: (i, OFF_RG // wv)),
                  pl.BlockSpec((c, wq), lambda i: (i, 0)),
                  pl.BlockSpec((c, wq), lambda i: (i, 0)),
                  pl.BlockSpec((1, wv), lambda i: (0, 0))],
        out_specs=pl.BlockSpec((c, wv), lambda i: (i, 0)),
        scratch_shapes=[pltpu.VMEM((RET_HEADS, RET_DK, RET_DV), F32)],
        compiler_params=_cparams(("arbitrary",)),
        name="retention",
    )(proj, proj, proj, proj, cos_r, sin_r, gain.reshape(1, -1))


def _diff_prep_kernel(q_ref, k_ref, v_ref, cos_ref, sin_ref, qo_ref, ko_ref, vo_ref):
    reps = q_ref.shape[1] // cos_ref.shape[1]
    cos = jnp.concatenate([cos_ref[...]] * reps, axis=-1)
    sin = jnp.concatenate([sin_ref[...]] * reps, axis=-1)
    q = q_ref[...]
    k = k_ref[...]
    q = (q * cos + _swap_halves(q, DIFF_DQK, ROPE_DIM // 2) * sin) * (DIFF_DQK ** -0.5 * LOG2_E)
    k = k * cos + _swap_halves(k, DIFF_DQK, ROPE_DIM // 2) * sin
    qo_ref[...] = q.T.astype(BF16)
    ko_ref[...] = k.astype(BF16)
    vo_ref[...] = v_ref[...].T.astype(BF16)


def _diff_prep(proj, cos_d, sin_d, *, tb=512):
    s = proj.shape[0]
    tb = min(tb, s)
    w = 2 * DIFF_HEADS * DIFF_DQK
    wt = cos_d.shape[1]
    out_t = jax.ShapeDtypeStruct((w, s), BF16)
    return pl.pallas_call(
        _diff_prep_kernel,
        out_shape=[out_t, jax.ShapeDtypeStruct((s, w), BF16), out_t],
        grid=(s // tb,),
        in_specs=[pl.BlockSpec((tb, w), lambda i: (i, OFF_DQ // w)),
                  pl.BlockSpec((tb, w), lambda i: (i, OFF_DK // w)),
                  pl.BlockSpec((tb, w), lambda i: (i, OFF_DV // w)),
                  pl.BlockSpec((tb, wt), lambda i: (i, 0)),
                  pl.BlockSpec((tb, wt), lambda i: (i, 0))],
        out_specs=[pl.BlockSpec((w, tb), lambda i: (0, i)),
                   pl.BlockSpec((tb, w), lambda i: (i, 0)),
                   pl.BlockSpec((w, tb), lambda i: (0, i))],
        compiler_params=_cparams(("parallel",)),
        name="diff_prep",
    )(proj, proj, proj, cos_d, sin_d)


def _diff_attn_kernel(qt_ref, k_ref, vt_ref, lq1_ref, lk1_ref, lq2_ref, lk2_ref, gain_ref, o_ref,
                      sa_ref, sb_ref, ca_ref, cb_ref, m0_ref, m1_ref, a0_ref, a1_ref,
                      *, lambda_init):
    i = pl.program_id(1)
    tq = qt_ref.shape[1]
    qt = qt_ref[...]
    feat = lax.broadcasted_iota(jnp.int32, qt.shape, 0)
    zero = jnp.zeros_like(qt)
    qts = (jnp.where(feat < DIFF_DQK, qt, zero), jnp.where(feat >= DIFF_DQK, qt, zero))
    m_refs, a_refs = (m0_ref, m1_ref), (a0_ref, a1_ref)
    ones_rows = jnp.ones((ATT_ONES_ROWS, tq), BF16)
    for mp in range(2):
        m_refs[mp][...] = jnp.full_like(m_refs[mp], MASK_VALUE)
        a_refs[mp][...] = jnp.zeros_like(a_refs[mp])

    def scores(t, s_ref, c_ref):
        kb = k_ref[pl.ds(pl.multiple_of(t * tq, tq), tq), :]
        for mp in range(2):
            s = _dot(kb, qts[mp])
            s_ref[mp] = s
            c_ref[mp] = jnp.max(s, axis=0, keepdims=True)

    def accumulate(t, s_ref, c_ref, diagonal):
        vtb = jnp.concatenate([vt_ref[:, pl.ds(pl.multiple_of(t * tq, tq), tq)], ones_rows],
                              axis=0)
        for mp in range(2):
            s = s_ref[mp]
            if diagonal:
                ki = lax.broadcasted_iota(jnp.int32, (tq, tq), 0)
                qi = lax.broadcasted_iota(jnp.int32, (tq, tq), 1)
                s = jnp.where(ki <= qi, s, MASK_VALUE)
                s_max = jnp.max(s, axis=0, keepdims=True)
            else:
                s_max = c_ref[mp]
            m_old = m_refs[mp][...]
            m_new = jnp.maximum(m_old, s_max)
            alpha = jnp.exp2(m_old - m_new)
            p = jnp.exp2(s - m_new)
            a_refs[mp][...] = alpha * a_refs[mp][...] + _dot(vtb, p.astype(BF16))
            m_refs[mp][...] = m_new

    def pair(jj, carry):
        t = 2 * jj
        scores(t + 1, sb_ref, cb_ref)
        accumulate(t, sa_ref, ca_ref, False)
        scores(t + 2, sa_ref, ca_ref)
        accumulate(t + 1, sb_ref, cb_ref, False)
        return carry

    scores(0, sa_ref, ca_ref)
    lax.fori_loop(0, i // 2, pair, 0)

    @pl.when(i % 2 == 0)
    def _():
        accumulate(i, sa_ref, ca_ref, True)

    @pl.when(i % 2 == 1)
    def _():
        scores(i, sb_ref, cb_ref)
        accumulate(i - 1, sa_ref, ca_ref, False)
        accumulate(i, sb_ref, cb_ref, True)

    lam = (jnp.exp(jnp.sum(lq1_ref[...] * lk1_ref[...], axis=-1, keepdims=True))
           - jnp.exp(jnp.sum(lq2_ref[...] * lk2_ref[...], axis=-1, keepdims=True)) + lambda_init)

    def normalised(mp):
        acc = a_refs[mp][...]
        return acc[:DIFF_DV] / acc[DIFF_DV:DIFF_DV + 1]

    o = normalised(0) - lam * normalised(1)
    o = o * lax.rsqrt(jnp.mean(o * o, axis=0, keepdims=True) + RMS_EPS)
    y = o * gain_ref[...] * (1.0 - lambda_init)
    o_ref[...] = y.T.astype(o_ref.dtype)


def _diff_attention(qt, kr, vt, lq1, lk1, lq2, lk2, gain, lambda_init, *, tq=ATT_T):
    s = kr.shape[0]
    tq = min(tq, s)
    wh = 2 * DIFF_DQK
    lam_spec = pl.BlockSpec((1, DIFF_DQK), lambda h, i: (0, 0))
    score = pltpu.VMEM((2, tq, tq), F32)
    stat = pltpu.VMEM((1, tq), F32)
    acc = pltpu.VMEM((DIFF_DV + ATT_ONES_ROWS, tq), F32)
    return pl.pallas_call(
        functools.partial(_diff_attn_kernel, lambda_init=lambda_init),
        out_shape=jax.ShapeDtypeStruct((s, DIFF_HEADS * DIFF_DV), BF16),
        grid=(DIFF_HEADS, s // tq),
        in_specs=[pl.BlockSpec((wh, tq), lambda h, i: (h, i)),
                  pl.BlockSpec((s, wh), lambda h, i: (0, h)),
                  pl.BlockSpec((DIFF_DV, s), lambda h, i: (h, 0)),
                  lam_spec, lam_spec, lam_spec, lam_spec,
                  pl.BlockSpec((DIFF_DV, 1), lambda h, i: (h, 0))],
        out_specs=pl.BlockSpec((tq, DIFF_DV), lambda h, i: (i, h)),
        scratch_shapes=[score, score, pltpu.VMEM((2, 1, tq), F32), pltpu.VMEM((2, 1, tq), F32),
                        stat, stat, acc, acc],
        compiler_params=_cparams(("parallel", "arbitrary")),
        name="diff_attention",
    )(qt, kr, vt, lq1.reshape(1, -1), lk1.reshape(1, -1), lq2.reshape(1, -1), lk2.reshape(1, -1),
      gain.reshape(-1, 1))


def _merge_kernel(x_ref, yh_ref, ys_ref, yr_ref, yd_ref, gt_ref, wh_ref, ws_ref, wr_ref, wd_ref,
                  wo_ref, g_ref, b_ref, o_ref, mg_ref, *, tn):
    d = D_MODEL
    yh = yh_ref[...]
    ys = ys_ref[...].astype(BF16)
    yr = yr_ref[...]
    yd = yd_ref[...]
    for n in range(d // tn):
        c = slice(n * tn, (n + 1) * tn)
        c2 = slice(d + n * tn, d + (n + 1) * tn)
        up_h = _dot(yh, wh_ref[:, c])
        up_s = _dot(ys, ws_ref[:, c]) * jax.nn.sigmoid(_dot(ys, ws_ref[:, c2]))
        up_r = _dot(yr, wr_ref[:, c])
        up_d = _dot(yd, wd_ref[:, c])

        def gate(b):
            return gt_ref[:, b * d + n * tn:b * d + (n + 1) * tn].astype(F32)

        mg = gate(0) * up_h + gate(1) * up_s + gate(2) * up_r + gate(3) * up_d
        mg_ref[:, c] = mg.astype(BF16)
    y = DN_ALPHA * x_ref[...] + _dot(mg_ref[...], wo_ref[...])
    o_ref[...] = _layer_norm(y, g_ref[...], b_ref[...])


def _merge(x, yh, ys, yr, yd, gates, wh, ws, wr, wd, wo, g, b, l, *, tm=256, tn=512):
    s, d = x.shape
    tm = min(tm, s)
    wy = yh.shape[1]

    def rows(width):
        return pl.BlockSpec((tm, width), lambda i: (i, 0))

    def whole(arr):
        if arr.ndim == 3:
            return pl.BlockSpec((None,) + arr.shape[1:], lambda i: (l, 0, 0),
                                pipeline_mode=pl.Buffered(1))
        return pl.BlockSpec(arr.shape, lambda i: (0, 0), pipeline_mode=pl.Buffered(1))

    g2, b2 = g.reshape(1, d), b.reshape(1, d)
    return pl.pallas_call(
        functools.partial(_merge_kernel, tn=tn),
        out_shape=jax.ShapeDtypeStruct((s, d), F32),
        grid=(s // tm,),
        in_specs=[rows(d), rows(wy), rows(wy), rows(wy), rows(wy), rows(GATE_WIDTH),
                  whole(wh), whole(ws), whole(wr), whole(wd), whole(wo), whole(g2), whole(b2)],
        out_specs=rows(d),
        scratch_shapes=[pltpu.VMEM((tm, d), BF16)],
        compiler_params=_cparams(("parallel",)),
        name="merge_out_ln",
    )(x, yh, ys, yr, yd, gates, wh, ws, wr, wd, wo, g2, b2)


def kernel(x, positions, ffa_w1, ffa_w3, ffa_w2, ln_a_g, ln_a_b, w_in, hg_lb_logits, hg_norm_g, s5_lam_re, s5_lam_im, s5_log_dt, s5_b_re, s5_b_im, s5_c_re, s5_c_im, s5_d, ret_norm_g, diff_lam_q1, diff_lam_k1, diff_lam_q2, diff_lam_k2, diff_norm_g, w_up_hg, w_up_s5, w_up_ret, w_up_diff, w_out, ln_m_g, ln_m_b, ffb_w1, ffb_w3, ffb_w2, ln_b_g, ln_b_b):
    bsz, s, d = x.shape
    assert bsz == 1 and d == D_MODEL
    depth = w_in.shape[0]
    x = x.reshape(s, d)

    p_lb = jax.nn.softmax(hg_lb_logits.astype(F32), axis=0)
    lower_bounds = jnp.maximum(jnp.cumsum(p_lb, axis=0) - p_lb[0], 0.0)
    cos_r, sin_r, cos_d, sin_d = _rope_tables(positions.reshape(s))

    w_up_hg, w_up_s5, w_up_ret, w_up_diff, w_out = [
        w.astype(BF16) for w in (w_up_hg, w_up_s5, w_up_ret, w_up_diff, w_out)]

    for l in range(depth):
        x, xb = _ffn_ln(x, ffa_w1, ffa_w3, ffa_w2, ln_a_g[l], ln_a_b[l], l, emit_bf16=True)
        proj = _in_proj_mix(xb, w_in, l)
        gates, y_hg = _gate_hgrn2(xb, w_in, l, proj, lower_bounds[l], hg_norm_g[l])
        tables = _s5_tables(s5_lam_re[l], s5_lam_im[l], s5_log_dt[l], s5_b_re[l], s5_b_im[l],
                            s5_c_re[l], s5_c_im[l], s5_d[l])
        y_s5 = _s5(proj, tables)
        y_r = _retention(proj, cos_r, sin_r, ret_norm_g[l])
        lambda_init = 0.8 - 0.6 * math.exp(-0.3 * l)
        qt, kr, vt = _diff_prep(proj, cos_d, sin_d)
        y_d = _diff_attention(qt, kr, vt, diff_lam_q1[l], diff_lam_k1[l], diff_lam_q2[l],
                              diff_lam_k2[l], diff_norm_g[l], lambda_init)

        x = _merge(x, y_hg, y_s5, y_r, y_d, gates, w_up_hg, w_up_s5, w_up_ret, w_up_diff, w_out,
                   ln_m_g[l], ln_m_b[l], l)
        x = _ffn_ln(x, ffb_w1, ffb_w3, ffb_w2, ln_b_g[l], ln_b_b[l], l)
    return x.reshape(bsz, s, d)
```

```python
import functools
import math

import jax
import jax.numpy as jnp
from jax import lax
from jax.experimental import pallas as pl
from jax.experimental.pallas import tpu as pltpu

F32 = jnp.float32
BF16 = jnp.bfloat16

D_MODEL = 2048
DEPTH = 2
N_BRANCH = 4
HG_HEADS, HG_DK, HG_DV = 4, 128, 128
S5_CH, S5_GROUP, S5_STATE = 512, 16, 64
S5_GROUPS = S5_CH // S5_GROUP
RET_HEADS, RET_DK, RET_DV = 4, 64, 128
RET_ROPE_BASE = 10000.0
DIFF_HEADS, DIFF_DQK, DIFF_DV = 4, 64, 128
ROPE_THETA = 500000.0
ROPE_DIM = DIFF_DQK // 4
D_FF = 5632
LN_EPS = 1e-5
RMS_EPS = 1e-6
MASK_VALUE = -1e30
DN_ALPHA = (2 * DEPTH) ** 0.25

OFF_HQ, OFF_HF, OFF_HI, OFF_HG = 0, 512, 1024, 1536
OFF_SU = 2048
OFF_RQ, OFF_RK, OFF_RV, OFF_RG = 2560, 2816, 3072, 3584
OFF_DQ, OFF_DK, OFF_DV = 4096, 4608, 5120
OFF_GL = 5632
MIX_WIDTH = OFF_GL
GATE_WIDTH = N_BRANCH * D_MODEL

VMEM_LIMIT_BYTES = 58 * 1024 * 1024

S5_T = 16
S5_LG = 8
HG_SUB = 16
RET_C = 256
FFN_LAST_SPLIT = 4
ATT_T = 512
ATT_ONES_ROWS = 16
LOG2_E = math.log2(math.e)


def _cparams(sem):
    return pltpu.CompilerParams(dimension_semantics=sem, vmem_limit_bytes=VMEM_LIMIT_BYTES)


def _layer_norm(y, g, b):
    mu = jnp.mean(y, axis=-1, keepdims=True)
    d = y - mu
    var = jnp.mean(d * d, axis=-1, keepdims=True)
    return d * lax.rsqrt(var + LN_EPS) * g + b


def _rms(o):
    return o * lax.rsqrt(jnp.mean(o * o, axis=-1, keepdims=True) + RMS_EPS)


def _silu(x):
    return x * jax.nn.sigmoid(x)


def _dot(a, b):
    return jnp.dot(a, b, preferred_element_type=F32)


def _dot_nt(a, b):
    return lax.dot_general(a, b, (((1,), (1,)), ((), ())), preferred_element_type=F32)


def _dot_tn(a, b):
    return lax.dot_general(a, b, (((0,), (0,)), ((), ())), preferred_element_type=F32)


def _ffn_kernel(x_ref, w1_ref, w3_ref, w2_ref, g_ref, b_ref, o_ref, xb_ref, *, emit_bf16):
    f = pl.program_id(1)

    last = pl.num_programs(1) - 1

    @pl.when(f == 0)
    def _():
        xb_ref[...] = x_ref[...].astype(BF16)
        o_ref[...] = jnp.zeros_like(o_ref)

    def partial_sum(rows, w1, w3, w2):
        xb = xb_ref[rows, :]
        h = (_silu(_dot(xb, w1)) * _dot(xb, w3)).astype(BF16)
        return _dot(h, w2)

    def weights():
        return w1_ref[...].astype(BF16), w3_ref[...].astype(BF16), w2_ref[...].astype(BF16)

    @pl.when(f < last)
    def _():
        o_ref[...] += partial_sum(slice(None), *weights())

    @pl.when(f == last)
    def _():
        w = weights()
        slab = o_ref.shape[0] // FFN_LAST_SPLIT
        for r in range(FFN_LAST_SPLIT):
            rows = slice(r * slab, (r + 1) * slab)
            y = DN_ALPHA * x_ref[rows, :] + 0.5 * (o_ref[rows, :] + partial_sum(rows, *w))
            out = _layer_norm(y, g_ref[...], b_ref[...])
            o_ref[rows, :] = out
            if emit_bf16:
                xb_ref[rows, :] = out.astype(BF16)


def _ffn_ln(x, w1, w3, w2, g, b, l, *, emit_bf16=False, tm=1024, tf=256):
    s, d = x.shape
    f = w1.shape[2]
    tm = min(tm, s)
    row_tile = pl.BlockSpec((tm, d), lambda i, j: (i, 0))
    out_f32 = jax.ShapeDtypeStruct((s, d), F32)
    return pl.pallas_call(
        functools.partial(_ffn_kernel, emit_bf16=emit_bf16),
        out_shape=[out_f32, jax.ShapeDtypeStruct((s, d), BF16)] if emit_bf16 else out_f32,
        grid=(s // tm, f // tf),
        in_specs=[
            pl.BlockSpec((tm, d), lambda i, j: (i, 0),
                         pipeline_mode=pl.Buffered(1 if emit_bf16 else 2)),
            pl.BlockSpec((None, d, tf), lambda i, j: (l, 0, j)),
            pl.BlockSpec((None, d, tf), lambda i, j: (l, 0, j)),
            pl.BlockSpec((None, tf, d), lambda i, j: (l, j, 0)),
            pl.BlockSpec((1, d), lambda i, j: (0, 0)),
            pl.BlockSpec((1, d), lambda i, j: (0, 0)),
        ],
        out_specs=[row_tile, row_tile] if emit_bf16 else row_tile,
        scratch_shapes=[] if emit_bf16 else [pltpu.VMEM((tm, d), BF16)],
        compiler_params=_cparams(("parallel", "arbitrary")),
        name="ffn_ln",
    )(x, w1, w3, w2, g.reshape(1, d), b.reshape(1, d))


def _proj_kernel(xb_ref, w_ref, o_ref):
    o_ref[...] = _dot(xb_ref[...], w_ref[...].astype(BF16))


def _in_proj_mix(xb, w_in, l, *, tm=2048, tn=512):
    s, d = xb.shape
    tm = min(tm, s)
    return pl.pallas_call(
        _proj_kernel,
        out_shape=jax.ShapeDtypeStruct((s, MIX_WIDTH), F32),
        grid=(s // tm, MIX_WIDTH // tn),
        in_specs=[pl.BlockSpec((tm, d), lambda i, j: (i, 0)),
                  pl.BlockSpec((None, d, tn), lambda i, j: (l, 0, j))],
        out_specs=pl.BlockSpec((tm, tn), lambda i, j: (i, j)),
        compiler_params=_cparams(("parallel", "arbitrary")),
        name="in_proj_mix",
    )(xb, w_in)


def _cumsum_rows(x, row):
    n = x.shape[0]
    shift = 1
    while shift < n:
        x = x + jnp.where(row >= shift, pltpu.roll(x, shift, axis=0), 0.0)
        shift *= 2
    return x


def _hgrn2_sub_chunk_step(q_ref, f_ref, i_ref, g_ref, lb_ref, gain_ref, o_ref, st_ref, sub):
    half = sub // 2
    row = lax.broadcasted_iota(jnp.int32, (sub, HG_DK), 0)
    row_h = lax.broadcasted_iota(jnp.int32, (half, HG_DK), 0)

    def pair_sum(qh, bh, ks, bs, vs, mask_from):
        dec = jnp.exp(bh - bs)
        if mask_from is not None:
            dec = jnp.where(row_h >= mask_from, dec, 0.0)
        return jnp.sum(qh * ks * dec, axis=-1, keepdims=True) * vs

    def head_step(rows, h):
        cs = slice(h * HG_DK, (h + 1) * HG_DK)
        lb = lb_ref[:, cs]
        fl = f_ref[rows, cs]
        q = _silu(q_ref[rows, cs])
        v = i_ref[rows, cs]
        log_f = jax.nn.log_sigmoid(fl) + jnp.log1p(lb * jnp.exp(-fl))
        k = (1.0 - lb) * jax.nn.sigmoid(-fl)
        b = _cumsum_rows(log_f, row)

        st = st_ref[h]
        o = _dot_nt((q * jnp.exp(b)).astype(BF16), st.astype(BF16))
        q_t, q_b, b_t, b_b = q[:half], q[half:], b[:half], b[half:]
        o_t, o_b = o[:half], o[half:]
        for s_ in range(half):
            ks, bs, vs = k[s_:s_ + 1], b[s_:s_ + 1], v[s_:s_ + 1]
            o_t = o_t + pair_sum(q_t, b_t, ks, bs, vs, s_ if s_ > 0 else None)
            o_b = o_b + pair_sum(q_b, b_b, ks, bs, vs, None)
        for s_ in range(half):
            r = half + s_
            ks, bs, vs = k[r:r + 1], b[r:r + 1], v[r:r + 1]
            o_b = o_b + pair_sum(q_b, b_b, ks, bs, vs, s_ if s_ > 0 else None)
        o = jnp.concatenate([o_t, o_b], axis=0)

        b_end = b[sub - 1:sub]
        kd = (k * jnp.exp(b_end - b)).astype(BF16)
        st_ref[h] = st * jnp.exp(b_end) + _dot_tn(v.astype(BF16), kd)

        y = _rms(o) * gain_ref[:, cs] * _silu(g_ref[rows, cs])
        o_ref[rows, cs] = y.astype(o_ref.dtype)

    def step(c):
        rows = pl.ds(pl.multiple_of(c * sub, sub), sub)
        for h in range(HG_HEADS):
            head_step(rows, h)

    return step


def _gate_hgrn2_kernel(xb_ref, w_ref, q_ref, f_ref, i_ref, g_ref, lb_ref, gain_ref,
                       gate_ref, y_ref, wb_ref, st_ref, *, sub):
    i, j = pl.program_id(0), pl.program_id(1)

    @pl.when((i == 0) & (j == 0))
    def _():
        st_ref[...] = jnp.zeros_like(st_ref)

    wb_ref[...] = w_ref[...].astype(BF16)
    n_it = q_ref.shape[0] // sub
    slab = xb_ref.shape[0] // n_it
    hgrn2_step = _hgrn2_sub_chunk_step(q_ref, f_ref, i_ref, g_ref, lb_ref, gain_ref, y_ref, st_ref, sub)

    def gate_rows(start, size):
        rows = pl.ds(pl.multiple_of(start, size), size)
        y = _dot(xb_ref[rows, :], wb_ref[...])
        gate_ref[rows, :] = (0.5 * jnp.tanh(0.5 * y) + 0.5).astype(gate_ref.dtype)

    def body(c, carry):
        gate_rows(c * slab, slab)
        hgrn2_step(c)
        return carry

    lax.fori_loop(0, n_it, body, 0, unroll=True)


def _gate_hgrn2(xb, w_in, l, proj, lb, gain, *, tm=2048, tn=512, sub=HG_SUB):
    s, d = xb.shape
    tm = min(tm, s)
    n_j = GATE_WIDTH // tn
    th = tm // n_j
    w = HG_HEADS * HG_DK
    cb0 = OFF_GL // tn

    def hcol(off):
        return pl.BlockSpec((th, w), lambda i, j: (i * n_j + j, off // w))

    return pl.pallas_call(
        functools.partial(_gate_hgrn2_kernel, sub=sub),
        out_shape=[jax.ShapeDtypeStruct((s, GATE_WIDTH), BF16),
                   jax.ShapeDtypeStruct((s, HG_HEADS * HG_DV), BF16)],
        grid=(s // tm, n_j),
        in_specs=[pl.BlockSpec((tm, d), lambda i, j: (i, 0)),
                  pl.BlockSpec((None, d, tn), lambda i, j: (l, 0, cb0 + j)),
                  hcol(OFF_HQ), hcol(OFF_HF), hcol(OFF_HI), hcol(OFF_HG),
                  pl.BlockSpec((1, w), lambda i, j: (0, 0)),
                  pl.BlockSpec((1, w), lambda i, j: (0, 0))],
        out_specs=[pl.BlockSpec((tm, tn), lambda i, j: (i, j)),
                   pl.BlockSpec((th, w), lambda i, j: (i * n_j + j, 0))],
        scratch_shapes=[pltpu.VMEM((d, tn), BF16), pltpu.VMEM((HG_HEADS, HG_DV, HG_DK), F32)],
        compiler_params=_cparams(("arbitrary", "arbitrary")),
        name="gate_hgrn2",
    )(xb, w_in, proj, proj, proj, proj, lb.reshape(1, -1), gain.reshape(1, -1))


def _s5_tables(lam_re, lam_im, log_dt, b_re, b_im, c_re, c_im, d_skip):
    hi = lax.Precision.HIGHEST
    g, p, cg, t = S5_GROUPS, S5_STATE, S5_GROUP, S5_T
    dt = jnp.exp(log_dt.astype(F32))[:, None]
    lam_re = lam_re.astype(F32)
    lam_im = lam_im.astype(F32)
    mag = jnp.exp(dt * lam_re)
    ab_re = mag * jnp.cos(dt * lam_im)
    ab_im = mag * jnp.sin(dt * lam_im)
    den = jnp.square(lam_re) + jnp.square(lam_im)
    nr = ab_re - 1.0
    coef_re = (nr * lam_re + ab_im * lam_im) / den
    coef_im = (ab_im * lam_re - nr * lam_im) / den
    b_re = b_re.astype(F32)
    b_im = b_im.astype(F32)
    bb_re = coef_re[..., None] * b_re - coef_im[..., None] * b_im
    bb_im = coef_re[..., None] * b_im + coef_im[..., None] * b_re
    c_re = c_re.astype(F32)
    c_im = c_im.astype(F32)
    tau = jnp.arange(t + 1, dtype=F32)[:, None, None]
    pmag = jnp.exp(tau * (dt * lam_re)[None])
    pw_re = pmag * jnp.cos(tau * (dt * lam_im)[None])
    pw_im = pmag * jnp.sin(tau * (dt * lam_im)[None])
    lb_re = pw_re[:t, :, :, None] * bb_re[None] - pw_im[:t, :, :, None] * bb_im[None]
    lb_im = pw_re[:t, :, :, None] * bb_im[None] + pw_im[:t, :, :, None] * bb_re[None]
    ktau = (jnp.einsum('gcp,tgpd->tgcd', c_re, lb_re, precision=hi)
            - jnp.einsum('gcp,tgpd->tgcd', c_im, lb_im, precision=hi))
    rev = lb_re[::-1], lb_im[::-1]
    p_op = jnp.concatenate([rev[0].transpose(1, 0, 3, 2), rev[1].transpose(1, 0, 3, 2)],
                           axis=-1).reshape(g, t * cg, 2 * p)
    cl_re = c_re[None] * pw_re[1:, :, None, :] - c_im[None] * pw_im[1:, :, None, :]
    cl_im = c_re[None] * pw_im[1:, :, None, :] + c_im[None] * pw_re[1:, :, None, :]
    q_op = jnp.concatenate([cl_re.transpose(1, 3, 0, 2), -cl_im.transpose(1, 3, 0, 2)],
                           axis=1).reshape(g, 2 * p, t * cg)
    a_blk = jnp.concatenate([pw_re[t], pw_im[t]], axis=-1)
    lg, nq = S5_LG, S5_GROUPS // S5_LG
    eye = jnp.eye(lg, dtype=F32)
    k5 = ktau.reshape(t, nq, lg, cg, cg)
    v_tau = jnp.einsum('tqjcd,jk->qtjdkc', k5, eye).reshape(nq, t, lg * cg, lg * cg)
    p_c = p_op.reshape(nq, lg, t, cg, 2 * p)
    q_c = q_op.reshape(nq, lg, 2 * p, t * cg)
    a_tile = a_blk.reshape(nq, 1, lg * 2 * p)
    d_tile = d_skip.astype(F32).reshape(nq, 1, lg * cg)
    return v_tau.astype(BF16), p_c.astype(BF16), q_c, a_tile, d_tile


def _s5_kernel(u_ref, v_ref, pc_ref, qc_ref, a_ref, d_ref, o_ref, bigv_ref, bigp_ref, bigq_ref):
    t_blk, w, lg = S5_T, S5_LG * S5_GROUP, S5_LG
    half = S5_STATE
    n = u_ref.shape[0] // t_blk

    lane_w = lax.broadcasted_iota(jnp.int32, (2 * half, w), 1)
    bigp_ref[...] = jnp.zeros_like(bigp_ref)
    for s in range(t_blk):
        for t in range(t_blk):
            blk = v_ref[t - s] if t >= s else jnp.zeros((w, w), BF16)
            bigv_ref[s * w:(s + 1) * w, t * w:(t + 1) * w] = blk
        for j in range(lg):
            bigp_ref[s * w + j * S5_GROUP:s * w + (j + 1) * S5_GROUP,
                     j * 2 * half:(j + 1) * 2 * half] = pc_ref[j, s]
    for j in range(lg):
        for t in range(t_blk):
            blk = qc_ref[j, :, (t // lg) * w:(t // lg + 1) * w]
            shift = ((j - t % lg) % lg) * S5_GROUP
            if shift:
                blk = pltpu.roll(blk, shift, axis=1)
            bigq_ref[j * 2 * half:(j + 1) * 2 * half, t * w:(t + 1) * w] = jnp.where(
                lane_w // S5_GROUP == j, blk, 0.0).astype(BF16)

    u = [u_ref[pl.ds(t, n, stride=t_blk), :] for t in range(t_blk)]
    lhs = jnp.concatenate(u, axis=-1).astype(BF16)
    incr = _dot(lhs, bigp_ref[...])
    row = lax.broadcasted_iota(jnp.int32, (n, 2 * half), 0)
    lane = lax.broadcasted_iota(jnp.int32, (1, 2 * half), 1)
    carried = []
    for j in range(S5_LG):
        h = incr[:, j * 2 * half:(j + 1) * 2 * half]
        a = a_ref[:, j * 2 * half:(j + 1) * 2 * half]
        shift = 1
        while shift < n:
            a_sw = pltpu.roll(a, half, axis=1)
            a1 = jnp.where(lane < half, a, a_sw)
            a2 = jnp.where(lane < half, -a_sw, a)
            x = jnp.where(row >= shift, pltpu.roll(h, shift, axis=0), 0.0)
            h = h + a1 * x + a2 * pltpu.roll(x, half, axis=1)
            a = a1 * a + a2 * a_sw
            shift *= 2
        carried.append(jnp.where(row >= 1, pltpu.roll(h, 1, axis=0), 0.0).astype(BF16))
    y = _dot(lhs, bigv_ref[...]) + _dot(jnp.concatenate(carried, axis=-1), bigq_ref[...])
    d = d_ref[...]
    for t in range(t_blk):
        yt = y[:, t * w:(t + 1) * w] + d * u[t]
        o_ref[pl.ds(t, n, stride=t_blk), :] = jax.nn.gelu(yt, approximate=True)


def _s5(proj, tables):
    s = proj.shape[0]
    t = S5_T
    w = S5_LG * S5_GROUP
    nq = S5_CH // w
    two_p = 2 * S5_STATE
    v_tau, p_c, q_c, a_tile, d_tile = tables

    def per_tile(arr):
        zeros = (0,) * (arr.ndim - 1)
        return pl.BlockSpec((None,) + arr.shape[1:], lambda q: (q,) + zeros)

    return pl.pallas_call(
        _s5_kernel,
        out_shape=jax.ShapeDtypeStruct((s, S5_CH), F32),
        grid=(nq,),
        in_specs=[pl.BlockSpec((s, w), lambda q: (0, OFF_SU // w + q)),
                  per_tile(v_tau), per_tile(p_c), per_tile(q_c), per_tile(a_tile), per_tile(d_tile)],
        out_specs=pl.BlockSpec((s, w), lambda q: (0, q)),
        scratch_shapes=[pltpu.VMEM((t * w, t * w), BF16), pltpu.VMEM((t * w, S5_LG * two_p), BF16),
                        pltpu.VMEM((S5_LG * two_p, t * w), BF16)],
        compiler_params=_cparams(("parallel",)),
        name="s5",
    )(proj, v_tau, p_c, q_c, a_tile, d_tile)


def _rope_kernel(pos_ref, fr_ref, fd_ref, cr_ref, sr_ref, cd_ref, sd_ref):
    pos = pos_ref[...].astype(F32)
    reps = fr_ref.shape[1] // 128
    ang_r = pos * fr_ref[:, :128]
    lane_r = lax.broadcasted_iota(jnp.int32, ang_r.shape, 1)
    cr_ref[...] = jnp.concatenate([jnp.cos(ang_r)] * reps, axis=-1)
    sin_r = jnp.where(lane_r % RET_DK < RET_DK // 2, -1.0, 1.0) * jnp.sin(ang_r)
    sr_ref[...] = jnp.concatenate([sin_r] * reps, axis=-1)
    ang_d = pos * fd_ref[...]
    lane_d = lax.broadcasted_iota(jnp.int32, ang_d.shape, 1)
    cd_ref[...] = jnp.cos(ang_d)
    sd_ref[...] = jnp.where(lane_d % DIFF_DQK < ROPE_DIM // 2, -1.0, 1.0) * jnp.sin(ang_d)


def _rope_tables(positions, *, tb=512):
    s = positions.shape[0]
    tb = min(tb, s)
    half_r = RET_DK // 2
    inv_r = 1.0 / jnp.power(RET_ROPE_BASE, jnp.arange(half_r, dtype=F32) * (2.0 / RET_DK))
    fr = jnp.tile(inv_r, RET_HEADS * RET_DK // half_r).reshape(1, RET_HEADS * RET_DK)
    half_d = ROPE_DIM // 2
    inv_d = 1.0 / jnp.power(ROPE_THETA, jnp.arange(half_d, dtype=F32) * (2.0 / ROPE_DIM))
    fd_head = jnp.concatenate([inv_d, inv_d, jnp.zeros((DIFF_DQK - ROPE_DIM,), F32)])
    fd = jnp.tile(fd_head, 2).reshape(1, 2 * DIFF_DQK)
    wr, wd = fr.shape[1], fd.shape[1]
    return pl.pallas_call(
        _rope_kernel,
        out_shape=[jax.ShapeDtypeStruct((s, wr), F32), jax.ShapeDtypeStruct((s, wr), F32),
                   jax.ShapeDtypeStruct((s, wd), F32), jax.ShapeDtypeStruct((s, wd), F32)],
        grid=(s // tb,),
        in_specs=[pl.BlockSpec((tb, 1), lambda i: (i, 0)),
                  pl.BlockSpec((1, wr), lambda i: (0, 0)),
                  pl.BlockSpec((1, wd), lambda i: (0, 0))],
        out_specs=[pl.BlockSpec((tb, wr), lambda i: (i, 0)), pl.BlockSpec((tb, wr), lambda i: (i, 0)),
                   pl.BlockSpec((tb, wd), lambda i: (i, 0)), pl.BlockSpec((tb, wd), lambda i: (i, 0))],
        compiler_params=_cparams(("parallel",)),
        name="rope_tables",
    )(positions.reshape(s, 1), fr, fd)


def _swap_halves(x, group, half):
    n = x.shape[-1]
    lane = lax.broadcasted_iota(jnp.int32, x.shape, x.ndim - 1)
    return jnp.where(lane % group < half,
                     pltpu.roll(x, n - half, axis=x.ndim - 1),
                     pltpu.roll(x, half, axis=x.ndim - 1))


def _ret_kernel(q_ref, k_ref, v_ref, g_ref, cos_ref, sin_ref, gain_ref, o_ref, r_ref):
    @pl.when(pl.program_id(0) == 0)
    def _():
        r_ref[...] = jnp.zeros_like(r_ref)

    c = q_ref.shape[0]
    cos = cos_ref[...]
    sin = sin_ref[...]
    q = q_ref[...]
    k = k_ref[...]
    q = q * cos + _swap_halves(q, RET_DK, RET_DK // 2) * sin
    k = (k * cos + _swap_halves(k, RET_DK, RET_DK // 2) * sin) * (RET_DK ** -0.5)
    ti = lax.broadcasted_iota(jnp.int32, (c, c), 0)
    si = lax.broadcasted_iota(jnp.int32, (c, c), 1)
    tri = ti >= si
    rel = jnp.where(tri, ti - si, 0).astype(F32)
    idx = lax.broadcasted_iota(jnp.int32, (c, 1), 0).astype(F32)
    gain = gain_ref[...]
    for h in range(RET_HEADS):
        log_gamma = math.log1p(-(2.0 ** (-5.0 - h)))
        qh = q[:, h * RET_DK:(h + 1) * RET_DK]
        kh = k[:, h * RET_DK:(h + 1) * RET_DK]
        vh = v_ref[:, h * RET_DV:(h + 1) * RET_DV]
        d_intra = jnp.where(tri, jnp.exp(rel * log_gamma), 0.0)
        scores = _dot_nt(qh.astype(BF16), kh.astype(BF16)) * d_intra
        r = r_ref[h]
        xi = jnp.exp((idx + 1.0) * log_gamma)
        o = _dot(scores.astype(BF16), vh.astype(BF16)) + _dot(qh.astype(BF16), r.astype(BF16)) * xi
        zeta = jnp.exp((c - 1.0 - idx) * log_gamma)
        r_ref[h] = math.exp(c * log_gamma) * r + _dot_tn((kh * zeta).astype(BF16), vh.astype(BF16))
        sl = slice(h * RET_DV, (h + 1) * RET_DV)
        y = _rms(o) * gain[:, sl] * _silu(g_ref[:, sl])
        o_ref[:, sl] = y.astype(o_ref.dtype)


def _retention(proj, cos_r, sin_r, gain, *, c=RET_C):
    s = proj.shape[0]
    c = min(c, s)
    wq, wv = RET_HEADS * RET_DK, RET_HEADS * RET_DV
    return pl.pallas_call(
        _ret_kernel,
        out_shape=jax.ShapeDtypeStruct((s, wv), BF16),
        grid=(s // c,),
        in_specs=[pl.BlockSpec((c, wq), lambda i: (i, OFF_RQ // wq)),
                  pl.BlockSpec((c, wq), lambda i: (i, OFF_RK // wq)),
                  pl.BlockSpec((c, wv), lambda i: (i, OFF_RV // wv)),
                  pl.BlockSpec((c, wv), lambda i: (i, OFF_RG // wv)),
                  pl.BlockSpec((c, wq), lambda i: (i, 0)),
                  pl.BlockSpec((c, wq), lambda i: (i, 0)),
                  pl.BlockSpec((1, wv), lambda i: (0, 0))],
        out_specs=pl.BlockSpec((c, wv), lambda i: (i, 0)),
        scratch_shapes=[pltpu.VMEM((RET_HEADS, RET_DK, RET_DV), F32)],
        compiler_params=_cparams(("arbitrary",)),
        name="retention",
    )(proj, proj, proj, proj, cos_r, sin_r, gain.reshape(1, -1))


def _diff_prep_kernel(q_ref, k_ref, v_ref, cos_ref, sin_ref, qo_ref, ko_ref, vo_ref):
    reps = q_ref.shape[1] // cos_ref.shape[1]
    cos = jnp.concatenate([cos_ref[...]] * reps, axis=-1)
    sin = jnp.concatenate([sin_ref[...]] * reps, axis=-1)
    q = q_ref[...]
    k = k_ref[...]
    q = (q * cos + _swap_halves(q, DIFF_DQK, ROPE_DIM // 2) * sin) * (DIFF_DQK ** -0.5 * LOG2_E)
    k = k * cos + _swap_halves(k, DIFF_DQK, ROPE_DIM // 2) * sin
    qo_ref[...] = q.T.astype(BF16)
    ko_ref[...] = k.astype(BF16)
    vo_ref[...] = v_ref[...].T.astype(BF16)


def _diff_prep(proj, cos_d, sin_d, *, tb=512):
    s = proj.shape[0]
    tb = min(tb, s)
    w = 2 * DIFF_HEADS * DIFF_DQK
    wt = cos_d.shape[1]
    out_t = jax.ShapeDtypeStruct((w, s), BF16)
    return pl.pallas_call(
        _diff_prep_kernel,
        out_shape=[out_t, jax.ShapeDtypeStruct((s, w), BF16), out_t],
        grid=(s // tb,),
        in_specs=[pl.BlockSpec((tb, w), lambda i: (i, OFF_DQ // w)),
                  pl.BlockSpec((tb, w), lambda i: (i, OFF_DK // w)),
                  pl.BlockSpec((tb, w), lambda i: (i, OFF_DV // w)),
                  pl.BlockSpec((tb, wt), lambda i: (i, 0)),
                  pl.BlockSpec((tb, wt), lambda i: (i, 0))],
        out_specs=[pl.BlockSpec((w, tb), lambda i: (0, i)),
                   pl.BlockSpec((tb, w), lambda i: (i, 0)),
                   pl.BlockSpec((w, tb), lambda i: (0, i))],
        compiler_params=_cparams(("parallel",)),
        name="diff_prep",
    )(proj, proj, proj, cos_d, sin_d)


def _diff_attn_kernel(qt_ref, k_ref, vt_ref, lq1_ref, lk1_ref, lq2_ref, lk2_ref, gain_ref, o_ref,
                      sa_ref, sb_ref, ca_ref, cb_ref, m0_ref, m1_ref, a0_ref, a1_ref,
                      *, lambda_init):
    i = pl.program_id(1)
    tq = qt_ref.shape[1]
    qt = qt_ref[...]
    feat = lax.broadcasted_iota(jnp.int32, qt.shape, 0)
    zero = jnp.zeros_like(qt)
    qts = (jnp.where(feat < DIFF_DQK, qt, zero), jnp.where(feat >= DIFF_DQK, qt, zero))
    m_refs, a_refs = (m0_ref, m1_ref), (a0_ref, a1_ref)
    ones_rows = jnp.ones((ATT_ONES_ROWS, tq), BF16)
    for mp in range(2):
        m_refs[mp][...] = jnp.full_like(m_refs[mp], MASK_VALUE)
        a_refs[mp][...] = jnp.zeros_like(a_refs[mp])

    def scores(t, s_ref, c_ref):
        kb = k_ref[pl.ds(pl.multiple_of(t * tq, tq), tq), :]
        for mp in range(2):
            s = _dot(kb, qts[mp])
            s_ref[mp] = s
            c_ref[mp] = jnp.max(s, axis=0, keepdims=True)

    def accumulate(t, s_ref, c_ref, diagonal):
        vtb = jnp.concatenate([vt_ref[:, pl.ds(pl.multiple_of(t * tq, tq), tq)], ones_rows],
                              axis=0)
        for mp in range(2):
            s = s_ref[mp]
            if diagonal:
                ki = lax.broadcasted_iota(jnp.int32, (tq, tq), 0)
                qi = lax.broadcasted_iota(jnp.int32, (tq, tq), 1)
                s = jnp.where(ki <= qi, s, MASK_VALUE)
                s_max = jnp.max(s, axis=0, keepdims=True)
            else:
                s_max = c_ref[mp]
            m_old = m_refs[mp][...]
            m_new = jnp.maximum(m_old, s_max)
            alpha = jnp.exp2(m_old - m_new)
            p = jnp.exp2(s - m_new)
            a_refs[mp][...] = alpha * a_refs[mp][...] + _dot(vtb, p.astype(BF16))
            m_refs[mp][...] = m_new

    def pair(jj, carry):
        t = 2 * jj
        scores(t + 1, sb_ref, cb_ref)
        accumulate(t, sa_ref, ca_ref, False)
        scores(t + 2, sa_ref, ca_ref)
        accumulate(t + 1, sb_ref, cb_ref, False)
        return carry

    scores(0, sa_ref, ca_ref)
    lax.fori_loop(0, i // 2, pair, 0)

    @pl.when(i % 2 == 0)
    def _():
        accumulate(i, sa_ref, ca_ref, True)

    @pl.when(i % 2 == 1)
    def _():
        scores(i, sb_ref, cb_ref)
        accumulate(i - 1, sa_ref, ca_ref, False)
        accumulate(i, sb_ref, cb_ref, True)

    lam = (jnp.exp(jnp.sum(lq1_ref[...] * lk1_ref[...], axis=-1, keepdims=True))
           - jnp.exp(jnp.sum(lq2_ref[...] * lk2_ref[...], axis=-1, keepdims=True)) + lambda_init)

    def normalised(mp):
        acc = a_refs[mp][...]
        return acc[:DIFF_DV] / acc[DIFF_DV:DIFF_DV + 1]

    o = normalised(0) - lam * normalised(1)
    o = o * lax.rsqrt(jnp.mean(o * o, axis=0, keepdims=True) + RMS_EPS)
    y = o * gain_ref[...] * (1.0 - lambda_init)
    o_ref[...] = y.T.astype(o_ref.dtype)


def _diff_attention(qt, kr, vt, lq1, lk1, lq2, lk2, gain, lambda_init, *, tq=ATT_T):
    s = kr.shape[0]
    tq = min(tq, s)
    wh = 2 * DIFF_DQK
    lam_spec = pl.BlockSpec((1, DIFF_DQK), lambda h, i: (0, 0))
    score = pltpu.VMEM((2, tq, tq), F32)
    stat = pltpu.VMEM((1, tq), F32)
    acc = pltpu.VMEM((DIFF_DV + ATT_ONES_ROWS, tq), F32)
    return pl.pallas_call(
        functools.partial(_diff_attn_kernel, lambda_init=lambda_init),
        out_shape=jax.ShapeDtypeStruct((s, DIFF_HEADS * DIFF_DV), BF16),
        grid=(DIFF_HEADS, s // tq),
        in_specs=[pl.BlockSpec((wh, tq), lambda h, i: (h, i)),
                  pl.BlockSpec((s, wh), lambda h, i: (0, h)),
                  pl.BlockSpec((DIFF_DV, s), lambda h, i: (h, 0)),
                  lam_spec, lam_spec, lam_spec, lam_spec,
                  pl.BlockSpec((DIFF_DV, 1), lambda h, i: (h, 0))],
        out_specs=pl.BlockSpec((tq, DIFF_DV), lambda h, i: (i, h)),
        scratch_shapes=[score, score, pltpu.VMEM((2, 1, tq), F32), pltpu.VMEM((2, 1, tq), F32),
                        stat, stat, acc, acc],
        compiler_params=_cparams(("parallel", "arbitrary")),
        name="diff_attention",
    )(qt, kr, vt, lq1.reshape(1, -1), lk1.reshape(1, -1), lq2.reshape(1, -1), lk2.reshape(1, -1),
      gain.reshape(-1, 1))


def _merge_kernel(x_ref, yh_ref, ys_ref, yr_ref, yd_ref, gt_ref, wh_ref, ws_ref, wr_ref, wd_ref,
                  wo_ref, g_ref, b_ref, o_ref, mg_ref, *, tn):
    d = D_MODEL
    yh = yh_ref[...]
    ys = ys_ref[...].astype(BF16)
    yr = yr_ref[...]
    yd = yd_ref[...]
    for n in range(d // tn):
        c = slice(n * tn, (n + 1) * tn)
        c2 = slice(d + n * tn, d + (n + 1) * tn)
        up_h = _dot(yh, wh_ref[:, c])
        up_s = _dot(ys, ws_ref[:, c]) * jax.nn.sigmoid(_dot(ys, ws_ref[:, c2]))
        up_r = _dot(yr, wr_ref[:, c])
        up_d = _dot(yd, wd_ref[:, c])

        def gate(b):
            return gt_ref[:, b * d + n * tn:b * d + (n + 1) * tn].astype(F32)

        mg = gate(0) * up_h + gate(1) * up_s + gate(2) * up_r + gate(3) * up_d
        mg_ref[:, c] = mg.astype(BF16)
    y = DN_ALPHA * x_ref[...] + _dot(mg_ref[...], wo_ref[...])
    o_ref[...] = _layer_norm(y, g_ref[...], b_ref[...])


def _merge(x, yh, ys, yr, yd, gates, wh, ws, wr, wd, wo, g, b, l, *, tm=256, tn=512):
    s, d = x.shape
    tm = min(tm, s)
    wy = yh.shape[1]

    def rows(width):
        return pl.BlockSpec((tm, width), lambda i: (i, 0))

    def whole(arr):
        if arr.ndim == 3:
            return pl.BlockSpec((None,) + arr.shape[1:], lambda i: (l, 0, 0),
                                pipeline_mode=pl.Buffered(1))
        return pl.BlockSpec(arr.shape, lambda i: (0, 0), pipeline_mode=pl.Buffered(1))

    g2, b2 = g.reshape(1, d), b.reshape(1, d)
    return pl.pallas_call(
        functools.partial(_merge_kernel, tn=tn),
        out_shape=jax.ShapeDtypeStruct((s, d), F32),
        grid=(s // tm,),
        in_specs=[rows(d), rows(wy), rows(wy), rows(wy), rows(wy), rows(GATE_WIDTH),
                  whole(wh), whole(ws), whole(wr), whole(wd), whole(wo), whole(g2), whole(b2)],
        out_specs=rows(d),
        scratch_shapes=[pltpu.VMEM((tm, d), BF16)],
        compiler_params=_cparams(("parallel",)),
        name="merge_out_ln",
    )(x, yh, ys, yr, yd, gates, wh, ws, wr, wd, wo, g2, b2)


def kernel(x, positions, ffa_w1, ffa_w3, ffa_w2, ln_a_g, ln_a_b, w_in, hg_lb_logits, hg_norm_g, s5_lam_re, s5_lam_im, s5_log_dt, s5_b_re, s5_b_im, s5_c_re, s5_c_im, s5_d, ret_norm_g, diff_lam_q1, diff_lam_k1, diff_lam_q2, diff_lam_k2, diff_norm_g, w_up_hg, w_up_s5, w_up_ret, w_up_diff, w_out, ln_m_g, ln_m_b, ffb_w1, ffb_w3, ffb_w2, ln_b_g, ln_b_b):
    bsz, s, d = x.shape
    assert bsz == 1 and d == D_MODEL
    depth = w_in.shape[0]
    x = x.reshape(s, d)

    p_lb = jax.nn.softmax(hg_lb_logits.astype(F32), axis=0)
    lower_bounds = jnp.maximum(jnp.cumsum(p_lb, axis=0) - p_lb[0], 0.0)
    cos_r, sin_r, cos_d, sin_d = _rope_tables(positions.reshape(s))

    w_up_hg, w_up_s5, w_up_ret, w_up_diff, w_out = [
        w.astype(BF16) for w in (w_up_hg, w_up_s5, w_up_ret, w_up_diff, w_out)]

    for l in range(depth):
        x, xb = _ffn_ln(x, ffa_w1, ffa_w3, ffa_w2, ln_a_g[l], ln_a_b[l], l, emit_bf16=True)
        proj = _in_proj_mix(xb, w_in, l)
        gates, y_hg = _gate_hgrn2(xb, w_in, l, proj, lower_bounds[l], hg_norm_g[l])
        tables = _s5_tables(s5_lam_re[l], s5_lam_im[l], s5_log_dt[l], s5_b_re[l], s5_b_im[l],
                            s5_c_re[l], s5_c_im[l], s5_d[l])
        y_s5 = _s5(proj, tables)
        y_r = _retention(proj, cos_r, sin_r, ret_norm_g[l])
        lambda_init = 0.8 - 0.6 * math.exp(-0.3 * l)
        qt, kr, vt = _diff_prep(proj, cos_d, sin_d)
        y_d = _diff_attention(qt, kr, vt, diff_lam_q1[l], diff_lam_k1[l], diff_lam_q2[l],
                              diff_lam_k2[l], diff_norm_g[l], lambda_init)

        x = _merge(x, y_hg, y_s5, y_r, y_d, gates, w_up_hg, w_up_s5, w_up_ret, w_up_diff, w_out,
                   ln_m_g[l], ln_m_b[l], l)
        x = _ffn_ln(x, ffb_w1, ffb_w3, ffb_w2, ln_b_g[l], ln_b_b[l], l)
    return x.reshape(bsz, s, d)
```

```python
import functools
import math

import jax
import jax.numpy as jnp
from jax import lax
from jax.experimental import pallas as pl
from jax.experimental.pallas import tpu as pltpu

F32 = jnp.float32
BF16 = jnp.bfloat16

D_MODEL = 2048
DEPTH = 2
N_BRANCH = 4
HG_HEADS, HG_DK, HG_DV = 4, 128, 128
S5_CH, S5_GROUP, S5_STATE = 512, 16, 64
S5_GROUPS = S5_CH // S5_GROUP
RET_HEADS, RET_DK, RET_DV = 4, 64, 128
RET_ROPE_BASE = 10000.0
DIFF_HEADS, DIFF_DQK, DIFF_DV = 4, 64, 128
ROPE_THETA = 500000.0
ROPE_DIM = DIFF_DQK // 4
D_FF = 5632
LN_EPS = 1e-5
RMS_EPS = 1e-6
MASK_VALUE = -1e30
DN_ALPHA = (2 * DEPTH) ** 0.25

OFF_HQ, OFF_HF, OFF_HI, OFF_HG = 0, 512, 1024, 1536
OFF_SU = 2048
OFF_RQ, OFF_RK, OFF_RV, OFF_RG = 2560, 2816, 3072, 3584
OFF_DQ, OFF_DK, OFF_DV = 4096, 4608, 5120
OFF_GL = 5632
MIX_WIDTH = OFF_GL
GATE_WIDTH = N_BRANCH * D_MODEL

VMEM_LIMIT_BYTES = 58 * 1024 * 1024

S5_T = 16
S5_LG = 8
HG_SUB = 16
RET_C = 256
FFN_LAST_SPLIT = 4
ATT_T = 512
ATT_ONES_ROWS = 16
LOG2_E = math.log2(math.e)


def _cparams(sem):
    return pltpu.CompilerParams(dimension_semantics=sem, vmem_limit_bytes=VMEM_LIMIT_BYTES)


def _layer_norm(y, g, b):
    mu = jnp.mean(y, axis=-1, keepdims=True)
    d = y - mu
    var = jnp.mean(d * d, axis=-1, keepdims=True)
    return d * lax.rsqrt(var + LN_EPS) * g + b


def _rms(o):
    return o * lax.rsqrt(jnp.mean(o * o, axis=-1, keepdims=True) + RMS_EPS)


def _silu(x):
    return x * jax.nn.sigmoid(x)


def _dot(a, b):
    return jnp.dot(a, b, preferred_element_type=F32)


def _dot_nt(a, b):
    return lax.dot_general(a, b, (((1,), (1,)), ((), ())), preferred_element_type=F32)


def _dot_tn(a, b):
    return lax.dot_general(a, b, (((0,), (0,)), ((), ())), preferred_element_type=F32)


def _ffn_kernel(x_ref, w1_ref, w3_ref, w2_ref, g_ref, b_ref, o_ref, xb_ref, *, emit_bf16):
    f = pl.program_id(1)

    last = pl.num_programs(1) - 1

    @pl.when(f == 0)
    def _():
        xb_ref[...] = x_ref[...].astype(BF16)
        o_ref[...] = jnp.zeros_like(o_ref)

    def partial_sum(rows, w1, w3, w2):
        xb = xb_ref[rows, :]
        h = (_silu(_dot(xb, w1)) * _dot(xb, w3)).astype(BF16)
        return _dot(h, w2)

    def weights():
        return w1_ref[...].astype(BF16), w3_ref[...].astype(BF16), w2_ref[...].astype(BF16)

    @pl.when(f < last)
    def _():
        o_ref[...] += partial_sum(slice(None), *weights())

    @pl.when(f == last)
    def _():
        w = weights()
        slab = o_ref.shape[0] // FFN_LAST_SPLIT
        for r in range(FFN_LAST_SPLIT):
            rows = slice(r * slab, (r + 1) * slab)
            y = DN_ALPHA * x_ref[rows, :] + 0.5 * (o_ref[rows, :] + partial_sum(rows, *w))
            out = _layer_norm(y, g_ref[...], b_ref[...])
            o_ref[rows, :] = out
            if emit_bf16:
                xb_ref[rows, :] = out.astype(BF16)


def _ffn_ln(x, w1, w3, w2, g, b, l, *, emit_bf16=False, tm=1024, tf=256):
    s, d = x.shape
    f = w1.shape[2]
    tm = min(tm, s)
    row_tile = pl.BlockSpec((tm, d), lambda i, j: (i, 0))
    out_f32 = jax.ShapeDtypeStruct((s, d), F32)
    return pl.pallas_call(
        functools.partial(_ffn_kernel, emit_bf16=emit_bf16),
        out_shape=[out_f32, jax.ShapeDtypeStruct((s, d), BF16)] if emit_bf16 else out_f32,
        grid=(s // tm, f // tf),
        in_specs=[
            row_tile,
            pl.BlockSpec((None, d, tf), lambda i, j: (l, 0, j)),
            pl.BlockSpec((None, d, tf), lambda i, j: (l, 0, j)),
            pl.BlockSpec((None, tf, d), lambda i, j: (l, j, 0)),
            pl.BlockSpec((1, d), lambda i, j: (0, 0)),
            pl.BlockSpec((1, d), lambda i, j: (0, 0)),
        ],
        out_specs=[row_tile, pl.BlockSpec((tm, d), lambda i, j: (i, 0), pipeline_mode=pl.Buffered(1))]
        if emit_bf16 else row_tile,
        scratch_shapes=[] if emit_bf16 else [pltpu.VMEM((tm, d), BF16)],
        compiler_params=_cparams(("parallel", "arbitrary")),
        name="ffn_ln",
    )(x, w1, w3, w2, g.reshape(1, d), b.reshape(1, d))


def _proj_kernel(xb_ref, w_ref, o_ref):
    o_ref[...] = _dot(xb_ref[...], w_ref[...].astype(BF16))


def _in_proj_mix(xb, w_in, l, *, tm=2048, tn=512):
    s, d = xb.shape
    tm = min(tm, s)
    return pl.pallas_call(
        _proj_kernel,
        out_shape=jax.ShapeDtypeStruct((s, MIX_WIDTH), F32),
        grid=(s // tm, MIX_WIDTH // tn),
        in_specs=[pl.BlockSpec((tm, d), lambda i, j: (i, 0)),
                  pl.BlockSpec((None, d, tn), lambda i, j: (l, 0, j))],
        out_specs=pl.BlockSpec((tm, tn), lambda i, j: (i, j)),
        compiler_params=_cparams(("parallel", "arbitrary")),
        name="in_proj_mix",
    )(xb, w_in)


def _cumsum_rows(x, row):
    n = x.shape[0]
    shift = 1
    while shift < n:
        x = x + jnp.where(row >= shift, pltpu.roll(x, shift, axis=0), 0.0)
        shift *= 2
    return x


def _hgrn2_sub_chunk_step(q_ref, f_ref, i_ref, g_ref, lb_ref, gain_ref, o_ref, st_ref, sub):
    half = sub // 2
    row = lax.broadcasted_iota(jnp.int32, (sub, HG_DK), 0)
    row_h = lax.broadcasted_iota(jnp.int32, (half, HG_DK), 0)

    def pair_sum(qh, bh, ks, bs, vs, mask_from):
        dec = jnp.exp(bh - bs)
        if mask_from is not None:
            dec = jnp.where(row_h >= mask_from, dec, 0.0)
        return jnp.sum(qh * ks * dec, axis=-1, keepdims=True) * vs

    def head_step(rows, h):
        cs = slice(h * HG_DK, (h + 1) * HG_DK)
        lb = lb_ref[:, cs]
        fl = f_ref[rows, cs]
        q = _silu(q_ref[rows, cs])
        v = i_ref[rows, cs]
        log_f = jax.nn.log_sigmoid(fl) + jnp.log1p(lb * jnp.exp(-fl))
        k = (1.0 - lb) * jax.nn.sigmoid(-fl)
        b = _cumsum_rows(log_f, row)

        st = st_ref[h]
        o = _dot_nt((q * jnp.exp(b)).astype(BF16), st.astype(BF16))
        q_t, q_b, b_t, b_b = q[:half], q[half:], b[:half], b[half:]
        o_t, o_b = o[:half], o[half:]
        for s_ in range(half):
            ks, bs, vs = k[s_:s_ + 1], b[s_:s_ + 1], v[s_:s_ + 1]
            o_t = o_t + pair_sum(q_t, b_t, ks, bs, vs, s_ if s_ > 0 else None)
            o_b = o_b + pair_sum(q_b, b_b, ks, bs, vs, None)
        for s_ in range(half):
            r = half + s_
            ks, bs, vs = k[r:r + 1], b[r:r + 1], v[r:r + 1]
            o_b = o_b + pair_sum(q_b, b_b, ks, bs, vs, s_ if s_ > 0 else None)
        o = jnp.concatenate([o_t, o_b], axis=0)

        b_end = b[sub - 1:sub]
        kd = (k * jnp.exp(b_end - b)).astype(BF16)
        st_ref[h] = st * jnp.exp(b_end) + _dot_tn(v.astype(BF16), kd)

        y = _rms(o) * gain_ref[:, cs] * _silu(g_ref[rows, cs])
        o_ref[rows, cs] = y.astype(o_ref.dtype)

    def step(c):
        rows = pl.ds(pl.multiple_of(c * sub, sub), sub)
        for h in range(HG_HEADS):
            head_step(rows, h)

    return step


def _gate_hgrn2_kernel(xb_ref, w_ref, q_ref, f_ref, i_ref, g_ref, lb_ref, gain_ref,
                       gate_ref, y_ref, wb_ref, st_ref, *, sub):
    i, j = pl.program_id(0), pl.program_id(1)

    @pl.when((i == 0) & (j == 0))
    def _():
        st_ref[...] = jnp.zeros_like(st_ref)

    wb_ref[...] = w_ref[...].astype(BF16)
    n_it = q_ref.shape[0] // sub
    slab = xb_ref.shape[0] // n_it
    hgrn2_step = _hgrn2_sub_chunk_step(q_ref, f_ref, i_ref, g_ref, lb_ref, gain_ref, y_ref, st_ref, sub)

    def gate_rows(start, size):
        rows = pl.ds(pl.multiple_of(start, size), size)
        y = _dot(xb_ref[rows, :], wb_ref[...])
        gate_ref[rows, :] = (0.5 * jnp.tanh(0.5 * y) + 0.5).astype(gate_ref.dtype)

    def body(c, carry):
        gate_rows(c * slab, slab)
        hgrn2_step(c)
        return carry

    lax.fori_loop(0, n_it, body, 0, unroll=True)


def _gate_hgrn2(xb, w_in, l, proj, lb, gain, *, tm=2048, tn=512, sub=HG_SUB):
    s, d = xb.shape
    tm = min(tm, s)
    n_j = GATE_WIDTH // tn
    th = tm // n_j
    w = HG_HEADS * HG_DK
    cb0 = OFF_GL // tn

    def hcol(off):
        return pl.BlockSpec((th, w), lambda i, j: (i * n_j + j, off // w))

    return pl.pallas_call(
        functools.partial(_gate_hgrn2_kernel, sub=sub),
        out_shape=[jax.ShapeDtypeStruct((s, GATE_WIDTH), BF16),
                   jax.ShapeDtypeStruct((s, HG_HEADS * HG_DV), BF16)],
        grid=(s // tm, n_j),
        in_specs=[pl.BlockSpec((tm, d), lambda i, j: (i, 0)),
                  pl.BlockSpec((None, d, tn), lambda i, j: (l, 0, cb0 + j)),
                  hcol(OFF_HQ), hcol(OFF_HF), hcol(OFF_HI), hcol(OFF_HG),
                  pl.BlockSpec((1, w), lambda i, j: (0, 0)),
                  pl.BlockSpec((1, w), lambda i, j: (0, 0))],
        out_specs=[pl.BlockSpec((tm, tn), lambda i, j: (i, j)),
                   pl.BlockSpec((th, w), lambda i, j: (i * n_j + j, 0))],
        scratch_shapes=[pltpu.VMEM((d, tn), BF16), pltpu.VMEM((HG_HEADS, HG_DV, HG_DK), F32)],
        compiler_params=_cparams(("arbitrary", "arbitrary")),
        name="gate_hgrn2",
    )(xb, w_in, proj, proj, proj, proj, lb.reshape(1, -1), gain.reshape(1, -1))


def _s5_tables(lam_re, lam_im, log_dt, b_re, b_im, c_re, c_im, d_skip):
    hi = lax.Precision.HIGHEST
    g, p, cg, t = S5_GROUPS, S5_STATE, S5_GROUP, S5_T
    dt = jnp.exp(log_dt.astype(F32))[:, None]
    lam_re = lam_re.astype(F32)
    lam_im = lam_im.astype(F32)
    mag = jnp.exp(dt * lam_re)
    ab_re = mag * jnp.cos(dt * lam_im)
    ab_im = mag * jnp.sin(dt * lam_im)
    den = jnp.square(lam_re) + jnp.square(lam_im)
    nr = ab_re - 1.0
    coef_re = (nr * lam_re + ab_im * lam_im) / den
    coef_im = (ab_im * lam_re - nr * lam_im) / den
    b_re = b_re.astype(F32)
    b_im = b_im.astype(F32)
    bb_re = coef_re[..., None] * b_re - coef_im[..., None] * b_im
    bb_im = coef_re[..., None] * b_im + coef_im[..., None] * b_re
    c_re = c_re.astype(F32)
    c_im = c_im.astype(F32)
    tau = jnp.arange(t + 1, dtype=F32)[:, None, None]
    pmag = jnp.exp(tau * (dt * lam_re)[None])
    pw_re = pmag * jnp.cos(tau * (dt * lam_im)[None])
    pw_im = pmag * jnp.sin(tau * (dt * lam_im)[None])
    lb_re = pw_re[:t, :, :, None] * bb_re[None] - pw_im[:t, :, :, None] * bb_im[None]
    lb_im = pw_re[:t, :, :, None] * bb_im[None] + pw_im[:t, :, :, None] * bb_re[None]
    ktau = (jnp.einsum('gcp,tgpd->tgcd', c_re, lb_re, precision=hi)
            - jnp.einsum('gcp,tgpd->tgcd', c_im, lb_im, precision=hi))
    rev = lb_re[::-1], lb_im[::-1]
    p_op = jnp.concatenate([rev[0].transpose(1, 0, 3, 2), rev[1].transpose(1, 0, 3, 2)],
                           axis=-1).reshape(g, t * cg, 2 * p)
    cl_re = c_re[None] * pw_re[1:, :, None, :] - c_im[None] * pw_im[1:, :, None, :]
    cl_im = c_re[None] * pw_im[1:, :, None, :] + c_im[None] * pw_re[1:, :, None, :]
    q_op = jnp.concatenate([cl_re.transpose(1, 3, 0, 2), -cl_im.transpose(1, 3, 0, 2)],
                           axis=1).reshape(g, 2 * p, t * cg)
    a_blk = jnp.concatenate([pw_re[t], pw_im[t]], axis=-1)
    lg, nq = S5_LG, S5_GROUPS // S5_LG
    eye = jnp.eye(lg, dtype=F32)
    k5 = ktau.reshape(t, nq, lg, cg, cg)
    v_tau = jnp.einsum('tqjcd,jk->qtjdkc', k5, eye).reshape(nq, t, lg * cg, lg * cg)
    p_c = p_op.reshape(nq, lg, t, cg, 2 * p)
    q_c = q_op.reshape(nq, lg, 2 * p, t * cg)
    a_tile = a_blk.reshape(nq, 1, lg * 2 * p)
    d_tile = d_skip.astype(F32).reshape(nq, 1, lg * cg)
    return v_tau.astype(BF16), p_c.astype(BF16), q_c, a_tile, d_tile


def _s5_kernel(u_ref, v_ref, pc_ref, qc_ref, a_ref, d_ref, o_ref, bigv_ref, bigp_ref, bigq_ref):
    t_blk, w, lg = S5_T, S5_LG * S5_GROUP, S5_LG
    half = S5_STATE
    n = u_ref.shape[0] // t_blk

    lane_w = lax.broadcasted_iota(jnp.int32, (2 * half, w), 1)
    bigp_ref[...] = jnp.zeros_like(bigp_ref)
    for s in range(t_blk):
        for t in range(t_blk):
            blk = v_ref[t - s] if t >= s else jnp.zeros((w, w), BF16)
            bigv_ref[s * w:(s + 1) * w, t * w:(t + 1) * w] = blk
        for j in range(lg):
            bigp_ref[s * w + j * S5_GROUP:s * w + (j + 1) * S5_GROUP,
                     j * 2 * half:(j + 1) * 2 * half] = pc_ref[j, s]
    for j in range(lg):
        for t in range(t_blk):
            blk = qc_ref[j, :, (t // lg) * w:(t // lg + 1) * w]
            shift = ((j - t % lg) % lg) * S5_GROUP
            if shift:
                blk = pltpu.roll(blk, shift, axis=1)
            bigq_ref[j * 2 * half:(j + 1) * 2 * half, t * w:(t + 1) * w] = jnp.where(
                lane_w // S5_GROUP == j, blk, 0.0).astype(BF16)

    u = [u_ref[pl.ds(t, n, stride=t_blk), :] for t in range(t_blk)]
    lhs = jnp.concatenate(u, axis=-1).astype(BF16)
    incr = _dot(lhs, bigp_ref[...])
    row = lax.broadcasted_iota(jnp.int32, (n, 2 * half), 0)
    lane = lax.broadcasted_iota(jnp.int32, (1, 2 * half), 1)
    carried = []
    for j in range(S5_LG):
        h = incr[:, j * 2 * half:(j + 1) * 2 * half]
        a = a_ref[:, j * 2 * half:(j + 1) * 2 * half]
        shift = 1
        while shift < n:
            a_sw = pltpu.roll(a, half, axis=1)
            a1 = jnp.where(lane < half, a, a_sw)
            a2 = jnp.where(lane < half, -a_sw, a)
            x = jnp.where(row >= shift, pltpu.roll(h, shift, axis=0), 0.0)
            h = h + a1 * x + a2 * pltpu.roll(x, half, axis=1)
            a = a1 * a + a2 * a_sw
            shift *= 2
        carried.append(jnp.where(row >= 1, pltpu.roll(h, 1, axis=0), 0.0).astype(BF16))
    y = _dot(lhs, bigv_ref[...]) + _dot(jnp.concatenate(carried, axis=-1), bigq_ref[...])
    d = d_ref[...]
    for t in range(t_blk):
        yt = y[:, t * w:(t + 1) * w] + d * u[t]
        o_ref[pl.ds(t, n, stride=t_blk), :] = jax.nn.gelu(yt, approximate=True)


def _s5(proj, tables):
    s = proj.shape[0]
    t = S5_T
    w = S5_LG * S5_GROUP
    nq = S5_CH // w
    two_p = 2 * S5_STATE
    v_tau, p_c, q_c, a_tile, d_tile = tables

    def per_tile(arr):
        zeros = (0,) * (arr.ndim - 1)
        return pl.BlockSpec((None,) + arr.shape[1:], lambda q: (q,) + zeros)

    return pl.pallas_call(
        _s5_kernel,
        out_shape=jax.ShapeDtypeStruct((s, S5_CH), F32),
        grid=(nq,),
        in_specs=[pl.BlockSpec((s, w), lambda q: (0, OFF_SU // w + q)),
                  per_tile(v_tau), per_tile(p_c), per_tile(q_c), per_tile(a_tile), per_tile(d_tile)],
        out_specs=pl.BlockSpec((s, w), lambda q: (0, q)),
        scratch_shapes=[pltpu.VMEM((t * w, t * w), BF16), pltpu.VMEM((t * w, S5_LG * two_p), BF16),
                        pltpu.VMEM((S5_LG * two_p, t * w), BF16)],
        compiler_params=_cparams(("parallel",)),
        name="s5",
    )(proj, v_tau, p_c, q_c, a_tile, d_tile)


def _rope_kernel(pos_ref, fr_ref, fd_ref, cr_ref, sr_ref, cd_ref, sd_ref):
    pos = pos_ref[...].astype(F32)
    reps = fr_ref.shape[1] // 128
    ang_r = pos * fr_ref[:, :128]
    lane_r = lax.broadcasted_iota(jnp.int32, ang_r.shape, 1)
    cr_ref[...] = jnp.concatenate([jnp.cos(ang_r)] * reps, axis=-1)
    sin_r = jnp.where(lane_r % RET_DK < RET_DK // 2, -1.0, 1.0) * jnp.sin(ang_r)
    sr_ref[...] = jnp.concatenate([sin_r] * reps, axis=-1)
    ang_d = pos * fd_ref[...]
    lane_d = lax.broadcasted_iota(jnp.int32, ang_d.shape, 1)
    cd_ref[...] = jnp.cos(ang_d)
    sd_ref[...] = jnp.where(lane_d % DIFF_DQK < ROPE_DIM // 2, -1.0, 1.0) * jnp.sin(ang_d)


def _rope_tables(positions, *, tb=512):
    s = positions.shape[0]
    tb = min(tb, s)
    half_r = RET_DK // 2
    inv_r = 1.0 / jnp.power(RET_ROPE_BASE, jnp.arange(half_r, dtype=F32) * (2.0 / RET_DK))
    fr = jnp.tile(inv_r, RET_HEADS * RET_DK // half_r).reshape(1, RET_HEADS * RET_DK)
    half_d = ROPE_DIM // 2
    inv_d = 1.0 / jnp.power(ROPE_THETA, jnp.arange(half_d, dtype=F32) * (2.0 / ROPE_DIM))
    fd_head = jnp.concatenate([inv_d, inv_d, jnp.zeros((DIFF_DQK - ROPE_DIM,), F32)])
    fd = jnp.tile(fd_head, 2).reshape(1, 2 * DIFF_DQK)
    wr, wd = fr.shape[1], fd.shape[1]
    return pl.pallas_call(
        _rope_kernel,
        out_shape=[jax.ShapeDtypeStruct((s, wr), F32), jax.ShapeDtypeStruct((s, wr), F32),
                   jax.ShapeDtypeStruct((s, wd), F32), jax.ShapeDtypeStruct((s, wd), F32)],
        grid=(s // tb,),
        in_specs=[pl.BlockSpec((tb, 1), lambda i: (i, 0)),
                  pl.BlockSpec((1, wr), lambda i: (0, 0)),
                  pl.BlockSpec((1, wd), lambda i: (0, 0))],
        out_specs=[pl.BlockSpec((tb, wr), lambda i: (i, 0)), pl.BlockSpec((tb, wr), lambda i: (i, 0)),
                   pl.BlockSpec((tb, wd), lambda i: (i, 0)), pl.BlockSpec((tb, wd), lambda i: (i, 0))],
        compiler_params=_cparams(("parallel",)),
        name="rope_tables",
    )(positions.reshape(s, 1), fr, fd)


def _swap_halves(x, group, half):
    n = x.shape[-1]
    lane = lax.broadcasted_iota(jnp.int32, x.shape, x.ndim - 1)
    return jnp.where(lane % group < half,
                     pltpu.roll(x, n - half, axis=x.ndim - 1),
                     pltpu.roll(x, half, axis=x.ndim - 1))


def _ret_kernel(q_ref, k_ref, v_ref, g_ref, cos_ref, sin_ref, gain_ref, o_ref, r_ref):
    @pl.when(pl.program_id(0) == 0)
    def _():
        r_ref[...] = jnp.zeros_like(r_ref)

    c = q_ref.shape[0]
    cos = cos_ref[...]
    sin = sin_ref[...]
    q = q_ref[...]
    k = k_ref[...]
    q = q * cos + _swap_halves(q, RET_DK, RET_DK // 2) * sin
    k = (k * cos + _swap_halves(k, RET_DK, RET_DK // 2) * sin) * (RET_DK ** -0.5)
    ti = lax.broadcasted_iota(jnp.int32, (c, c), 0)
    si = lax.broadcasted_iota(jnp.int32, (c, c), 1)
    tri = ti >= si
    rel = jnp.where(tri, ti - si, 0).astype(F32)
    idx = lax.broadcasted_iota(jnp.int32, (c, 1), 0).astype(F32)
    gain = gain_ref[...]
    for h in range(RET_HEADS):
        log_gamma = math.log1p(-(2.0 ** (-5.0 - h)))
        qh = q[:, h * RET_DK:(h + 1) * RET_DK]
        kh = k[:, h * RET_DK:(h + 1) * RET_DK]
        vh = v_ref[:, h * RET_DV:(h + 1) * RET_DV]
        d_intra = jnp.where(tri, jnp.exp(rel * log_gamma), 0.0)
        scores = _dot_nt(qh.astype(BF16), kh.astype(BF16)) * d_intra
        r = r_ref[h]
        xi = jnp.exp((idx + 1.0) * log_gamma)
        o = _dot(scores.astype(BF16), vh.astype(BF16)) + _dot(qh.astype(BF16), r.astype(BF16)) * xi
        zeta = jnp.exp((c - 1.0 - idx) * log_gamma)
        r_ref[h] = math.exp(c * log_gamma) * r + _dot_tn((kh * zeta).astype(BF16), vh.astype(BF16))
        sl = slice(h * RET_DV, (h + 1) * RET_DV)
        y = _rms(o) * gain[:, sl] * _silu(g_ref[:, sl])
        o_ref[:, sl] = y.astype(o_ref.dtype)


def _retention(proj, cos_r, sin_r, gain, *, c=RET_C):
    s = proj.shape[0]
    c = min(c, s)
    wq, wv = RET_HEADS * RET_DK, RET_HEADS * RET_DV
    return pl.pallas_call(
        _ret_kernel,
        out_shape=jax.ShapeDtypeStruct((s, wv), BF16),
        grid=(s // c,),
        in_specs=[pl.BlockSpec((c, wq), lambda i: (i, OFF_RQ // wq)),
                  pl.BlockSpec((c, wq), lambda i: (i, OFF_RK // wq)),
                  pl.BlockSpec((c, wv), lambda i: (i, OFF_RV // wv)),
                  pl.BlockSpec((c, wv), lambda i: (i, OFF_RG // wv)),
                  pl.BlockSpec((c, wq), lambda i: (i, 0)),
                  pl.BlockSpec((c, wq), lambda i: (i, 0)),
                  pl.BlockSpec((1, wv), lambda i: (0, 0))],
        out_specs=pl.BlockSpec((c, wv), lambda i: (i, 0)),
        scratch_shapes=[pltpu.VMEM((RET_HEADS, RET_DK, RET_DV), F32)],
        compiler_params=_cparams(("arbitrary",)),
        name="retention",
    )(proj, proj, proj, proj, cos_r, sin_r, gain.reshape(1, -1))


def _diff_prep_kernel(q_ref, k_ref, v_ref, cos_ref, sin_ref, qo_ref, ko_ref, vo_ref):
    reps = q_ref.shape[1] // cos_ref.shape[1]
    cos = jnp.concatenate([cos_ref[...]] * reps, axis=-1)
    sin = jnp.concatenate([sin_ref[...]] * reps, axis=-1)
    q = q_ref[...]
    k = k_ref[...]
    q = (q * cos + _swap_halves(q, DIFF_DQK, ROPE_DIM // 2) * sin) * (DIFF_DQK ** -0.5 * LOG2_E)
    k = k * cos + _swap_halves(k, DIFF_DQK, ROPE_DIM // 2) * sin
    qo_ref[...] = q.T.astype(BF16)
    ko_ref[...] = k.astype(BF16)
    vo_ref[...] = v_ref[...].T.astype(BF16)


def _diff_prep(proj, cos_d, sin_d, *, tb=512):
    s = proj.shape[0]
    tb = min(tb, s)
    w = 2 * DIFF_HEADS * DIFF_DQK
    wt = cos_d.shape[1]
    out_t = jax.ShapeDtypeStruct((w, s), BF16)
    return pl.pallas_call(
        _diff_prep_kernel,
        out_shape=[out_t, jax.ShapeDtypeStruct((s, w), BF16), out_t],
        grid=(s // tb,),
        in_specs=[pl.BlockSpec((tb, w), lambda i: (i, OFF_DQ // w)),
                  pl.BlockSpec((tb, w), lambda i: (i, OFF_DK // w)),
                  pl.BlockSpec((tb, w), lambda i: (i, OFF_DV // w)),
                  pl.BlockSpec((tb, wt), lambda i: (i, 0)),
                  pl.BlockSpec((tb, wt), lambda i: (i, 0))],
        out_specs=[pl.BlockSpec((w, tb), lambda i: (0, i)),
                   pl.BlockSpec((tb, w), lambda i: (i, 0)),
                   pl.BlockSpec((w, tb), lambda i: (0, i))],
        compiler_params=_cparams(("parallel",)),
        name="diff_prep",
    )(proj, proj, proj, cos_d, sin_d)


def _diff_attn_kernel(qt_ref, k_ref, vt_ref, lq1_ref, lk1_ref, lq2_ref, lk2_ref, gain_ref, o_ref,
                      sa_ref, sb_ref, ca_ref, cb_ref, m0_ref, m1_ref, a0_ref, a1_ref,
                      *, lambda_init):
    i = pl.program_id(1)
    tq = qt_ref.shape[1]
    qt = qt_ref[...]
    feat = lax.broadcasted_iota(jnp.int32, qt.shape, 0)
    zero = jnp.zeros_like(qt)
    qts = (jnp.where(feat < DIFF_DQK, qt, zero), jnp.where(feat >= DIFF_DQK, qt, zero))
    m_refs, a_refs = (m0_ref, m1_ref), (a0_ref, a1_ref)
    ones_rows = jnp.ones((ATT_ONES_ROWS, tq), BF16)
    for mp in range(2):
        m_refs[mp][...] = jnp.full_like(m_refs[mp], MASK_VALUE)
        a_refs[mp][...] = jnp.zeros_like(a_refs[mp])

    def scores(t, s_ref, c_ref):
        kb = k_ref[pl.ds(pl.multiple_of(t * tq, tq), tq), :]
        for mp in range(2):
            s = _dot(kb, qts[mp])
            s_ref[mp] = s
            c_ref[mp] = jnp.max(s, axis=0, keepdims=True)

    def accumulate(t, s_ref, c_ref, diagonal):
        vtb = jnp.concatenate([vt_ref[:, pl.ds(pl.multiple_of(t * tq, tq), tq)], ones_rows],
                              axis=0)
        for mp in range(2):
            s = s_ref[mp]
            if diagonal:
                ki = lax.broadcasted_iota(jnp.int32, (tq, tq), 0)
                qi = lax.broadcasted_iota(jnp.int32, (tq, tq), 1)
                s = jnp.where(ki <= qi, s, MASK_VALUE)
                s_max = jnp.max(s, axis=0, keepdims=True)
            else:
                s_max = c_ref[mp]
            m_old = m_refs[mp][...]
            m_new = jnp.maximum(m_old, s_max)
            alpha = jnp.exp2(m_old - m_new)
            p = jnp.exp2(s - m_new)
            a_refs[mp][...] = alpha * a_refs[mp][...] + _dot(vtb, p.astype(BF16))
            m_refs[mp][...] = m_new

    def pair(jj, carry):
        t = 2 * jj
        scores(t + 1, sb_ref, cb_ref)
        accumulate(t, sa_ref, ca_ref, False)
        scores(t + 2, sa_ref, ca_ref)
        accumulate(t + 1, sb_ref, cb_ref, False)
        return carry

    scores(0, sa_ref, ca_ref)
    lax.fori_loop(0, i // 2, pair, 0)

    @pl.when(i % 2 == 0)
    def _():
        accumulate(i, sa_ref, ca_ref, True)

    @pl.when(i % 2 == 1)
    def _():
        scores(i, sb_ref, cb_ref)
        accumulate(i - 1, sa_ref, ca_ref, False)
        accumulate(i, sb_ref, cb_ref, True)

    lam = (jnp.exp(jnp.sum(lq1_ref[...] * lk1_ref[...], axis=-1, keepdims=True))
           - jnp.exp(jnp.sum(lq2_ref[...] * lk2_ref[...], axis=-1, keepdims=True)) + lambda_init)

    def normalised(mp):
        acc = a_refs[mp][...]
        return acc[:DIFF_DV] / acc[DIFF_DV:DIFF_DV + 1]

    o = normalised(0) - lam * normalised(1)
    o = o * lax.rsqrt(jnp.mean(o * o, axis=0, keepdims=True) + RMS_EPS)
    y = o * gain_ref[...] * (1.0 - lambda_init)
    o_ref[...] = y.T.astype(o_ref.dtype)


def _diff_attention(qt, kr, vt, lq1, lk1, lq2, lk2, gain, lambda_init, *, tq=ATT_T):
    s = kr.shape[0]
    tq = min(tq, s)
    wh = 2 * DIFF_DQK
    lam_spec = pl.BlockSpec((1, DIFF_DQK), lambda h, i: (0, 0))
    score = pltpu.VMEM((2, tq, tq), F32)
    stat = pltpu.VMEM((1, tq), F32)
    acc = pltpu.VMEM((DIFF_DV + ATT_ONES_ROWS, tq), F32)
    return pl.pallas_call(
        functools.partial(_diff_attn_kernel, lambda_init=lambda_init),
        out_shape=jax.ShapeDtypeStruct((s, DIFF_HEADS * DIFF_DV), BF16),
        grid=(DIFF_HEADS, s // tq),
        in_specs=[pl.BlockSpec((wh, tq), lambda h, i: (h, i)),
                  pl.BlockSpec((s, wh), lambda h, i: (0, h)),
                  pl.BlockSpec((DIFF_DV, s), lambda h, i: (h, 0)),
                  lam_spec, lam_spec, lam_spec, lam_spec,
                  pl.BlockSpec((DIFF_DV, 1), lambda h, i: (h, 0))],
        out_specs=pl.BlockSpec((tq, DIFF_DV), lambda h, i: (i, h)),
        scratch_shapes=[score, score, pltpu.VMEM((2, 1, tq), F32), pltpu.VMEM((2, 1, tq), F32),
                        stat, stat, acc, acc],
        compiler_params=_cparams(("parallel", "arbitrary")),
        name="diff_attention",
    )(qt, kr, vt, lq1.reshape(1, -1), lk1.reshape(1, -1), lq2.reshape(1, -1), lk2.reshape(1, -1),
      gain.reshape(-1, 1))


def _merge_kernel(x_ref, yh_ref, ys_ref, yr_ref, yd_ref, gt_ref, wh_ref, ws_ref, wr_ref, wd_ref,
                  wo_ref, g_ref, b_ref, o_ref, mg_ref, *, tn):
    d = D_MODEL
    yh = yh_ref[...]
    ys = ys_ref[...].astype(BF16)
    yr = yr_ref[...]
    yd = yd_ref[...]
    for n in range(d // tn):
        c = slice(n * tn, (n + 1) * tn)
        c2 = slice(d + n * tn, d + (n + 1) * tn)
        up_h = _dot(yh, wh_ref[:, c])
        up_s = _dot(ys, ws_ref[:, c]) * jax.nn.sigmoid(_dot(ys, ws_ref[:, c2]))
        up_r = _dot(yr, wr_ref[:, c])
        up_d = _dot(yd, wd_ref[:, c])

        def gate(b):
            return gt_ref[:, b * d + n * tn:b * d + (n + 1) * tn].astype(F32)

        mg = gate(0) * up_h + gate(1) * up_s + gate(2) * up_r + gate(3) * up_d
        mg_ref[:, c] = mg.astype(BF16)
    y = DN_ALPHA * x_ref[...] + _dot(mg_ref[...], wo_ref[...])
    o_ref[...] = _layer_norm(y, g_ref[...], b_ref[...])


def _merge(x, yh, ys, yr, yd, gates, wh, ws, wr, wd, wo, g, b, l, *, tm=256, tn=512):
    s, d = x.shape
    tm = min(tm, s)
    wy = yh.shape[1]

    def rows(width):
        return pl.BlockSpec((tm, width), lambda i: (i, 0))

    def whole(arr):
        if arr.ndim == 3:
            return pl.BlockSpec((None,) + arr.shape[1:], lambda i: (l, 0, 0),
                                pipeline_mode=pl.Buffered(1))
        return pl.BlockSpec(arr.shape, lambda i: (0, 0), pipeline_mode=pl.Buffered(1))

    g2, b2 = g.reshape(1, d), b.reshape(1, d)
    return pl.pallas_call(
        functools.partial(_merge_kernel, tn=tn),
        out_shape=jax.ShapeDtypeStruct((s, d), F32),
        grid=(s // tm,),
        in_specs=[rows(d), rows(wy), rows(wy), rows(wy), rows(wy), rows(GATE_WIDTH),
                  whole(wh), whole(ws), whole(wr), whole(wd), whole(wo), whole(g2), whole(b2)],
        out_specs=rows(d),
        scratch_shapes=[pltpu.VMEM((tm, d), BF16)],
        compiler_params=_cparams(("parallel",)),
        name="merge_out_ln",
    )(x, yh, ys, yr, yd, gates, wh, ws, wr, wd, wo, g2, b2)


def kernel(x, positions, ffa_w1, ffa_w3, ffa_w2, ln_a_g, ln_a_b, w_in, hg_lb_logits, hg_norm_g, s5_lam_re, s5_lam_im, s5_log_dt, s5_b_re, s5_b_im, s5_c_re, s5_c_im, s5_d, ret_norm_g, diff_lam_q1, diff_lam_k1, diff_lam_q2, diff_lam_k2, diff_norm_g, w_up_hg, w_up_s5, w_up_ret, w_up_diff, w_out, ln_m_g, ln_m_b, ffb_w1, ffb_w3, ffb_w2, ln_b_g, ln_b_b):
    bsz, s, d = x.shape
    assert bsz == 1 and d == D_MODEL
    depth = w_in.shape[0]
    x = x.reshape(s, d)

    p_lb = jax.nn.softmax(hg_lb_logits.astype(F32), axis=0)
    lower_bounds = jnp.maximum(jnp.cumsum(p_lb, axis=0) - p_lb[0], 0.0)
    cos_r, sin_r, cos_d, sin_d = _rope_tables(positions.reshape(s))

    w_up_hg, w_up_s5, w_up_ret, w_up_diff, w_out = [
        w.astype(BF16) for w in (w_up_hg, w_up_s5, w_up_ret, w_up_diff, w_out)]

    for l in range(depth):
        x, xb = _ffn_ln(x, ffa_w1, ffa_w3, ffa_w2, ln_a_g[l], ln_a_b[l], l, emit_bf16=True)
        proj = _in_proj_mix(xb, w_in, l)
        gates, y_hg = _gate_hgrn2(xb, w_in, l, proj, lower_bounds[l], hg_norm_g[l])
        tables = _s5_tables(s5_lam_re[l], s5_lam_im[l], s5_log_dt[l], s5_b_re[l], s5_b_im[l],
                            s5_c_re[l], s5_c_im[l], s5_d[l])
        y_s5 = _s5(proj, tables)
        y_r = _retention(proj, cos_r, sin_r, ret_norm_g[l])
        lambda_init = 0.8 - 0.6 * math.exp(-0.3 * l)
        qt, kr, vt = _diff_prep(proj, cos_d, sin_d)
        y_d = _diff_attention(qt, kr, vt, diff_lam_q1[l], diff_lam_k1[l], diff_lam_q2[l],
                              diff_lam_k2[l], diff_norm_g[l], lambda_init)

        x = _merge(x, y_hg, y_s5, y_r, y_d, gates, w_up_hg, w_up_s5, w_up_ret, w_up_diff, w_out,
                   ln_m_g[l], ln_m_b[l], l)
        x = _ffn_ln(x, ffb_w1, ffb_w3, ffb_w2, ln_b_g[l], ln_b_b[l], l)
    return x.reshape(bsz, s, d)
```
